```python
import jax
import jax.numpy as jnp
from jax import lax
import numpy as np

D_MODEL = 1024
BATCH = 2
SEQ = 8192
DEPTH = 4
DEC_BATCH = 128
DEC_SEQ = 8
PAST_LEN = 2048
PAGE_SIZE = 128

HEAD_DIM = 64
ROT_DIM = HEAD_DIM // 4
ROPE_THETA = 500000.0
NSA_HEADS = 8
NSA_GROUPS = 2
NSA_HPG = NSA_HEADS // NSA_GROUPS
L_CMP = 32
L_SEL = 64
SEL_RATIO = L_SEL // L_CMP
TOP_N = 16
WINDOW = 512
FOX_HEADS = 4
GM_GROUPS = 4
GM_DIM = 64
GM_WIDTH = GM_GROUPS * GM_DIM
CHUNK = 128
Q_BLOCK = 128
D_FF = ((8 * D_MODEL // 3 + 127) // 128) * 128
WIDTH_A = NSA_HEADS * HEAD_DIM
WIDTH_B = FOX_HEADS * HEAD_DIM
KV_A = NSA_GROUPS * HEAD_DIM
IN_SPLITS = (WIDTH_A, KV_A, KV_A, KV_A, KV_A, KV_A, KV_A, 3 * NSA_HEADS,
             WIDTH_B, WIDTH_B, WIDTH_B, FOX_HEADS, GM_WIDTH, GM_WIDTH, 3 * D_MODEL)
N_IN = sum(IN_SPLITS)
EPS = 1e-6
NEG_BIG = -1e30
FORCE_SCORE = 1e4

kernel_name = 'nsa_fox_gmlp_macaron_decode_step'


def _rms(x, g):
    x32 = x.astype(jnp.float32)
    y = x32 * lax.rsqrt(jnp.mean(x32 * x32, axis=-1, keepdims=True) + EPS)
    return (y * g.astype(jnp.float32)).astype(x.dtype)


def _rope(x, pos):
    half = ROT_DIM // 2
    inv = ROPE_THETA ** (-jnp.arange(half, dtype=jnp.float32) / half)
    ang = pos.astype(jnp.float32)[:, None] * inv[None, :]
    cos = jnp.cos(ang)[:, None, :].astype(x.dtype)
    sin = jnp.sin(ang)[:, None, :].astype(x.dtype)
    x1 = x[..., :half]
    x2 = x[..., half:ROT_DIM]
    return jnp.concatenate([x1 * cos - x2 * sin, x2 * cos + x1 * sin, x[..., ROT_DIM:]], axis=-1)


def _swiglu(x, w_gu, w_down):
    g, u = jnp.split(x @ w_gu, 2, axis=-1)
    return (jax.nn.silu(g) * u) @ w_down


def _masked_softmax(s, mask):
    mask = jnp.broadcast_to(mask, s.shape)
    p = jax.nn.softmax(jnp.where(mask, s, NEG_BIG), axis=-1)
    return jnp.where(jnp.any(mask, axis=-1, keepdims=True), p, 0.0)


def _project(h, pos, lw):
    B, T, _ = h.shape
    cuts = np.cumsum(IN_SPLITS)[:-1].tolist()
    (qa, kc, vc, ks, vs, kw, vw, ga, qb, kb, vb, fl, gu, gv, mg) = jnp.split(h @ lw['w_in'], cuts, axis=-1)
    gn = lw['qk_gain']

    def heads(t, n):
        return t.reshape(B, T, n, HEAD_DIM)

    def qk(t, n, i, rot):
        t = _rms(heads(t, n), gn[i])
        return _rope(t, pos) if rot else t

    gm_v = _rms(jax.nn.gelu(gv).reshape(B, T, GM_GROUPS, GM_DIM), lw['gm_norm'].reshape(GM_GROUPS, GM_DIM))
    return {
        'q_a': qk(qa, NSA_HEADS, 0, True).reshape(B, T, NSA_GROUPS, NSA_HPG, HEAD_DIM),
        'kc': qk(kc, NSA_GROUPS, 1, True), 'vc': heads(vc, NSA_GROUPS),
        'ks': qk(ks, NSA_GROUPS, 2, True), 'vs': heads(vs, NSA_GROUPS),
        'kw': qk(kw, NSA_GROUPS, 3, True), 'vw': heads(vw, NSA_GROUPS),
        'g_a': jax.nn.sigmoid(ga).reshape(B, T, NSA_GROUPS, NSA_HPG, 3),
        'q_b': qk(qb, FOX_HEADS, 4, False), 'k_b': qk(kb, FOX_HEADS, 5, False), 'v_b': heads(vb, FOX_HEADS),
        'logf': jax.nn.log_sigmoid((fl + lw['b_forget']).astype(jnp.float32)),
        'u': jax.nn.gelu(gu), 'v': gm_v,
        'mg': jax.nn.sigmoid(mg).reshape(B, T, 3, D_MODEL),
    }


def _compress(rows, w, pe):
    B, S = rows.shape[:2]
    nc = S // L_CMP
    blk = rows[:, :nc * L_CMP].reshape(B, nc, L_CMP, NSA_GROUPS, HEAD_DIM) + pe[:, None, :]
    return jnp.einsum('bnlgd,lde->bnge', blk, w)


def _nsa_core(q, q_pos, g, kc_blk, vc_blk, ks_blk, vs_blk, kw, vw, kw_pos):
    B, Tq = q.shape[:2]
    scale = HEAD_DIM ** -0.5
    nc = kc_blk.shape[1]
    ns = ks_blk.shape[2]
    s_c = jnp.einsum('btghd,bngd->bghtn', q, kc_blk).astype(jnp.float32) * scale
    m_c = (jnp.arange(nc)[None, :] + 1) * L_CMP - 1 <= q_pos[:, None]
    p_c = _masked_softmax(s_c, m_c)
    o_c = jnp.einsum('bghtn,bngd->btghd', p_c.astype(vc_blk.dtype), vc_blk)
    imp = jnp.sum(p_c, axis=2)
    imp = jnp.pad(imp, ((0, 0), (0, 0), (0, 0), (0, ns * SEL_RATIO - nc)))
    imp = imp.reshape(B, NSA_GROUPS, Tq, ns, SEL_RATIO).sum(-1)
    blk = jnp.arange(ns)[None, :]
    cur = (q_pos // L_SEL)[:, None]
    forced = (blk == 0) | (blk == cur) | (blk == cur - 1)
    score = jnp.where(blk <= cur, jnp.where(forced, FORCE_SCORE, imp), -jnp.inf)
    _, idx = lax.top_k(score, min(TOP_N, ns))
    gather = jax.vmap(jax.vmap(lambda kb, ix: kb[ix]))
    ks_g = gather(ks_blk, idx)
    vs_g = gather(vs_blk, idx)
    s_s = jnp.einsum('btghd,bgtkld->bghtkl', q, ks_g).astype(jnp.float32) * scale
    pos_s = idx[..., None] * L_SEL + jnp.arange(L_SEL)
    m_s = (pos_s <= q_pos[:, None, None])[:, :, None]
    sh = s_s.shape
    p_s = _masked_softmax(s_s.reshape(sh[0], sh[1], sh[2], sh[3], -1),
                          jnp.broadcast_to(m_s, sh).reshape(sh[0], sh[1], sh[2], sh[3], -1)).reshape(sh)
    o_s = jnp.einsum('bghtkl,bgtkld->btghd', p_s.astype(vs_g.dtype), vs_g)
    s_w = jnp.einsum('btghd,bsgd->bghts', q, kw).astype(jnp.float32) * scale
    dpos = q_pos[:, None] - kw_pos[None, :]
    m_w = (dpos >= 0) & (dpos < WINDOW) & (kw_pos[None, :] >= 0)
    p_w = _masked_softmax(s_w, m_w)
    o_w = jnp.einsum('bghts,bsgd->btghd', p_w.astype(vw.dtype), vw)
    out = g[..., 0:1] * o_c + g[..., 1:2] * o_s + g[..., 2:3] * o_w
    return out.reshape(B, Tq, WIDTH_A)


def _sel_blocks(rows):
    B, S = rows.shape[:2]
    ns = -(-S // L_SEL)
    rows = jnp.pad(rows, ((0, 0), (0, ns * L_SEL - S), (0, 0), (0, 0)))
    return rows.reshape(B, ns, L_SEL, NSA_GROUPS, HEAD_DIM).transpose(0, 3, 1, 2, 4)


def _nsa_prompt(q, g, kc, vc, ks, vs, kw, vw, w_cmp, pe_cmp):
    B, T = q.shape[:2]
    kc_blk = _compress(kc, w_cmp[0], pe_cmp[0])
    vc_blk = _compress(vc, w_cmp[1], pe_cmp[1])
    ks_blk = _sel_blocks(ks)
    vs_blk = _sel_blocks(vs)
    pad = ((0, 0), (WINDOW, 0), (0, 0), (0, 0))
    kw_pad = jnp.pad(kw, pad)
    vw_pad = jnp.pad(vw, pad)
    nq = T // Q_BLOCK
    qb = q.reshape(B, nq, Q_BLOCK, NSA_GROUPS, NSA_HPG, HEAD_DIM).swapaxes(0, 1)
    gb = g.reshape(B, nq, Q_BLOCK, NSA_GROUPS, NSA_HPG, 3).swapaxes(0, 1)

    def step(args):
        i, q_i, g_i = args
        start = i * Q_BLOCK
        q_pos = start + jnp.arange(Q_BLOCK)
        kw_i = lax.dynamic_slice_in_dim(kw_pad, start, WINDOW + Q_BLOCK, axis=1)
        vw_i = lax.dynamic_slice_in_dim(vw_pad, start, WINDOW + Q_BLOCK, axis=1)
        kw_pos = start - WINDOW + jnp.arange(WINDOW + Q_BLOCK)
        return _nsa_core(q_i, q_pos, g_i, kc_blk, vc_blk, ks_blk, vs_blk, kw_i, vw_i, kw_pos)

    out = lax.map(step, (jnp.arange(nq), qb, gb))
    return out.swapaxes(0, 1).reshape(B, T, WIDTH_A)


def _nsa_sample(q, g, q_pos, kc_all, vc_all, ks_all, vs_all, kw_ctx, vw_ctx, kw_pos, w_cmp, pe_cmp):
    kc_blk = _compress(kc_all, w_cmp[0], pe_cmp[0])
    vc_blk = _compress(vc_all, w_cmp[1], pe_cmp[1])
    ks_blk = _sel_blocks(ks_all)
    vs_blk = _sel_blocks(vs_all)

    def step(args):
        q_t, g_t, p_t = args
        return _nsa_core(q_t[:, None], p_t[None], g_t[:, None], kc_blk, vc_blk, ks_blk, vs_blk,
                         kw_ctx, vw_ctx, kw_pos)[:, 0]

    out = lax.map(step, (q.swapaxes(0, 1), g.swapaxes(0, 1), q_pos))
    return out.swapaxes(0, 1)


def _fox_core(q, q_pos, fq, k, v, fk, k_pos):
    B, Tq = q.shape[:2]
    s = jnp.einsum('bthd,bshd->bhts', q, k).astype(jnp.float32) * (HEAD_DIM ** -0.5)
    s = s + (jnp.swapaxes(fq, 1, 2)[..., :, None] - jnp.swapaxes(fk, 1, 2)[..., None, :])
    p = _masked_softmax(s, k_pos[None, :] <= q_pos[:, None])
    return jnp.einsum('bhts,bshd->bthd', p.astype(v.dtype), v).reshape(B, Tq, WIDTH_B)


def _fox_prompt(q, k, v, logf):
    B, T = q.shape[:2]
    F = jnp.cumsum(logf, axis=1)
    nq = T // Q_BLOCK
    k_pos = jnp.arange(T)
    qb = q.reshape(B, nq, Q_BLOCK, FOX_HEADS, HEAD_DIM).swapaxes(0, 1)
    fb = F.reshape(B, nq, Q_BLOCK, FOX_HEADS).swapaxes(0, 1)

    def step(args):
        i, q_i, f_i = args
        return _fox_core(q_i, i * Q_BLOCK + jnp.arange(Q_BLOCK), f_i, k, v, F, k_pos)

    out = lax.map(step, (jnp.arange(nq), qb, fb))
    return out.swapaxes(0, 1).reshape(B, T, WIDTH_B)


def _fox_sample(q, q_pos, k_all, v_all, logf_all):
    S = k_all.shape[1]
    Tn = q.shape[1]
    F = jnp.cumsum(logf_all, axis=1)
    return _fox_core(q, q_pos, F[:, S - Tn:], k_all, v_all, F, jnp.arange(S))


def _gmlp(u, v, w_s, b_s):
    B, T = v.shape[:2]
    c = min(CHUNK, T)
    w = jnp.tril(w_s[:, :c, :c])
    vb = v.reshape(B, T // c, c, GM_GROUPS, GM_DIM)
    s = jnp.einsum('gts,bnsgd->bntgd', w, vb) + b_s[:, :c].T[None, None, :, :, None]
    return u * s.reshape(B, T, GM_WIDTH)


def _mix_inputs(x, pos, lw):
    x = x + 0.5 * _swiglu(_rms(x, lw['g_ffn_a']), lw['w_ffn_a_gu'], lw['w_ffn_a_down'])
    return x, _project(_rms(x, lw['g_mix']), pos, lw)


def _mix_outputs(x, o_a, o_b, o_c, mg, lw):
    m = (mg[..., 0, :] * (o_a @ lw['w_branch_a']) + mg[..., 1, :] * (o_b @ lw['w_branch_b'])
         + mg[..., 2, :] * (o_c @ lw['w_branch_c']))
    x = x + m @ lw['w_out']
    return x + 0.5 * _swiglu(_rms(x, lw['g_ffn_b']), lw['w_ffn_b_gu'], lw['w_ffn_b_down'])


def _layer_prompt(x, lw):
    T = x.shape[1]
    x, pr = _mix_inputs(x, jnp.arange(T), lw)
    o_a = _nsa_prompt(pr['q_a'], pr['g_a'], pr['kc'], pr['vc'], pr['ks'], pr['vs'], pr['kw'], pr['vw'],
                      lw['w_cmp'], lw['pe_cmp'])
    o_b = _fox_prompt(pr['q_b'], pr['k_b'], pr['v_b'], pr['logf'])
    o_c = _gmlp(pr['u'], pr['v'], lw['w_spatial'], lw['b_spatial'])
    x = _mix_outputs(x, o_a, o_b, o_c, pr['mg'], lw)
    nsa_rows = jnp.stack([pr['kc'], pr['vc'], pr['ks'], pr['vs']], axis=2)
    win_rows = jnp.stack([pr['kw'], pr['vw']], axis=2)[:, T - min(WINDOW, T):]
    fox_rows = jnp.stack([pr['k_b'], pr['v_b']], axis=2)
    return x, nsa_rows, win_rows, fox_rows, pr['logf']


def _layer_sample(x, nsa_past, fox_past, logf_past, win_state, lw):
    B, Tn = x.shape[:2]
    P = nsa_past.shape[1]
    pos = P + jnp.arange(Tn)
    x, pr = _mix_inputs(x, pos, lw)
    nsa_new = jnp.stack([pr['kc'], pr['vc'], pr['ks'], pr['vs']], axis=2)
    nsa_all = jnp.concatenate([nsa_past.astype(nsa_new.dtype), nsa_new], axis=1)
    win_new = jnp.stack([pr['kw'], pr['vw']], axis=2)
    wb = win_state.shape[1]
    win_ctx = jnp.concatenate([win_state.astype(win_new.dtype), win_new], axis=1)
    kw_pos = P - wb + jnp.arange(wb + Tn)
    o_a = _nsa_sample(pr['q_a'], pr['g_a'], pos, nsa_all[:, :, 0], nsa_all[:, :, 1], nsa_all[:, :, 2],
                      nsa_all[:, :, 3], win_ctx[:, :, 0], win_ctx[:, :, 1], kw_pos, lw['w_cmp'], lw['pe_cmp'])
    fox_new = jnp.stack([pr['k_b'], pr['v_b']], axis=2)
    fox_all = jnp.concatenate([fox_past.astype(fox_new.dtype), fox_new], axis=1)
    logf_all = jnp.concatenate([logf_past.astype(jnp.float32), pr['logf']], axis=1)
    o_b = _fox_sample(pr['q_b'], pos, fox_all[:, :, 0], fox_all[:, :, 1], logf_all)
    o_c = _gmlp(pr['u'], pr['v'], lw['w_spatial'], lw['b_spatial'])
    x = _mix_outputs(x, o_a, o_b, o_c, pr['mg'], lw)
    return x, nsa_new, win_ctx[:, Tn:], fox_new, pr['logf'], pr['v'].reshape(B, Tn, GM_WIDTH)


def _gather_pages(pool, layer, page_table):
    g = pool[layer, page_table]
    return g.reshape((g.shape[0], g.shape[1] * g.shape[2]) + g.shape[3:])


def setup_inputs(seed: int = 0) -> dict:
    key = jax.random.key(seed)
    ks = jax.random.split(key, 32)
    f32 = jnp.float32
    n_pages = PAST_LEN // PAGE_SIZE
    n_used = DEC_BATCH * n_pages
    n_pool = n_used + n_used // 4
    w_buf = min(WINDOW, PAST_LEN)

    def nrm(k, shape, scale=1.0):
        return jax.random.normal(k, shape, f32) * scale

    def gain(k, shape):
        return 1.0 + 0.05 * jax.random.normal(k, shape, f32)

    page_table = jax.random.permutation(ks[0], n_pool)[:n_used].reshape(DEC_BATCH, n_pages).astype(jnp.int32)
    return {
        'x_prompt': nrm(ks[1], (BATCH, SEQ, D_MODEL)),
        'x_sample': nrm(ks[2], (DEC_BATCH, DEC_SEQ, D_MODEL)),
        'cache_nsa_kv': nrm(ks[3], (DEPTH, n_pool, PAGE_SIZE, 4, NSA_GROUPS, HEAD_DIM)),
        'cache_fox_kv': nrm(ks[4], (DEPTH, n_pool, PAGE_SIZE, 2, FOX_HEADS, HEAD_DIM)),
        'cache_fox_logf': jax.nn.log_sigmoid(2.5 + nrm(ks[5], (DEPTH, n_pool, PAGE_SIZE, FOX_HEADS))),
        'state_nsa_win': nrm(ks[6], (DEPTH, DEC_BATCH, w_buf, 2, NSA_GROUPS, HEAD_DIM)),
        'page_table': page_table,
        'g_ffn_a': gain(ks[7], (DEPTH, D_MODEL)),
        'w_ffn_a_gu': nrm(ks[8], (DEPTH, D_MODEL, 2 * D_FF), D_MODEL ** -0.5),
        'w_ffn_a_down': nrm(ks[9], (DEPTH, D_FF, D_MODEL), D_FF ** -0.5),
        'g_mix': gain(ks[10], (DEPTH, D_MODEL)),
        'w_in': nrm(ks[11], (DEPTH, D_MODEL, N_IN), D_MODEL ** -0.5),
        'b_forget': jax.random.uniform(ks[12], (DEPTH, FOX_HEADS), f32, 1.0, 4.0),
        'qk_gain': gain(ks[13], (DEPTH, 6, HEAD_DIM)),
        'w_cmp': nrm(ks[14], (DEPTH, 2, L_CMP, HEAD_DIM, HEAD_DIM), (L_CMP * HEAD_DIM) ** -0.5),
        'pe_cmp': nrm(ks[15], (DEPTH, 2, L_CMP, HEAD_DIM), 0.1),
        'gm_norm': gain(ks[16], (DEPTH, GM_WIDTH)),
        'w_spatial': nrm(ks[17], (DEPTH, GM_GROUPS, CHUNK, CHUNK), CHUNK ** -0.5),
        'b_spatial': 1.0 + nrm(ks[18], (DEPTH, GM_GROUPS, CHUNK), 0.1),
        'w_branch_a': nrm(ks[19], (DEPTH, WIDTH_A, D_MODEL), WIDTH_A ** -0.5),
        'w_branch_b': nrm(ks[20], (DEPTH, WIDTH_B, D_MODEL), WIDTH_B ** -0.5),
        'w_branch_c': nrm(ks[21], (DEPTH, GM_WIDTH, D_MODEL), GM_WIDTH ** -0.5),
        'w_out': nrm(ks[22], (DEPTH, D_MODEL, D_MODEL), D_MODEL ** -0.5),
        'g_ffn_b': gain(ks[23], (DEPTH, D_MODEL)),
        'w_ffn_b_gu': nrm(ks[24], (DEPTH, D_MODEL, 2 * D_FF), D_MODEL ** -0.5),
        'w_ffn_b_down': nrm(ks[25], (DEPTH, D_FF, D_MODEL), D_FF ** -0.5),
    }


def reference(x_prompt, x_sample, cache_nsa_kv, cache_fox_kv, cache_fox_logf, state_nsa_win, page_table,
              g_ffn_a, w_ffn_a_gu, w_ffn_a_down, g_mix, w_in, b_forget, qk_gain, w_cmp, pe_cmp,
              gm_norm, w_spatial, b_spatial, w_branch_a, w_branch_b, w_branch_c, w_out,
              g_ffn_b, w_ffn_b_gu, w_ffn_b_down):
    xp = x_prompt
    xs = x_sample
    nsa_p, nsa_s, win_p, win_s = [], [], [], []
    fox_p, fox_s, lf_p, lf_s, gmv_s = [], [], [], [], []
    for l in range(DEPTH):
        lw = {
            'g_ffn_a': g_ffn_a[l], 'w_ffn_a_gu': w_ffn_a_gu[l], 'w_ffn_a_down': w_ffn_a_down[l],
            'g_mix': g_mix[l], 'w_in': w_in[l], 'b_forget': b_forget[l], 'qk_gain': qk_gain[l],
            'w_cmp': w_cmp[l], 'pe_cmp': pe_cmp[l], 'gm_norm': gm_norm[l],
            'w_spatial': w_spatial[l], 'b_spatial': b_spatial[l],
            'w_branch_a': w_branch_a[l], 'w_branch_b': w_branch_b[l], 'w_branch_c': w_branch_c[l],
            'w_out': w_out[l], 'g_ffn_b': g_ffn_b[l], 'w_ffn_b_gu': w_ffn_b_gu[l],
            'w_ffn_b_down': w_ffn_b_down[l],
        }
        xp, a, b, c, d = _layer_prompt(xp, lw)
        nsa_p.append(a)
        win_p.append(b)
        fox_p.append(c)
        lf_p.append(d)
        nsa_past = _gather_pages(cache_nsa_kv, l, page_table)
        fox_past = _gather_pages(cache_fox_kv, l, page_table)
        lf_past = _gather_pages(cache_fox_logf, l, page_table)
        xs, a, b, c, d, e = _layer_sample(xs, nsa_past, fox_past, lf_past, state_nsa_win[l], lw)
        nsa_s.append(a)
        win_s.append(b)
        fox_s.append(c)
        lf_s.append(d)
        gmv_s.append(e)
    return (xp, xs, jnp.stack(nsa_p), jnp.stack(nsa_s), jnp.stack(win_p), jnp.stack(win_s),
            jnp.stack(fox_p), jnp.stack(fox_s), jnp.stack(lf_p), jnp.stack(lf_s), jnp.stack(gmv_s))
```

```python
import functools

import numpy as np
import jax
import jax.numpy as jnp
from jax import lax
from jax.experimental import pallas as pl
from jax.experimental.pallas import tpu as pltpu

F32 = jnp.float32
BF16 = jnp.bfloat16

HEAD_DIM = 64
ROT_DIM = HEAD_DIM // 4
ROPE_THETA = 500000.0
NSA_HEADS = 8
NSA_GROUPS = 2
NSA_HPG = NSA_HEADS // NSA_GROUPS
L_CMP = 32
L_SEL = 64
TOP_N = 16
WINDOW = 512
FOX_HEADS = 4
GM_GROUPS = 4
GM_DIM = 64
GM_WIDTH = GM_GROUPS * GM_DIM
CHUNK = 128
PAGE = 128
EPS = 1e-6
NEG_BIG = -1e30
WIDTH_A = NSA_HEADS * HEAD_DIM
WIDTH_B = FOX_HEADS * HEAD_DIM
KV_A = NSA_GROUPS * HEAD_DIM
QK_SCALE = HEAD_DIM ** -0.5

LANES = 128
Q_BLOCK = 128
KEY_TILE = 256
FOX_Q_BLOCK = 512
MASK_BIG = float(2 ** 30)
FORCE_BASE = 30000.0
VMEM_LIMIT = 56 * 1024 * 1024

P_QA = 0
P_KV = 512
P_FOX = 1280
P_GM = 2048
P_MISC = 2560
P_COLS = 2688
R_GQ, R_GKC, R_GKS, R_GKW, R_GQB, R_GKB, R_GMN, R_BF, R_COLS = 0, 512, 640, 768, 896, 1152, 1408, 1664, 1792
GATE_LANE0 = FOX_HEADS


def _dot(a, b):
    return jnp.dot(a, b, preferred_element_type=F32)


def _dot_nt(a, b):
    return lax.dot_general(a, b, (((1,), (1,)), ((), ())), preferred_element_type=F32)


def _split3(x):
    a = x.astype(BF16)
    r = x - a.astype(F32)
    b = r.astype(BF16)
    c = (r - b.astype(F32)).astype(BF16)
    return a, b, c


def _rms_rows(x, g):
    return x * lax.rsqrt(jnp.mean(x * x, axis=-1, keepdims=True) + EPS) * g


def _cparams(sem):
    return pltpu.CompilerParams(dimension_semantics=sem, vmem_limit_bytes=VMEM_LIMIT)


def _ffn_kernel(x_ref, g_ref, wgu_ref, wd_ref, o_ref, acc_ref, *, d_ff, chunk):
    x = x_ref[...]
    h = _rms_rows(x, g_ref[...]).astype(BF16)
    acc_ref[...] = jnp.zeros_like(acc_ref)
    for c in range(d_ff // chunk):
        g = _dot(h, wgu_ref[:, c * chunk:(c + 1) * chunk])
        u = _dot(h, wgu_ref[:, d_ff + c * chunk:d_ff + (c + 1) * chunk])
        a = (jax.nn.silu(g) * u).astype(BF16)
        acc_ref[...] += _dot(a, wd_ref[c * chunk:(c + 1) * chunk, :])
    o_ref[...] = x + 0.5 * acc_ref[...]


def _ffn(x, g, wgu, wd, tm):
    m, d = x.shape
    d_ff = wd.shape[0]
    return pl.pallas_call(
        functools.partial(_ffn_kernel, d_ff=d_ff, chunk=256),
        grid=(m // tm,),
        in_specs=[pl.BlockSpec((tm, d), lambda i: (i, 0)),
                  pl.BlockSpec((1, d), lambda i: (0, 0)),
                  pl.BlockSpec((d, 2 * d_ff), lambda i: (0, 0)),
                  pl.BlockSpec((d_ff, d), lambda i: (0, 0))],
        out_specs=pl.BlockSpec((tm, d), lambda i: (i, 0)),
        out_shape=jax.ShapeDtypeStruct((m, d), F32),
        scratch_shapes=[pltpu.VMEM((tm, d), F32)],
        compiler_params=_cparams(("parallel",)),
        name="ffn",
    )(x, g, wgu, wd)


def _proj_kernel(x_ref, gmix_ref, w_ref, prm_ref, rope_ref, bd_ref, ltri_ref, e_ref, wmix_ref, btab_ref,
                 qa_ref, nsa_ref, win_ref, kvsel_ref, kvwin_ref, qb_ref, fox_ref, kvfox_ref, faug_ref,
                 misc_ref, oc_ref, v_ref, carry_ref, *, tm, tiles_per_seq):
    i = pl.program_id(0)
    x = x_ref[...]
    h = _rms_rows(x, gmix_ref[...]).astype(BF16)
    cos = rope_ref[:, 0:128]
    sin_lo = rope_ref[:, 128:256]
    sin_hi = rope_ref[:, 256:384]
    bd = bd_ref[...]
    lane = lax.broadcasted_iota(jnp.int32, (tm, LANES), 1)
    lo64 = lane < HEAD_DIM

    def seg(a, b):
        return _dot(h, w_ref[:, a:b])

    def headnorm(t, gain):
        hi, lo, _ = _split3(t * t)
        ms = _dot(hi, bd) + _dot(lo, bd)
        return t * lax.rsqrt(ms + EPS) * gain

    def rope(t):
        return t * cos + pltpu.roll(t, LANES - ROT_DIM // 2, 1) * sin_lo + pltpu.roll(t, ROT_DIM // 2, 1) * sin_hi

    def pair_pack(k, v):
        return (jnp.where(lo64, k, pltpu.roll(v, HEAD_DIM, 1)).astype(BF16),
                jnp.where(lo64, pltpu.roll(k, HEAD_DIM, 1), v).astype(BF16))

    def head_split(t):
        return (jnp.where(lo64, t, 0.0).astype(BF16),
                jnp.where(lo64, pltpu.roll(t, HEAD_DIM, 1), 0.0).astype(BF16))

    for c in range(WIDTH_A // LANES):
        t = seg(P_QA + c * LANES, P_QA + (c + 1) * LANES)
        t = rope(headnorm(t, prm_ref[:, R_GQ + c * LANES:R_GQ + (c + 1) * LANES])) * QK_SCALE
        a, b = head_split(t)
        qa_ref[:, (2 * c) * LANES:(2 * c + 1) * LANES] = a
        qa_ref[:, (2 * c + 1) * LANES:(2 * c + 2) * LANES] = b

    kc = rope(headnorm(seg(P_KV, P_KV + 128), prm_ref[:, R_GKC:R_GKC + 128]))
    vc = seg(P_KV + 128, P_KV + 256)
    ks = rope(headnorm(seg(P_KV + 256, P_KV + 384), prm_ref[:, R_GKS:R_GKS + 128]))
    vs = seg(P_KV + 384, P_KV + 512)
    kw = rope(headnorm(seg(P_KV + 512, P_KV + 640), prm_ref[:, R_GKW:R_GKW + 128]))
    vw = seg(P_KV + 640, P_KV + 768)
    nsa_ref[:, 0:128] = kc
    nsa_ref[:, 128:256] = vc
    nsa_ref[:, 256:384] = ks
    nsa_ref[:, 384:512] = vs
    win_ref[:, 0:128] = kw
    win_ref[:, 128:256] = vw
    a, b = pair_pack(ks, vs)
    kvsel_ref[:, 0:128] = a
    kvsel_ref[:, 128:256] = b
    a, b = pair_pack(kw, vw)
    kvwin_ref[:, 0:128] = a
    kvwin_ref[:, 128:256] = b

    for c in range(WIDTH_B // LANES):
        t = seg(P_FOX + c * LANES, P_FOX + (c + 1) * LANES)
        t = headnorm(t, prm_ref[:, R_GQB + c * LANES:R_GQB + (c + 1) * LANES]) * QK_SCALE
        a, b = head_split(t)
        qb_ref[:, (2 * c) * LANES:(2 * c + 1) * LANES] = a
        qb_ref[:, (2 * c + 1) * LANES:(2 * c + 2) * LANES] = b
        k = seg(P_FOX + WIDTH_B + c * LANES, P_FOX + WIDTH_B + (c + 1) * LANES)
        k = headnorm(k, prm_ref[:, R_GKB + c * LANES:R_GKB + (c + 1) * LANES])
        v = seg(P_FOX + 2 * WIDTH_B + c * LANES, P_FOX + 2 * WIDTH_B + (c + 1) * LANES)
        fox_ref[:, c * LANES:(c + 1) * LANES] = k
        fox_ref[:, WIDTH_B + c * LANES:WIDTH_B + (c + 1) * LANES] = v
        a, b = pair_pack(k, v)
        kvfox_ref[:, (2 * c) * LANES:(2 * c + 1) * LANES] = a
        kvfox_ref[:, (2 * c + 1) * LANES:(2 * c + 2) * LANES] = b

    lane_grp = lax.broadcasted_iota(jnp.int32, (CHUNK, GM_WIDTH), 1) // GM_DIM
    for c in range(GM_WIDTH // LANES):
        gv = jax.nn.gelu(seg(P_GM + GM_WIDTH + c * LANES, P_GM + GM_WIDTH + (c + 1) * LANES))
        v_ref[:, c * LANES:(c + 1) * LANES] = headnorm(gv, prm_ref[:, R_GMN + c * LANES:R_GMN + (c + 1) * LANES])
    for r in range(tm // CHUNK):
        rows = slice(r * CHUNK, (r + 1) * CHUNK)
        vsub = v_ref[rows, :]
        s = btab_ref[0]
        for g in range(GM_GROUPS):
            s = s + _dot(wmix_ref[0, g], jnp.where(lane_grp == g, vsub, 0.0).astype(BF16))
        u = jnp.concatenate(
            [jax.nn.gelu(_dot(h[rows, :], w_ref[:, P_GM + c * LANES:P_GM + (c + 1) * LANES]))
             for c in range(GM_WIDTH // LANES)], axis=1)
        oc_ref[rows, :] = (u * s).astype(BF16)

    t = seg(P_MISC, P_MISC + LANES)
    z = t + prm_ref[:, R_BF:R_BF + LANES]
    logf = jnp.minimum(z, 0.0) - jnp.log(1.0 + jnp.exp(-jnp.abs(z)))
    is_f = lane < FOX_HEADS
    misc_ref[...] = jnp.where(is_f, logf, jax.nn.sigmoid(t))

    @pl.when(i % tiles_per_seq == 0)
    def _():
        carry_ref[...] = jnp.zeros_like(carry_ref)

    ltri = ltri_ref[...]
    e = e_ref[...]
    cum = carry_ref[...]
    for part in _split3(jnp.where(is_f, logf, 0.0)):
        cum = cum + _dot(ltri, _dot(part, e).astype(BF16))
    carry_ref[...] = cum[tm - 1:tm, :]
    hi, mid, lo = _split3(-cum)
    third = lane % 3
    faug_ref[...] = jnp.where(third == 0, hi, jnp.where(third == 1, mid, lo))


def _proj(x, gmix, w, prm, rope, consts, tm, tiles_per_seq, n_prompt_tiles):
    m, d = x.shape
    bd, ltri, e, wmix, btab = consts
    row = lambda width: pl.BlockSpec((tm, width), lambda i: (i, 0))
    full = lambda a: pl.BlockSpec(a.shape, lambda i: (0,) * a.ndim)
    kind = lambda i: (i >= n_prompt_tiles).astype(jnp.int32)
    outs = [("qa", 2 * WIDTH_A, BF16), ("nsa", 4 * KV_A, F32), ("win", 2 * KV_A, F32),
            ("kvsel", 2 * KV_A, BF16), ("kvwin", 2 * KV_A, BF16), ("qb", 2 * WIDTH_B, BF16),
            ("fox", 2 * WIDTH_B, F32), ("kvfox", 2 * WIDTH_B, BF16), ("faug", LANES, BF16),
            ("misc", LANES, F32), ("oc", GM_WIDTH, BF16), ("v", GM_WIDTH, F32)]
    res = pl.pallas_call(
        functools.partial(_proj_kernel, tm=tm, tiles_per_seq=tiles_per_seq),
        grid=(m // tm,),
        in_specs=[row(d), full(gmix), full(w), full(prm), row(3 * LANES), full(bd), full(ltri), full(e),
                  pl.BlockSpec((1,) + wmix.shape[1:], lambda i: (kind(i), 0, 0, 0)),
                  pl.BlockSpec((1,) + btab.shape[1:], lambda i: (kind(i), 0, 0))],
        out_specs=[row(wd) for _, wd, _ in outs],
        out_shape=[jax.ShapeDtypeStruct((m, wd), dt) for _, wd, dt in outs],
        scratch_shapes=[pltpu.VMEM((1, LANES), F32)],
        compiler_params=_cparams(("arbitrary",)),
        name="proj",
    )(x, gmix, w, prm, rope, bd, ltri, e, wmix, btab)
    return dict(zip([n for n, _, _ in outs], res))


def _mix_kernel(x_ref, gmix_ref, wmg_ref, oa_ref, ob_ref, oc_ref, wa_ref, wb_ref, wc_ref, wout_ref, o_ref):
    x = x_ref[...]
    d = x.shape[1]
    h = _rms_rows(x, gmix_ref[...]).astype(BF16)
    m = jax.nn.sigmoid(_dot(h, wmg_ref[:, 0:d])) * _dot(oa_ref[...].astype(BF16), wa_ref[...])
    m = m + jax.nn.sigmoid(_dot(h, wmg_ref[:, d:2 * d])) * _dot(ob_ref[...].astype(BF16), wb_ref[...])
    m = m + jax.nn.sigmoid(_dot(h, wmg_ref[:, 2 * d:3 * d])) * _dot(oc_ref[...], wc_ref[...])
    o_ref[...] = x + _dot(m.astype(BF16), wout_ref[...])


def _mix(x, gmix, wmg, oa, ob, oc, wa, wb, wc, wout, tm):
    m, d = x.shape
    row = lambda a: pl.BlockSpec((tm, a.shape[1]), lambda i: (i, 0))
    full = lambda a: pl.BlockSpec(a.shape, lambda i: (0,) * a.ndim)
    return pl.pallas_call(
        _mix_kernel,
        grid=(m // tm,),
        in_specs=[row(x), full(gmix), full(wmg), row(oa), row(ob), row(oc), full(wa), full(wb), full(wc),
                  full(wout)],
        out_specs=row(x),
        out_shape=jax.ShapeDtypeStruct((m, d), F32),
        compiler_params=_cparams(("parallel",)),
        name="mix",
    )(x, gmix, wmg, oa, ob, oc, wa, wb, wc, wout)


def _compress_kernel(x_ref, pe_ref, w_ref, o_ref):
    o_ref[0] = _dot((x_ref[0] + pe_ref[...]).astype(BF16), w_ref[...])


def _compress_prompt(x2, pe2, w2):
    b, nc, k = x2.shape
    n = w2.shape[1]
    return pl.pallas_call(
        _compress_kernel,
        grid=(b,),
        in_specs=[pl.BlockSpec((1, nc, k), lambda i: (i, 0, 0)),
                  pl.BlockSpec((1, k), lambda i: (0, 0)),
                  pl.BlockSpec((k, n), lambda i: (0, 0))],
        out_specs=pl.BlockSpec((1, nc, n), lambda i: (i, 0, 0)),
        out_shape=jax.ShapeDtypeStruct((b, nc, n), F32),
        compiler_params=_cparams(("parallel",)),
        name="compress_prompt",
    )(x2, pe2, w2)


def _online_update(s, kv, acc_ref, m_ref):
    m_old = m_ref[...]
    m_new = jnp.maximum(m_old, jnp.max(s, axis=-1, keepdims=True))
    alpha = jnp.exp(m_old - m_new)
    p = jnp.exp(s - m_new)
    col = lax.broadcasted_iota(jnp.int32, kv.shape, 1)
    rhs = jnp.where(col == 0, jnp.ones_like(kv), kv)
    acc_ref[...] = alpha * acc_ref[...] + _dot(p.astype(BF16), rhs)
    m_ref[...] = m_new


def _select_blocks(imp_t, tok0):
    nb, nt = imp_t.shape
    blk = lax.broadcasted_iota(jnp.int32, (nb, nt), 0)
    cur = (tok0 + lax.broadcasted_iota(jnp.int32, (nb, nt), 1)) // L_SEL
    forced = (blk == 0) | (blk == cur) | (blk == cur - 1)
    score = jnp.where(blk <= cur, jnp.where(forced, FORCE_BASE - blk.astype(F32), imp_t), -jnp.inf)
    sel = jnp.zeros((nb, nt), F32)
    for _ in range(min(TOP_N, nb)):
        mx = jnp.max(score, axis=0, keepdims=True)
        first = jnp.min(jnp.where(score == mx, blk, nb), axis=0, keepdims=True)
        hit = (blk == first) & (mx > -jnp.inf)
        sel = jnp.where(hit, 1.0, sel)
        score = jnp.where(hit, -jnp.inf, score)
    return sel


def _nsa_prompt_kernel(q_ref, misc_ref, kcv_ref, kvsel_ref, kvwin_ref, o_ref, acc_ref, m_ref, *, n_cmp):
    i = pl.program_id(1)
    rows = NSA_HPG * Q_BLOCK
    tok0 = i * Q_BLOCK
    tok = tok0 + (lax.broadcasted_iota(jnp.int32, (rows, 1), 0) % Q_BLOCK)
    half = n_cmp // 2
    for g in range(NSA_GROUPS):
        gl = slice(g * LANES, (g + 1) * LANES)
        q_lo = jnp.concatenate(
            [q_ref[:, (g * NSA_HPG + h) * LANES:(g * NSA_HPG + h + 1) * LANES] for h in range(NSA_HPG)], axis=0)

        kcv = kcv_ref[0, :, gl].astype(BF16)
        s_c = _dot_nt(q_lo, kcv)
        col = lax.broadcasted_iota(jnp.int32, (1, n_cmp), 1)
        blk_c = 2 * (col % half) + col // half
        vis = (blk_c + 1) * L_CMP - 1 <= tok
        s_c = jnp.where(vis, s_c, NEG_BIG)
        e_c = jnp.exp(s_c - jnp.max(s_c, axis=-1, keepdims=True))
        p_c = e_c / jnp.sum(e_c, axis=-1, keepdims=True)
        p_c = jnp.where(tok >= L_CMP - 1, p_c, 0.0)
        o_c = _dot(p_c.astype(BF16), kcv)[:, HEAD_DIM:]
        imp = p_c[0:Q_BLOCK]
        for h in range(1, NSA_HPG):
            imp = imp + p_c[h * Q_BLOCK:(h + 1) * Q_BLOCK]
        imp_sel = imp[:, :half] + imp[:, half:]
        if half < LANES:
            imp_sel = jnp.concatenate([imp_sel, jnp.zeros((Q_BLOCK, LANES - half), F32)], axis=1)
        sel_t = _select_blocks(imp_sel.T, tok0)
        nsel = (1.0 - sel_t).T.astype(BF16)
        q_aug = jnp.concatenate([q_lo, jnp.concatenate([nsel] * NSA_HPG, axis=0)], axis=1)

        acc_ref[...] = jnp.zeros_like(acc_ref)
        m_ref[...] = jnp.full_like(m_ref, -jnp.inf)
        n_kt = (tok0 + Q_BLOCK + KEY_TILE - 1) // KEY_TILE

        def sel_tile(kt, causal):
            kv = kvsel_ref[pl.ds(pl.multiple_of(kt * KEY_TILE, KEY_TILE), KEY_TILE), gl]
            rk = lax.broadcasted_iota(jnp.int32, (KEY_TILE, LANES), 0)
            ck = lax.broadcasted_iota(jnp.int32, (KEY_TILE, LANES), 1)
            aug = jnp.where(ck == kt * (KEY_TILE // L_SEL) + rk // L_SEL, -MASK_BIG, 0.0).astype(BF16)
            s = _dot_nt(q_aug, jnp.concatenate([kv, aug], axis=1))
            if causal:
                keypos = kt * KEY_TILE + lax.broadcasted_iota(jnp.int32, (1, KEY_TILE), 1)
                s = jnp.where(keypos <= tok, s, -MASK_BIG)
            _online_update(s, kv, acc_ref, m_ref)

        def sel_body(kt, carry):
            sel_tile(kt, False)
            return carry

        lax.fori_loop(0, n_kt - 1, sel_body, 0)
        sel_tile(n_kt - 1, True)
        acc = acc_ref[...]
        o_s = acc[:, HEAD_DIM:] / acc[:, 0:1]

        acc_ref[...] = jnp.zeros_like(acc_ref)
        m_ref[...] = jnp.full_like(m_ref, -jnp.inf)
        for j in range(WINDOW // Q_BLOCK + 1):
            start = tok0 - WINDOW + j * Q_BLOCK
            kv = kvwin_ref[pl.ds(pl.multiple_of(jnp.maximum(start, 0), Q_BLOCK), Q_BLOCK), gl]
            dpos = tok - (start + lax.broadcasted_iota(jnp.int32, (1, Q_BLOCK), 1))
            ok = (dpos >= 0) & (dpos < WINDOW) & (start >= 0)
            s = jnp.where(ok, _dot_nt(q_lo, kv), -MASK_BIG)
            _online_update(s, kv, acc_ref, m_ref)
        acc = acc_ref[...]
        o_w = acc[:, HEAD_DIM:] / acc[:, 0:1]

        gates = misc_ref[...]
        pieces = []
        for h in range(NSA_HPG):
            hr = slice(h * Q_BLOCK, (h + 1) * Q_BLOCK)
            c0 = GATE_LANE0 + (g * NSA_HPG + h) * 3
            pieces.append(gates[:, c0:c0 + 1] * o_c[hr] + gates[:, c0 + 1:c0 + 2] * o_s[hr]
                          + gates[:, c0 + 2:c0 + 3] * o_w[hr])
        o_ref[:, g * NSA_HPG * HEAD_DIM:(g + 1) * NSA_HPG * HEAD_DIM] = jnp.concatenate(pieces, axis=1)


def _nsa_prompt(qa, misc, kcv, kvsel, kvwin, batch, seq):
    m = qa.shape[0]
    nq = seq // Q_BLOCK
    n_cmp = kcv.shape[1]
    return pl.pallas_call(
        functools.partial(_nsa_prompt_kernel, n_cmp=n_cmp),
        grid=(batch, nq),
        in_specs=[pl.BlockSpec((Q_BLOCK, 2 * WIDTH_A), lambda b, i: (b * nq + i, 0)),
                  pl.BlockSpec((Q_BLOCK, LANES), lambda b, i: (b * nq + i, 0)),
                  pl.BlockSpec((1, n_cmp, 2 * KV_A), lambda b, i: (b, 0, 0)),
                  pl.BlockSpec((seq, 2 * KV_A), lambda b, i: (b, 0)),
                  pl.BlockSpec((seq, 2 * KV_A), lambda b, i: (b, 0))],
        out_specs=pl.BlockSpec((Q_BLOCK, WIDTH_A), lambda b, i: (b * nq + i, 0)),
        out_shape=jax.ShapeDtypeStruct((m, WIDTH_A), F32),
        scratch_shapes=[pltpu.VMEM((NSA_HPG * Q_BLOCK, LANES), F32), pltpu.VMEM((NSA_HPG * Q_BLOCK, 1), F32)],
        compiler_params=_cparams(("parallel", "arbitrary")),
        name="nsa_prompt",
    )(qa, misc, kcv, kvsel, kvwin)


def _fox_prompt_kernel(q_ref, kv_ref, faug_ref, o_ref, acc_ref, m_ref, *, qb):
    i = pl.program_id(1)
    tok = i * qb + lax.broadcasted_iota(jnp.int32, (qb, 1), 0)
    lane = lax.broadcasted_iota(jnp.int32, (qb, LANES), 1)
    for h in range(FOX_HEADS):
        hl = slice(h * LANES, (h + 1) * LANES)
        ones = jnp.where((lane >= 3 * h) & (lane < 3 * h + 3), 1.0, 0.0).astype(BF16)
        q_aug = jnp.concatenate([q_ref[:, hl], ones], axis=1)
        acc_ref[...] = jnp.zeros_like(acc_ref)
        m_ref[...] = jnp.full_like(m_ref, -jnp.inf)

        def tile(kt, causal):
            ks = pl.ds(pl.multiple_of(kt * KEY_TILE, KEY_TILE), KEY_TILE)
            kv = kv_ref[ks, hl]
            s = _dot_nt(q_aug, jnp.concatenate([kv, faug_ref[ks, :]], axis=1))
            if causal:
                keypos = kt * KEY_TILE + lax.broadcasted_iota(jnp.int32, (1, KEY_TILE), 1)
                s = jnp.where(keypos <= tok, s, -MASK_BIG)
            _online_update(s, kv, acc_ref, m_ref)

        def body(kt, carry):
            tile(kt, False)
            return carry

        n_full = i * (qb // KEY_TILE)
        lax.fori_loop(0, n_full, body, 0)
        for d in range(qb // KEY_TILE):
            tile(n_full + d, True)
        acc = acc_ref[...]
        o_ref[:, h * HEAD_DIM:(h + 1) * HEAD_DIM] = acc[:, HEAD_DIM:] / acc[:, 0:1]


def _fox_prompt(qb_arr, kvfox, faug, batch, seq):
    m = qb_arr.shape[0]
    qb = min(FOX_Q_BLOCK, seq)
    nq = seq // qb
    return pl.pallas_call(
        functools.partial(_fox_prompt_kernel, qb=qb),
        grid=(batch, nq),
        in_specs=[pl.BlockSpec((qb, 2 * WIDTH_B), lambda b, i: (b * nq + i, 0)),
                  pl.BlockSpec((seq, 2 * WIDTH_B), lambda b, i: (b, 0)),
                  pl.BlockSpec((seq, LANES), lambda b, i: (b, 0))],
        out_specs=pl.BlockSpec((qb, WIDTH_B), lambda b, i: (b * nq + i, 0)),
        out_shape=jax.ShapeDtypeStruct((m, WIDTH_B), F32),
        scratch_shapes=[pltpu.VMEM((qb, LANES), F32), pltpu.VMEM((qb, 1), F32)],
        compiler_params=_cparams(("parallel", "arbitrary")),
        name="fox_prompt",
    )(qb_arr, kvfox, faug)


def _cmp_pool_kernel(x_ref, pe_ref, w_ref, o_ref):
    acc = jnp.zeros(o_ref.shape, F32)
    for l in range(L_CMP):
        acc = acc + _dot((x_ref[0, :, l, :] + pe_ref[l:l + 1, :]).astype(BF16), w_ref[l])
    o_ref[...] = acc


def _compress_pool(cache4, layer, pe_l, w_l, nb):
    n = cache4.shape[1]
    return pl.pallas_call(
        _cmp_pool_kernel,
        grid=(n // nb,),
        in_specs=[pl.BlockSpec((1, nb, L_CMP, 2 * KV_A), lambda i: (layer, i, 0, 0)),
                  pl.BlockSpec(pe_l.shape, lambda i: (0, 0)),
                  pl.BlockSpec(w_l.shape, lambda i: (0, 0, 0))],
        out_specs=pl.BlockSpec((nb, 2 * KV_A), lambda i: (i, 0)),
        out_shape=jax.ShapeDtypeStruct((n, 2 * KV_A), F32),
        compiler_params=_cparams(("parallel",)),
        name="compress_pool",
    )(cache4, pe_l, w_l)


def _softmax_pieces(pieces):
    mx = functools.reduce(jnp.maximum, [jnp.max(s, axis=-1, keepdims=True) for s in pieces])
    es = [jnp.exp(s - mx) for s in pieces]
    den = functools.reduce(lambda a, b: a + b, [jnp.sum(e, axis=-1, keepdims=True) for e in es])
    return [e / den for e in es]


def _nsa_sample_kernel(pt_ref, q_ref, misc_ref, new_ref, wnew_ref, state_ref, pair_ref, *rest, n_pages, past):
    del pt_ref
    pages = rest[:n_pages]
    cmps = rest[n_pages:2 * n_pages]
    o_ref, wout_ref = rest[2 * n_pages:]
    tn = q_ref.shape[0]
    rows = NSA_HEADS * tn
    rid = lax.broadcasted_iota(jnp.int32, (rows, 1), 0)
    qi = rid % tn
    tok = past + qi
    is_g1 = rid >= NSA_HPG * tn
    zeros_half = jnp.zeros((tn, 2 * HEAD_DIM), F32)

    q = q_ref[...]
    blocks = []
    for g in range(NSA_GROUPS):
        for h in range(NSA_HPG):
            c = q[:, (g * NSA_HPG + h) * LANES:(g * NSA_HPG + h + 1) * LANES]
            if g == 1:
                c = pltpu.roll(c, HEAD_DIM, 1)
            blocks.append(jnp.concatenate([c, zeros_half], axis=1))
    q64 = jnp.concatenate(blocks, axis=0).astype(BF16)

    def pick(x):
        return jnp.where(is_g1, x[:, 3 * HEAD_DIM:4 * HEAD_DIM], x[:, 2 * HEAD_DIM:3 * HEAD_DIM])

    def pad_keys(x):
        return jnp.concatenate([x, jnp.zeros((2 * tn - x.shape[0], x.shape[1]), x.dtype)], axis=0)

    n_cmp = n_pages * (PAGE // L_CMP)
    kcv = jnp.concatenate([c[0] for c in cmps], axis=0).astype(BF16)
    s_c = _dot_nt(q64, kcv)
    blk_c = lax.broadcasted_iota(jnp.int32, (1, n_cmp), 1)
    s_c = jnp.where((blk_c + 1) * L_CMP - 1 <= tok, s_c, NEG_BIG)
    e_c = jnp.exp(s_c - jnp.max(s_c, axis=-1, keepdims=True))
    p_c = e_c / jnp.sum(e_c, axis=-1, keepdims=True)
    p_c = jnp.where(tok >= L_CMP - 1, p_c, 0.0)
    o_c = pick(_dot(p_c.astype(BF16), kcv))

    lane = lax.broadcasted_iota(jnp.int32, (tn, LANES), 1)
    cur = (past + lax.broadcasted_iota(jnp.int32, (tn, LANES), 0)) // L_SEL
    forced = (lane == 0) | (lane == cur) | (lane == cur - 1)
    bias_rows = []
    for g in range(NSA_GROUPS):
        imp = p_c[g * NSA_HPG * tn:g * NSA_HPG * tn + tn]
        for h in range(1, NSA_HPG):
            imp = imp + p_c[(g * NSA_HPG + h) * tn:(g * NSA_HPG + h + 1) * tn]
        imp_sel = jnp.zeros((tn, LANES), F32)
        for part in _split3(imp):
            imp_sel = imp_sel + _dot(part, pair_ref[...])
        score = jnp.where(lane <= cur, jnp.where(forced, FORCE_BASE - lane.astype(F32), imp_sel), -jnp.inf)
        sel = jnp.zeros((tn, LANES), F32)
        for _ in range(TOP_N):
            mx = jnp.max(score, axis=-1, keepdims=True)
            first = jnp.min(jnp.where(score == mx, lane, LANES), axis=-1, keepdims=True)
            hit = (lane == first) & (mx > -jnp.inf)
            sel = jnp.where(hit, 1.0, sel)
            score = jnp.where(hit, -jnp.inf, score)
        bias_rows.append(jnp.where(sel > 0.0, 0.0, -MASK_BIG))
    bias_sel = jnp.concatenate([bias_rows[0]] * NSA_HPG + [bias_rows[1]] * NSA_HPG, axis=0)

    m_run = jnp.full((rows, 1), -jnp.inf, F32)
    l_run = jnp.zeros((rows, 1), F32)
    acc = jnp.zeros((rows, 4 * HEAD_DIM), F32)
    lane_lo = lax.broadcasted_iota(jnp.int32, (rows, PAGE), 1) < L_SEL
    new_kv = pad_keys(new_ref[:, 2 * KV_A:4 * KV_A]).astype(BF16)
    for j in range(n_pages + 1):
        if j < n_pages:
            kv = pages[j][0, 0].astype(BF16)
            bias = jnp.where(lane_lo, bias_sel[:, 2 * j:2 * j + 1], bias_sel[:, 2 * j + 1:2 * j + 2])
            s = _dot_nt(q64, kv) + bias
        else:
            kv = new_kv
            kcol = lax.broadcasted_iota(jnp.int32, (1, 2 * tn), 1)
            s = _dot_nt(q64, kv) + bias_sel[:, 2 * n_pages:2 * n_pages + 1]
            s = jnp.where((kcol <= qi) & (kcol < tn), s, -MASK_BIG)
        m_new = jnp.maximum(m_run, jnp.max(s, axis=-1, keepdims=True))
        alpha = jnp.exp(m_run - m_new)
        p = jnp.exp(s - m_new)
        l_run = alpha * l_run + jnp.sum(p, axis=-1, keepdims=True)
        acc = alpha * acc + _dot(p.astype(BF16), kv)
        m_run = m_new
    o_s = pick(acc) / l_run

    wb = state_ref.shape[2]
    st = state_ref[0, 0]
    wnew = wnew_ref[...]
    st_b = st.astype(BF16)
    wn_b = pad_keys(wnew).astype(BF16)
    kpos = past - wb + lax.broadcasted_iota(jnp.int32, (1, wb), 1)
    dpos = tok - kpos
    s1 = jnp.where((dpos >= 0) & (dpos < WINDOW) & (kpos >= 0), _dot_nt(q64, st_b), NEG_BIG)
    kcol = lax.broadcasted_iota(jnp.int32, (1, 2 * tn), 1)
    s2 = jnp.where((kcol <= qi) & (kcol < tn), _dot_nt(q64, wn_b), NEG_BIG)
    p1, p2 = _softmax_pieces([s1, s2])
    o_w = pick(_dot(p1.astype(BF16), st_b) + _dot(p2.astype(BF16), wn_b))
    wout_ref[0, 0, 0:wb - tn, :] = st[tn:wb, :]
    wout_ref[0, 0, wb - tn:wb, :] = wnew

    gates = misc_ref[...]
    pieces = []
    for hh in range(NSA_HEADS):
        hr = slice(hh * tn, (hh + 1) * tn)
        c0 = GATE_LANE0 + hh * 3
        pieces.append(gates[:, c0:c0 + 1] * o_c[hr] + gates[:, c0 + 1:c0 + 2] * o_s[hr]
                      + gates[:, c0 + 2:c0 + 3] * o_w[hr])
    o_ref[...] = jnp.concatenate(pieces, axis=1)


def _nsa_sample(page_table, layer, qa_s, misc_s, nsa_s, win_s, state4, pair, cache2, cmp_pool3, oa_full, n_prompt):
    nseq, n_pages = page_table.shape
    tn = qa_s.shape[0] // nseq
    past = n_pages * PAGE
    wb = state4.shape[2]
    row = lambda width: pl.BlockSpec((tn, width), lambda b, pt: (b, 0))
    page_specs = [pl.BlockSpec((1, 1, PAGE, 2 * KV_A), functools.partial(lambda b, pt, j: (layer, pt[b, j], 0, 1), j=j))
                  for j in range(n_pages)]
    cmp_specs = [pl.BlockSpec((1, PAGE // L_CMP, 2 * KV_A), functools.partial(lambda b, pt, j: (pt[b, j], 0, 0), j=j))
                 for j in range(n_pages)]
    grid_spec = pltpu.PrefetchScalarGridSpec(
        num_scalar_prefetch=1,
        grid=(nseq,),
        in_specs=[row(2 * WIDTH_A), row(LANES), row(4 * KV_A), row(2 * KV_A),
                  pl.BlockSpec((1, 1, wb, 2 * KV_A), lambda b, pt: (layer, b, 0, 0)),
                  pl.BlockSpec(pair.shape, lambda b, pt: (0, 0))] + page_specs + cmp_specs
                 + [pl.BlockSpec(memory_space=pl.ANY)],
        out_specs=[pl.BlockSpec((tn, WIDTH_A), lambda b, pt: (n_prompt // tn + b, 0)),
                   pl.BlockSpec((1, 1, wb, 2 * KV_A), lambda b, pt: (0, b, 0, 0))],
    )

    def body(pt_ref, *refs):
        ins = refs[:6 + 2 * n_pages]
        outs = refs[7 + 2 * n_pages:]
        _nsa_sample_kernel(pt_ref, *ins, *outs, n_pages=n_pages, past=past)

    n_in = 1 + 6 + 2 * n_pages
    return pl.pallas_call(
        body,
        grid_spec=grid_spec,
        out_shape=[jax.ShapeDtypeStruct(oa_full.shape, F32), jax.ShapeDtypeStruct((1, nseq, wb, 2 * KV_A), F32)],
        input_output_aliases={n_in: 0},
        compiler_params=_cparams(("arbitrary",)),
        name="nsa_sample",
    )(page_table, qa_s, misc_s, nsa_s, win_s, state4, pair, *([cache2] * n_pages), *([cmp_pool3] * n_pages), oa_full)


def _fox_sample_kernel(pt_ref, q_ref, new_ref, lfn_ref, ustrict_ref, *rest, n_pages):
    del pt_ref
    pages = rest[:n_pages]
    lfps = rest[n_pages:2 * n_pages]
    o_ref = rest[2 * n_pages]
    tn = q_ref.shape[0]
    rows = FOX_HEADS * tn
    qi = lax.broadcasted_iota(jnp.int32, (rows, 1), 0) % tn

    q = q_ref[...]
    zeros_half = jnp.zeros((tn, LANES), F32)
    blocks = []
    for h in range(FOX_HEADS):
        c = q[:, h * LANES:(h + 1) * LANES]
        if h % 2 == 1:
            c = pltpu.roll(c, HEAD_DIM, 1)
        blocks.append(jnp.concatenate([c, zeros_half] if h < 2 else [zeros_half, c], axis=1))
    q32 = jnp.concatenate(blocks, axis=0).astype(BF16)

    lf = jnp.concatenate([r[0, 0] for r in lfps], axis=0)
    within = jnp.zeros(lf.shape, F32)
    for part in _split3(lf):
        within = within + _dot(part, ustrict_ref[...])
    tot = jnp.sum(lf, axis=-1, keepdims=True)
    after = jnp.zeros((FOX_HEADS, 1), F32)
    page_bias = [None] * n_pages
    for j in reversed(range(n_pages)):
        b4 = within[j * FOX_HEADS:(j + 1) * FOX_HEADS] + after
        page_bias[j] = jnp.concatenate(
            [jnp.broadcast_to(b4[h:h + 1], (tn, PAGE)) for h in range(FOX_HEADS)], axis=0)
        after = after + tot[j * FOX_HEADS:(j + 1) * FOX_HEADS]

    lfn = lfn_ref[0]
    run = lfn[:, 0:1]
    cols = [run]
    for r in range(1, tn):
        run = run + lfn[:, r:r + 1]
        cols.append(run)
    cum = jnp.concatenate(cols + [jnp.zeros((FOX_HEADS, tn), F32)], axis=1)
    new_bias = jnp.concatenate(
        [jnp.broadcast_to(-cum[h:h + 1], (tn, 2 * tn)) for h in range(FOX_HEADS)], axis=0)

    m_run = jnp.full((rows, 1), -jnp.inf, F32)
    l_run = jnp.zeros((rows, 1), F32)
    acc = jnp.zeros((rows, WIDTH_B), F32)
    new = jnp.concatenate([new_ref[...], jnp.zeros((tn, 2 * WIDTH_B), F32)], axis=0).astype(BF16)
    for j in range(n_pages + 1):
        if j < n_pages:
            page = pages[j][0, 0]
            k = page[:, :WIDTH_B].astype(BF16)
            v = page[:, WIDTH_B:].astype(BF16)
            s = _dot_nt(q32, k) + page_bias[j]
        else:
            k = new[:, :WIDTH_B]
            v = new[:, WIDTH_B:]
            kcol = lax.broadcasted_iota(jnp.int32, (1, 2 * tn), 1)
            s = jnp.where((kcol <= qi) & (kcol < tn), _dot_nt(q32, k) + new_bias, -MASK_BIG)
        m_new = jnp.maximum(m_run, jnp.max(s, axis=-1, keepdims=True))
        alpha = jnp.exp(m_run - m_new)
        p = jnp.exp(s - m_new)
        l_run = alpha * l_run + jnp.sum(p, axis=-1, keepdims=True)
        acc = alpha * acc + _dot(p.astype(BF16), v)
        m_run = m_new
    o32 = acc / l_run
    lane_head = lax.broadcasted_iota(jnp.int32, (tn, WIDTH_B), 1) // HEAD_DIM
    out = jnp.zeros((tn, WIDTH_B), F32)
    for h in range(FOX_HEADS):
        out = out + jnp.where(lane_head == h, o32[h * tn:(h + 1) * tn], 0.0)
    o_ref[...] = out


def _fox_sample(page_table, layer, qb_s, fox_s, lfn_t, ustrict, cache2, logf_t, ob_full, n_prompt):
    nseq, n_pages = page_table.shape
    tn = qb_s.shape[0] // nseq
    row = lambda width: pl.BlockSpec((tn, width), lambda b, pt: (b, 0))
    page_specs = [pl.BlockSpec((1, 1, PAGE, 2 * WIDTH_B), functools.partial(lambda b, pt, j: (layer, pt[b, j], 0, 0), j=j))
                  for j in range(n_pages)]
    lf_specs = [pl.BlockSpec((1, 1, FOX_HEADS, PAGE), functools.partial(lambda b, pt, j: (layer, pt[b, j], 0, 0), j=j))
                for j in range(n_pages)]
    grid_spec = pltpu.PrefetchScalarGridSpec(
        num_scalar_prefetch=1,
        grid=(nseq,),
        in_specs=[row(2 * WIDTH_B), row(2 * WIDTH_B),
                  pl.BlockSpec((1, FOX_HEADS, tn), lambda b, pt: (b, 0, 0)),
                  pl.BlockSpec(ustrict.shape, lambda b, pt: (0, 0))] + page_specs + lf_specs
                 + [pl.BlockSpec(memory_space=pl.ANY)],
        out_specs=pl.BlockSpec((tn, WIDTH_B), lambda b, pt: (n_prompt // tn + b, 0)),
    )

    def body(pt_ref, *refs):
        ins = refs[:4 + 2 * n_pages]
        outs = refs[5 + 2 * n_pages:]
        _fox_sample_kernel(pt_ref, *ins, *outs, n_pages=n_pages)

    n_in = 1 + 4 + 2 * n_pages
    return pl.pallas_call(
        body,
        grid_spec=grid_spec,
        out_shape=jax.ShapeDtypeStruct(ob_full.shape, F32),
        input_output_aliases={n_in: 0},
        compiler_params=_cparams(("arbitrary",)),
        name="fox_sample",
    )(page_table, qb_s, fox_s, lfn_t, ustrict, *([cache2] * n_pages), *([logf_t] * n_pages), ob_full)


def _in_offsets():
    splits = (WIDTH_A, KV_A, KV_A, KV_A, KV_A, KV_A, KV_A, 3 * NSA_HEADS, WIDTH_B, WIDTH_B, WIDTH_B, FOX_HEADS,
              GM_WIDTH, GM_WIDTH)
    offs = np.concatenate([[0], np.cumsum(splits)])
    names = ("qa", "kc", "vc", "ks", "vs", "kw", "vw", "ga", "qb", "kb", "vb", "fl", "gu", "gv", "mg")
    return {n: int(o) for n, o in zip(names, offs)}


def _pack_params(w_in, qk_gain, gm_norm, b_forget, d_model):
    depth = w_in.shape[0]
    o = _in_offsets()
    cols = np.concatenate([
        np.arange(o["qa"], o["qa"] + WIDTH_A),
        np.arange(o["kc"], o["kc"] + 6 * KV_A),
        np.arange(o["qb"], o["qb"] + 3 * WIDTH_B),
        np.arange(o["gu"], o["gu"] + 2 * GM_WIDTH),
        np.arange(o["fl"], o["fl"] + FOX_HEADS),
        np.arange(o["ga"], o["ga"] + 3 * NSA_HEADS)])
    w_p = jnp.take(w_in, jnp.asarray(cols), axis=2)
    w_p = jnp.pad(w_p, ((0, 0), (0, 0), (0, P_COLS - w_p.shape[2]))).astype(BF16)
    w_mg = w_in[:, :, o["mg"]:o["mg"] + 3 * d_model].astype(BF16)
    tile = lambda g, n: jnp.tile(g, (1, n))
    prm = jnp.concatenate([
        tile(qk_gain[:, 0], NSA_HEADS), tile(qk_gain[:, 1], NSA_GROUPS), tile(qk_gain[:, 2], NSA_GROUPS),
        tile(qk_gain[:, 3], NSA_GROUPS), tile(qk_gain[:, 4], FOX_HEADS), tile(qk_gain[:, 5], FOX_HEADS),
        gm_norm, b_forget, jnp.zeros((depth, LANES - FOX_HEADS), F32)], axis=1)
    assert prm.shape[1] == R_COLS
    return w_p, w_mg, prm[:, None, :]


def _rope_table(pos):
    half = ROT_DIM // 2
    inv = ROPE_THETA ** (-jnp.arange(half, dtype=F32) / half)
    ang = pos.astype(F32)[:, None] * inv[None, :]
    cos, sin = jnp.cos(ang), jnp.sin(ang)
    n = pos.shape[0]
    zero = jnp.zeros((n, HEAD_DIM - ROT_DIM), F32)
    zero8 = jnp.zeros((n, half), F32)
    c64 = jnp.concatenate([cos, cos, zero + 1.0], axis=1)
    lo64 = jnp.concatenate([-sin, zero8, zero], axis=1)
    hi64 = jnp.concatenate([zero8, sin, zero], axis=1)
    rep = lambda t: jnp.tile(t, (1, LANES // HEAD_DIM))
    return jnp.concatenate([rep(c64), rep(lo64), rep(hi64)], axis=1)


def _const_tables(tm, tn, w_spatial, b_spatial):
    bd = np.kron(np.eye(LANES // HEAD_DIM), np.full((HEAD_DIM, HEAD_DIM), 1.0 / HEAD_DIM))
    ltri = np.tril(np.ones((tm, tm)))
    e = np.zeros((LANES, LANES))
    for h in range(FOX_HEADS):
        e[h, 3 * h:3 * h + 3] = 1.0
    consts = (jnp.asarray(bd, BF16), jnp.asarray(ltri, BF16), jnp.asarray(e, BF16))
    w_p = jnp.tril(w_spatial)
    eye = jnp.asarray(np.eye(CHUNK // tn), F32)
    w_s = jnp.einsum("ab,lgts->lgatbs", eye, jnp.tril(w_spatial[:, :, :tn, :tn]))
    w_s = w_s.reshape(w_spatial.shape[0], GM_GROUPS, CHUNK, CHUNK)
    wmix = jnp.stack([w_p, w_s], axis=1).astype(BF16)
    b_p = jnp.repeat(jnp.swapaxes(b_spatial, 1, 2), GM_DIM, axis=2)
    b_s = jnp.tile(b_p[:, :tn], (1, CHUNK // tn, 1))
    btab = jnp.stack([b_p, b_s], axis=1)
    return consts, wmix, btab


def _compress_weights(w_cmp, pe_cmp):
    depth = w_cmp.shape[0]
    eye_g = jnp.asarray(np.eye(NSA_GROUPS), F32)
    w_l = jnp.einsum("kq,gh,zklde->zlkgdqhe", jnp.asarray(np.eye(2), F32), eye_g, w_cmp)
    w_l = w_l.reshape(depth, L_CMP, 2 * KV_A, 2 * KV_A)
    pe_l = jnp.broadcast_to(pe_cmp[:, :, :, None, :], (depth, 2, L_CMP, NSA_GROUPS, HEAD_DIM))
    pe_l = jnp.transpose(pe_l, (0, 2, 1, 3, 4)).reshape(depth, L_CMP, 2 * KV_A)
    return w_l.astype(BF16), pe_l


def _group_major(x):
    s = x.shape[:-1]
    return jnp.swapaxes(x.reshape(s + (2, NSA_GROUPS, HEAD_DIM)), -3, -2).reshape(s + (2 * KV_A,))


def kernel(x_prompt, x_sample, cache_nsa_kv, cache_fox_kv, cache_fox_logf, state_nsa_win, page_table,
           g_ffn_a, w_ffn_a_gu, w_ffn_a_down, g_mix, w_in, b_forget, qk_gain, w_cmp, pe_cmp,
           gm_norm, w_spatial, b_spatial, w_branch_a, w_branch_b, w_branch_c, w_out,
           g_ffn_b, w_ffn_b_gu, w_ffn_b_down):
    batch, seq, d_model = x_prompt.shape
    nseq, tn, _ = x_sample.shape
    depth = w_in.shape[0]
    n_pool = cache_nsa_kv.shape[1]
    n_pages = page_table.shape[1]
    past = n_pages * PAGE
    wb = state_nsa_win.shape[2]
    n_prompt = batch * seq
    n_sample = nseq * tn
    tm = next(t for t in (512, 256, 128) if seq % t == 0 and n_sample % t == 0)
    assert seq % FOX_Q_BLOCK == 0 or seq < FOX_Q_BLOCK
    assert CHUNK % tn == 0 and past % L_SEL == 0 and wb >= tn

    x = jnp.concatenate([x_prompt.reshape(n_prompt, d_model), x_sample.reshape(n_sample, d_model)], axis=0)
    pos = jnp.concatenate([jnp.tile(jnp.arange(seq), batch), jnp.tile(past + jnp.arange(tn), nseq)])
    rope = _rope_table(pos)

    w_p, w_mg, prm = _pack_params(w_in, qk_gain, gm_norm, b_forget, d_model)
    consts, wmix, btab = _const_tables(tm, tn, w_spatial, b_spatial)
    w_l, pe_l = _compress_weights(w_cmp, pe_cmp)
    w2 = w_l.reshape(depth, L_CMP * 2 * KV_A, 2 * KV_A)
    w2 = _group_major(w2)
    pe2 = pe_l.reshape(depth, 1, L_CMP * 2 * KV_A)
    w_lg = _group_major(w_l)
    n_cmp_p = seq // L_CMP

    bf = lambda a: a.astype(BF16)
    wgu_a, wd_a, wgu_b, wd_b = bf(w_ffn_a_gu), bf(w_ffn_a_down), bf(w_ffn_b_gu), bf(w_ffn_b_down)
    wa, wb_, wc, wo = bf(w_branch_a), bf(w_branch_b), bf(w_branch_c), bf(w_out)

    cache_nsa2 = cache_nsa_kv.reshape(depth, n_pool, PAGE, 4 * KV_A)
    cache_nsa4 = cache_nsa_kv.reshape(depth, n_pool * (PAGE // L_CMP), L_CMP, 4 * KV_A)
    cache_fox2 = cache_fox_kv.reshape(depth, n_pool, PAGE, 2 * WIDTH_B)
    logf_t = jnp.swapaxes(cache_fox_logf, 2, 3)
    state4 = state_nsa_win.reshape(depth, nseq, wb, 2 * KV_A)
    n_blocks = n_pool * (PAGE // L_CMP)
    nb = next(t for t in (256, 128, 64, 32, 16, 8) if n_blocks % t == 0)
    n_sel = past // L_SEL + 1
    pair = np.zeros((n_pages * (PAGE // L_CMP), LANES))
    for c in range(pair.shape[0]):
        pair[c, c // 2] = 1.0
    pair = jnp.asarray(pair, BF16)
    ustrict = jnp.asarray(np.triu(np.ones((PAGE, PAGE)), 1).T, BF16)
    assert n_sel <= LANES

    outs = {k: [] for k in ("nsa", "win_p", "win_s", "fox", "logf", "gmv")}
    for l in range(depth):
        x = _ffn(x, g_ffn_a[l][None], wgu_a[l], wd_a[l], tm)
        pr = _proj(x, g_mix[l][None], w_p[l], prm[l], rope, consts + (wmix[l], btab[l]), tm, seq // tm,
                   n_prompt // tm)

        blocks = pr["nsa"][:n_prompt, :2 * KV_A].reshape(batch, n_cmp_p // 2, 2, L_CMP * 2 * KV_A)
        x2 = jnp.swapaxes(blocks, 1, 2).reshape(batch, n_cmp_p, L_CMP * 2 * KV_A)
        kcv = _compress_prompt(x2, pe2[l], w2[l])
        oa = _nsa_prompt(pr["qa"], pr["misc"], kcv, pr["kvsel"], pr["kvwin"], batch, seq)
        ob = _fox_prompt(pr["qb"], pr["kvfox"], pr["faug"], batch, seq)

        cmp_pool = _compress_pool(cache_nsa4, l, pe_l[l], w_l[l], nb)
        cmp_pool3 = cmp_pool.reshape(n_pool, PAGE // L_CMP, 2 * KV_A)
        sl = slice(n_prompt, None)
        oa, win_s = _nsa_sample(page_table, l, pr["qa"][sl].astype(F32), pr["misc"][sl], pr["nsa"][sl],
                                pr["win"][sl], state4, pair, cache_nsa2, cmp_pool3, oa, n_prompt)
        lfn_t = jnp.swapaxes(pr["misc"][sl, :FOX_HEADS].reshape(nseq, tn, FOX_HEADS), 1, 2)
        ob = _fox_sample(page_table, l, pr["qb"][sl].astype(F32), pr["fox"][sl], lfn_t, ustrict, cache_fox2,
                         logf_t, ob, n_prompt)

        x = _mix(x, g_mix[l][None], w_mg[l], oa, ob, pr["oc"], wa[l], wb_[l], wc[l], wo[l], tm)
        x = _ffn(x, g_ffn_b[l][None], wgu_b[l], wd_b[l], tm)

        outs["nsa"].append(pr["nsa"])
        outs["win_p"].append(pr["win"][:n_prompt].reshape(batch, seq, 2 * KV_A)[:, seq - min(WINDOW, seq):])
        outs["win_s"].append(win_s[0])
        outs["fox"].append(pr["fox"])
        outs["logf"].append(pr["misc"][:, :FOX_HEADS])
        outs["gmv"].append(pr["v"][sl])

    nsa = jnp.stack(outs["nsa"])
    fox = jnp.stack(outs["fox"])
    logf = jnp.stack(outs["logf"])
    wlen = min(WINDOW, seq)
    return (
        x[:n_prompt].reshape(batch, seq, d_model),
        x[n_prompt:].reshape(nseq, tn, d_model),
        nsa[:, :n_prompt].reshape(depth, batch, seq, 4, NSA_GROUPS, HEAD_DIM),
        nsa[:, n_prompt:].reshape(depth, nseq, tn, 4, NSA_GROUPS, HEAD_DIM),
        jnp.stack(outs["win_p"]).reshape(depth, batch, wlen, 2, NSA_GROUPS, HEAD_DIM),
        jnp.stack(outs["win_s"]).reshape(depth, nseq, wb, 2, NSA_GROUPS, HEAD_DIM),
        fox[:, :n_prompt].reshape(depth, batch, seq, 2, FOX_HEADS, HEAD_DIM),
        fox[:, n_prompt:].reshape(depth, nseq, tn, 2, FOX_HEADS, HEAD_DIM),
        logf[:, :n_prompt].reshape(depth, batch, seq, FOX_HEADS),
        logf[:, n_prompt:].reshape(depth, nseq, tn, FOX_HEADS),
        jnp.stack(outs["gmv"]).reshape(depth, nseq, tn, GM_WIDTH),
    )
```

```python
import functools

import numpy as np
import jax
import jax.numpy as jnp
from jax import lax
from jax.experimental import pallas as pl
from jax.experimental.pallas import tpu as pltpu

F32 = jnp.float32
BF16 = jnp.bfloat16

HEAD_DIM = 64
ROT_DIM = HEAD_DIM // 4
ROT_HALF = ROT_DIM // 2
ROPE_THETA = 500000.0
NSA_HEADS = 8
NSA_GROUPS = 2
NSA_HPG = NSA_HEADS // NSA_GROUPS
L_CMP = 32
L_SEL = 64
TOP_N = 16
WINDOW = 512
FOX_HEADS = 4
GM_GROUPS = 4
GM_DIM = 64
GM_WIDTH = GM_GROUPS * GM_DIM
CHUNK = 128
PAGE = 128
EPS = 1e-6
NEG_BIG = -1e30
WIDTH_A = NSA_HEADS * HEAD_DIM
WIDTH_B = FOX_HEADS * HEAD_DIM
KV_A = NSA_GROUPS * HEAD_DIM
QK_SCALE = HEAD_DIM ** -0.5

LANES = 128
SUBLANES = 8
Q_BLOCK = 128
KEY_TILE = 256
FOX_Q_BLOCK = 512
FAUG_ROWS = 16
MASK_BIG = float(2 ** 30)
FORCE_BASE = 30000.0
VMEM_LIMIT = 56 * 1024 * 1024

P_QA = 0
P_QB = 512
P_GM = 768
P_MISC = 1280
P_COLS = 1408
T_KC, T_VC, T_KS, T_VS, T_KW, T_VW, T_KB, T_VB, T_FL, T_ROWS = 0, 128, 256, 384, 512, 640, 768, 1024, 1280, 1288
R_GQ, R_GQB, R_GMN, R_BF, R_COLS = 0, 512, 768, 1024, 1152
GATE_LANE0 = FOX_HEADS


def _dot(a, b):
    return jnp.dot(a, b, preferred_element_type=F32)


def _dot_nt(a, b):
    return lax.dot_general(a, b, (((1,), (1,)), ((), ())), preferred_element_type=F32)


def _split3(x):
    a = x.astype(BF16)
    r = x - a.astype(F32)
    b = r.astype(BF16)
    c = (r - b.astype(F32)).astype(BF16)
    return a, b, c


def _rms_rows(x, g):
    return x * lax.rsqrt(jnp.mean(x * x, axis=-1, keepdims=True) + EPS) * g


def _log_sigmoid(z):
    return jnp.minimum(z, 0.0) - jnp.log(1.0 + jnp.exp(-jnp.abs(z)))


def _cparams(sem):
    return pltpu.CompilerParams(dimension_semantics=sem, vmem_limit_bytes=VMEM_LIMIT)


def _ffn_kernel(x_ref, g_ref, wgu_ref, wd_ref, o_ref, acc_ref, *, d_ff, chunk):
    x = x_ref[...]
    h = _rms_rows(x, g_ref[...]).astype(BF16)
    acc_ref[...] = jnp.zeros_like(acc_ref)
    for c in range(d_ff // chunk):
        g = _dot(h, wgu_ref[:, c * chunk:(c + 1) * chunk])
        u = _dot(h, wgu_ref[:, d_ff + c * chunk:d_ff + (c + 1) * chunk])
        a = (jax.nn.silu(g) * u).astype(BF16)
        acc_ref[...] += _dot(a, wd_ref[c * chunk:(c + 1) * chunk, :])
    o_ref[...] = x + 0.5 * acc_ref[...]


def _ffn(x, g, wgu, wd, tm):
    m, d = x.shape
    d_ff = wd.shape[0]
    return pl.pallas_call(
        functools.partial(_ffn_kernel, d_ff=d_ff, chunk=256),
        grid=(m // tm,),
        in_specs=[pl.BlockSpec((tm, d), lambda i: (i, 0)),
                  pl.BlockSpec((1, d), lambda i: (0, 0)),
                  pl.BlockSpec((d, 2 * d_ff), lambda i: (0, 0)),
                  pl.BlockSpec((d_ff, d), lambda i: (0, 0))],
        out_specs=pl.BlockSpec((tm, d), lambda i: (i, 0)),
        out_shape=jax.ShapeDtypeStruct((m, d), F32),
        scratch_shapes=[pltpu.VMEM((tm, d), F32)],
        compiler_params=_cparams(("parallel",)),
        name="ffn",
    )(x, g, wgu, wd)


def _proj_kernel(x_ref, gmix_ref, w_ref, wt_ref, prm_ref, col_ref, rope_ref, ropet_ref, bd_ref, wmix_ref, btab_ref,
                 qa_ref, qb_ref, misc_ref, oc_ref, kcvc_ref, nsat_ref, wint_ref, foxt_ref, faug_ref, logft_ref,
                 nsas_ref, wins_ref, foxs_ref, vs_ref, v_scr, carry_ref, *, tm, tiles_per_seq, n_prompt_tiles):
    i = pl.program_id(0)
    x = x_ref[...]
    h = _rms_rows(x, gmix_ref[...]).astype(BF16)
    lane = lax.broadcasted_iota(jnp.int32, (tm, LANES), 1)
    lo64 = lane < HEAD_DIM

    cos = rope_ref[:, 0:128]
    sin_lo = rope_ref[:, 128:256]
    sin_hi = rope_ref[:, 256:384]
    bd = bd_ref[...]

    def seg(a, b):
        return _dot(h, w_ref[:, a:b])

    def headnorm(t, gain):
        hi, lo, _ = _split3(t * t)
        ms = _dot(hi, bd) + _dot(lo, bd)
        return t * lax.rsqrt(ms + EPS) * gain

    def rope(t):
        return t * cos + pltpu.roll(t, LANES - ROT_HALF, 1) * sin_lo + pltpu.roll(t, ROT_HALF, 1) * sin_hi

    def head_split(t):
        return (jnp.where(lo64, t, 0.0).astype(BF16),
                jnp.where(lo64, pltpu.roll(t, HEAD_DIM, 1), 0.0).astype(BF16))

    for c in range(WIDTH_A // LANES):
        t = seg(P_QA + c * LANES, P_QA + (c + 1) * LANES)
        t = rope(headnorm(t, prm_ref[:, R_GQ + c * LANES:R_GQ + (c + 1) * LANES])) * QK_SCALE
        a, b = head_split(t)
        qa_ref[:, (2 * c) * LANES:(2 * c + 1) * LANES] = a
        qa_ref[:, (2 * c + 1) * LANES:(2 * c + 2) * LANES] = b
    for c in range(WIDTH_B // LANES):
        t = seg(P_QB + c * LANES, P_QB + (c + 1) * LANES)
        t = headnorm(t, prm_ref[:, R_GQB + c * LANES:R_GQB + (c + 1) * LANES]) * QK_SCALE
        a, b = head_split(t)
        qb_ref[:, (2 * c) * LANES:(2 * c + 1) * LANES] = a
        qb_ref[:, (2 * c + 1) * LANES:(2 * c + 2) * LANES] = b

    lane_grp = lax.broadcasted_iota(jnp.int32, (CHUNK, GM_WIDTH), 1) // GM_DIM
    for c in range(GM_WIDTH // LANES):
        gv = jax.nn.gelu(seg(P_GM + GM_WIDTH + c * LANES, P_GM + GM_WIDTH + (c + 1) * LANES))
        v_scr[:, c * LANES:(c + 1) * LANES] = headnorm(gv, prm_ref[:, R_GMN + c * LANES:R_GMN + (c + 1) * LANES])
    for r in range(tm // CHUNK):
        rows = slice(r * CHUNK, (r + 1) * CHUNK)
        vsub = v_scr[rows, :]
        s = btab_ref[0]
        for g in range(GM_GROUPS):
            s = s + _dot(wmix_ref[0, g], jnp.where(lane_grp == g, vsub, 0.0).astype(BF16))
        u = jnp.concatenate(
            [jax.nn.gelu(_dot(h[rows, :], w_ref[:, P_GM + c * LANES:P_GM + (c + 1) * LANES]))
             for c in range(GM_WIDTH // LANES)], axis=1)
        oc_ref[rows, :] = (u * s).astype(BF16)

    t = seg(P_MISC, P_MISC + LANES)
    logf = _log_sigmoid(t + prm_ref[:, R_BF:R_BF + LANES])
    misc_ref[...] = jnp.where(lane < FOX_HEADS, logf, jax.nn.sigmoid(t))

    cos_t = ropet_ref[0:ROT_HALF, :]
    sin_t = ropet_ref[ROT_HALF:ROT_DIM, :]

    def seg_t(a, b):
        return _dot_nt(wt_ref[a:b, :], h)

    def headnorm_t(t, row0, rot):
        outs = []
        for hh in range(t.shape[0] // HEAD_DIM):
            blk = t[hh * HEAD_DIM:(hh + 1) * HEAD_DIM]
            ms = jnp.mean(blk * blk, axis=0, keepdims=True)
            n = blk * lax.rsqrt(ms + EPS) * col_ref[row0 + hh * HEAD_DIM:row0 + (hh + 1) * HEAD_DIM, :]
            if rot:
                x1, x2 = n[0:ROT_HALF], n[ROT_HALF:ROT_DIM]
                n = jnp.concatenate([x1 * cos_t - x2 * sin_t, x2 * cos_t + x1 * sin_t, n[ROT_DIM:]], axis=0)
            outs.append(n)
        return jnp.concatenate(outs, axis=0)

    kc = headnorm_t(seg_t(T_KC, T_KC + KV_A), T_KC, True)
    vc = seg_t(T_VC, T_VC + KV_A)
    ks = headnorm_t(seg_t(T_KS, T_KS + KV_A), T_KS, True)
    vs = seg_t(T_VS, T_VS + KV_A)
    kw = headnorm_t(seg_t(T_KW, T_KW + KV_A), T_KW, True)
    vw = seg_t(T_VW, T_VW + KV_A)
    kb = headnorm_t(seg_t(T_KB, T_KB + WIDTH_B), T_KB, False)
    vb = seg_t(T_VB, T_VB + WIDTH_B)

    @pl.when(i % tiles_per_seq == 0)
    def _():
        carry_ref[...] = jnp.zeros_like(carry_ref)

    zf = seg_t(T_FL, T_FL + SUBLANES) + col_ref[T_FL:T_FL + SUBLANES, :]
    sub = lax.broadcasted_iota(jnp.int32, (SUBLANES, tm), 0)
    lane_t = lax.broadcasted_iota(jnp.int32, (SUBLANES, tm), 1)
    logf_t = jnp.where(sub < FOX_HEADS, _log_sigmoid(zf), 0.0)
    cum = logf_t
    shift = 1
    while shift < tm:
        cum = cum + jnp.where(lane_t >= shift, pltpu.roll(cum, shift, 1), 0.0)
        shift *= 2
    cum = cum + carry_ref[:, 0:1]
    carry_ref[...] = jnp.broadcast_to(cum[:, tm - 1:tm], carry_ref.shape)
    hi, mid, lo = [p.astype(F32) for p in _split3(-cum)]
    sub16 = lax.broadcasted_iota(jnp.int32, (FAUG_ROWS, tm), 0)

    @pl.when(i < n_prompt_tiles)
    def _():
        nsat_ref[0, 0:128, :] = kc
        nsat_ref[0, 128:256, :] = vc
        nsat_ref[0, 256:384, :] = ks
        nsat_ref[0, 384:512, :] = vs
        wint_ref[0, 0:128, :] = kw
        wint_ref[0, 128:256, :] = vw
        foxt_ref[0, 0:WIDTH_B, :] = kb
        foxt_ref[0, WIDTH_B:2 * WIDTH_B, :] = vb
        logft_ref[0] = logf_t
        for hh in range(FOX_HEADS):
            blk = jnp.where(sub16 == 0, hi[hh:hh + 1],
                            jnp.where(sub16 == 1, mid[hh:hh + 1], jnp.where(sub16 == 2, lo[hh:hh + 1], 0.0)))
            faug_ref[0, hh * FAUG_ROWS:(hh + 1) * FAUG_ROWS, :] = blk.astype(BF16)
        kcvc_ref[:, 0:128] = kc.T
        kcvc_ref[:, 128:256] = vc.T

    @pl.when(i >= n_prompt_tiles)
    def _():
        for j, t_ in enumerate((kc, vc, ks, vs)):
            nsas_ref[:, j * KV_A:(j + 1) * KV_A] = t_.T
        wins_ref[:, 0:128] = kw.T
        wins_ref[:, 128:256] = vw.T
        for c in range(WIDTH_B // LANES):
            foxs_ref[:, c * LANES:(c + 1) * LANES] = kb[c * LANES:(c + 1) * LANES].T
            foxs_ref[:, WIDTH_B + c * LANES:WIDTH_B + (c + 1) * LANES] = vb[c * LANES:(c + 1) * LANES].T
        vs_ref[...] = v_scr[...]


def _proj(x, gmix, w, wt, prm, col, rope, ropet, bd, wmix, btab, tm, batch, seq, n_sample):
    m, d = x.shape
    tps = seq // tm
    npt = batch * tps
    row = lambda width: pl.BlockSpec((tm, width), lambda i: (i, 0))
    full = lambda a: pl.BlockSpec(a.shape, lambda i: (0,) * a.ndim)
    kind = lambda i: (i >= npt).astype(jnp.int32)

    def featmajor(rows):
        def idx(i):
            ii = jnp.minimum(i, npt - 1)
            return (ii // tps, 0, ii % tps)
        return pl.BlockSpec((1, rows, tm), idx)

    prow = lambda width: pl.BlockSpec((tm, width), lambda i: (jnp.minimum(i, npt - 1), 0))
    srow = lambda width: pl.BlockSpec((tm, width), lambda i: (jnp.maximum(i - npt, 0), 0))
    n_prompt = batch * seq
    outs = [
        ("qa", row(2 * WIDTH_A), (m, 2 * WIDTH_A), BF16),
        ("qb", row(2 * WIDTH_B), (m, 2 * WIDTH_B), BF16),
        ("misc", row(LANES), (m, LANES), F32),
        ("oc", row(GM_WIDTH), (m, GM_WIDTH), BF16),
        ("kcvc", prow(2 * KV_A), (n_prompt, 2 * KV_A), F32),
        ("nsat", featmajor(4 * KV_A), (batch, 4 * KV_A, seq), F32),
        ("wint", featmajor(2 * KV_A), (batch, 2 * KV_A, seq), F32),
        ("foxt", featmajor(2 * WIDTH_B), (batch, 2 * WIDTH_B, seq), F32),
        ("faug", featmajor(FOX_HEADS * FAUG_ROWS), (batch, FOX_HEADS * FAUG_ROWS, seq), BF16),
        ("logft", featmajor(SUBLANES), (batch, SUBLANES, seq), F32),
        ("nsas", srow(4 * KV_A), (n_sample, 4 * KV_A), F32),
        ("wins", srow(2 * KV_A), (n_sample, 2 * KV_A), F32),
        ("foxs", srow(2 * WIDTH_B), (n_sample, 2 * WIDTH_B), F32),
        ("vs", srow(GM_WIDTH), (n_sample, GM_WIDTH), F32),
    ]
    res = pl.pallas_call(
        functools.partial(_proj_kernel, tm=tm, tiles_per_seq=tps, n_prompt_tiles=npt),
        grid=(m // tm,),
        in_specs=[row(d), full(gmix), full(w), full(wt), full(prm), full(col), row(3 * LANES),
                  pl.BlockSpec((ROT_DIM, tm), lambda i: (0, i)), full(bd),
                  pl.BlockSpec((1,) + wmix.shape[1:], lambda i: (kind(i), 0, 0, 0)),
                  pl.BlockSpec((1,) + btab.shape[1:], lambda i: (kind(i), 0, 0))],
        out_specs=[o[1] for o in outs],
        out_shape=[jax.ShapeDtypeStruct(o[2], o[3]) for o in outs],
        scratch_shapes=[pltpu.VMEM((tm, GM_WIDTH), F32), pltpu.VMEM((SUBLANES, LANES), F32)],
        compiler_params=_cparams(("arbitrary",)),
        name="proj",
    )(x, gmix, w, wt, prm, col, rope, ropet, bd, wmix, btab)
    return dict(zip([o[0] for o in outs], res))


def _mix_kernel(x_ref, gmix_ref, wmg_ref, oa_ref, ob_ref, oc_ref, wa_ref, wb_ref, wc_ref, wout_ref, o_ref):
    x = x_ref[...]
    d = x.shape[1]
    h = _rms_rows(x, gmix_ref[...]).astype(BF16)
    m = jax.nn.sigmoid(_dot(h, wmg_ref[:, 0:d])) * _dot(oa_ref[...].astype(BF16), wa_ref[...])
    m = m + jax.nn.sigmoid(_dot(h, wmg_ref[:, d:2 * d])) * _dot(ob_ref[...].astype(BF16), wb_ref[...])
    m = m + jax.nn.sigmoid(_dot(h, wmg_ref[:, 2 * d:3 * d])) * _dot(oc_ref[...], wc_ref[...])
    o_ref[...] = x + _dot(m.astype(BF16), wout_ref[...])


def _mix(x, gmix, wmg, oa, ob, oc, wa, wb, wc, wout, tm):
    m, d = x.shape
    row = lambda a: pl.BlockSpec((tm, a.shape[1]), lambda i: (i, 0))
    full = lambda a: pl.BlockSpec(a.shape, lambda i: (0,) * a.ndim)
    return pl.pallas_call(
        _mix_kernel,
        grid=(m // tm,),
        in_specs=[row(x), full(gmix), full(wmg), row(oa), row(ob), row(oc), full(wa), full(wb), full(wc),
                  full(wout)],
        out_specs=row(x),
        out_shape=jax.ShapeDtypeStruct((m, d), F32),
        compiler_params=_cparams(("parallel",)),
        name="mix",
    )(x, gmix, wmg, oa, ob, oc, wa, wb, wc, wout)


def _compress_kernel(x_ref, pe_ref, w_ref, o_ref):
    o_ref[0] = _dot((x_ref[0] + pe_ref[...]).astype(BF16), w_ref[...])


def _compress_prompt(x2, pe2, w2):
    b, nc, k = x2.shape
    n = w2.shape[1]
    return pl.pallas_call(
        _compress_kernel,
        grid=(b,),
        in_specs=[pl.BlockSpec((1, nc, k), lambda i: (i, 0, 0)),
                  pl.BlockSpec((1, k), lambda i: (0, 0)),
                  pl.BlockSpec((k, n), lambda i: (0, 0))],
        out_specs=pl.BlockSpec((1, nc, n), lambda i: (i, 0, 0)),
        out_shape=jax.ShapeDtypeStruct((b, nc, n), F32),
        compiler_params=_cparams(("parallel",)),
        name="compress_prompt",
    )(x2, pe2, w2)


def _online_update(s, vt_aug, acc_ref, m_ref):
    n_chunk = s.shape[1] // LANES
    m_old = m_ref[...]
    mx = s[:, 0:LANES]
    for c in range(1, n_chunk):
        mx = jnp.maximum(mx, s[:, c * LANES:(c + 1) * LANES])
    m_new = jnp.maximum(m_old, jnp.max(mx, axis=-1, keepdims=True))
    alpha = jnp.exp(m_old - m_new)
    p = jnp.concatenate(
        [jnp.exp(s[:, c * LANES:(c + 1) * LANES] - m_new).astype(BF16) for c in range(n_chunk)], axis=1)
    acc_ref[...] = alpha * acc_ref[...] + _dot_nt(p, vt_aug)
    m_ref[...] = m_new


def _finish(acc):
    return (acc / pltpu.roll(acc, HEAD_DIM, 1))[:, :HEAD_DIM]


def _select_blocks(imp_t, tok0):
    nb, nt = imp_t.shape
    blk = lax.broadcasted_iota(jnp.int32, (nb, nt), 0)
    cur = (tok0 + lax.broadcasted_iota(jnp.int32, (nb, nt), 1)) // L_SEL
    forced = (blk == 0) | (blk == cur) | (blk == cur - 1)
    score = jnp.where(blk <= cur, jnp.where(forced, FORCE_BASE - blk.astype(F32), imp_t), -jnp.inf)
    sel = jnp.zeros((nb, nt), F32)
    for _ in range(min(TOP_N, nb)):
        mx = jnp.max(score, axis=0, keepdims=True)
        first = jnp.min(jnp.where(score == mx, blk, nb), axis=0, keepdims=True)
        hit = (blk == first) & (mx > -jnp.inf)
        sel = jnp.where(hit, 1.0, sel)
        score = jnp.where(hit, -jnp.inf, score)
    return sel


def _nsa_prompt_kernel(q_ref, misc_ref, kcv_ref, kv_ref, win_ref, o_ref, acc_ref, m_ref, *, n_cmp):
    i = pl.program_id(1)
    rows = NSA_HPG * Q_BLOCK
    tok0 = i * Q_BLOCK
    tok = tok0 + (lax.broadcasted_iota(jnp.int32, (rows, 1), 0) % Q_BLOCK)
    half = n_cmp // 2
    ones_rows = jnp.ones((HEAD_DIM, 1), BF16)
    zero_rows = jnp.zeros((HEAD_DIM, 1), BF16)

    q_los, q_augs, o_cs = [], [], []
    for g in range(NSA_GROUPS):
        gl = slice(g * LANES, (g + 1) * LANES)
        q_lo = jnp.concatenate(
            [q_ref[:, (g * NSA_HPG + h) * LANES:(g * NSA_HPG + h + 1) * LANES] for h in range(NSA_HPG)], axis=0)
        kcv = kcv_ref[0, :, gl].astype(BF16)
        s_c = _dot_nt(q_lo, kcv)
        col = lax.broadcasted_iota(jnp.int32, (1, n_cmp), 1)
        blk_c = 2 * (col % half) + col // half
        s_c = jnp.where((blk_c + 1) * L_CMP - 1 <= tok, s_c, NEG_BIG)
        e_c = jnp.exp(s_c - jnp.max(s_c, axis=-1, keepdims=True))
        p_c = e_c / jnp.sum(e_c, axis=-1, keepdims=True)
        p_c = jnp.where(tok >= L_CMP - 1, p_c, 0.0)
        o_cs.append(_dot(p_c.astype(BF16), kcv)[:, HEAD_DIM:])
        imp = p_c[0:Q_BLOCK]
        for h in range(1, NSA_HPG):
            imp = imp + p_c[h * Q_BLOCK:(h + 1) * Q_BLOCK]
        imp_sel = imp[:, :half] + imp[:, half:]
        if half < LANES:
            imp_sel = jnp.concatenate([imp_sel, jnp.zeros((Q_BLOCK, LANES - half), F32)], axis=1)
        sel_t = _select_blocks(imp_sel.T, tok0)
        nsel = (1.0 - sel_t).T.astype(BF16)
        q_los.append(q_lo)
        q_augs.append(jnp.concatenate([q_lo, jnp.concatenate([nsel] * NSA_HPG, axis=0)], axis=1))

    acc_ref[...] = jnp.zeros_like(acc_ref)
    m_ref[...] = jnp.full_like(m_ref, -jnp.inf)
    n_kt = (tok0 + Q_BLOCK + KEY_TILE - 1) // KEY_TILE

    def sel_tile(kt, causal):
        ks = pl.ds(pl.multiple_of(kt * KEY_TILE, KEY_TILE), KEY_TILE)
        crow = lax.broadcasted_iota(jnp.int32, (LANES, KEY_TILE), 0)
        kblk = (kt * KEY_TILE + lax.broadcasted_iota(jnp.int32, (LANES, KEY_TILE), 1)) // L_SEL
        aug = jnp.where(crow == kblk, -MASK_BIG, 0.0).astype(BF16)
        for g in range(NSA_GROUPS):
            k_t = kv_ref[0, g * HEAD_DIM:(g + 1) * HEAD_DIM, ks].astype(BF16)
            v_t = kv_ref[0, KV_A + g * HEAD_DIM:KV_A + (g + 1) * HEAD_DIM, ks].astype(BF16)
            zeros = jnp.broadcast_to(zero_rows, (HEAD_DIM, KEY_TILE))
            s = _dot(q_augs[g], jnp.concatenate([k_t, zeros, aug], axis=0))
            if causal:
                keypos = kt * KEY_TILE + lax.broadcasted_iota(jnp.int32, (1, KEY_TILE), 1)
                s = jnp.where(keypos <= tok, s, -MASK_BIG)
            ones = jnp.broadcast_to(ones_rows, (HEAD_DIM, KEY_TILE))
            _online_update(s, jnp.concatenate([v_t, ones], axis=0), acc_ref.at[g], m_ref.at[g])

    def sel_body(j, carry):
        sel_tile(2 * j, False)
        sel_tile(2 * j + 1, False)
        return carry

    n_pairs = (n_kt - 1) // 2
    lax.fori_loop(0, n_pairs, sel_body, 0)

    @pl.when(2 * n_pairs < n_kt - 1)
    def _():
        sel_tile(n_kt - 2, False)

    sel_tile(n_kt - 1, True)

    band = WINDOW + Q_BLOCK
    start = pl.multiple_of(jnp.maximum(tok0 - WINDOW, 0), Q_BLOCK)
    wk = pl.ds(start, band)
    dpos = tok - (start + lax.broadcasted_iota(jnp.int32, (1, band), 1))
    ok = (dpos >= 0) & (dpos < WINDOW)
    gates = misc_ref[...]
    for g in range(NSA_GROUPS):
        k_t = win_ref[0, g * HEAD_DIM:(g + 1) * HEAD_DIM, wk].astype(BF16)
        v_t = win_ref[0, KV_A + g * HEAD_DIM:KV_A + (g + 1) * HEAD_DIM, wk].astype(BF16)
        zeros = jnp.broadcast_to(zero_rows, (HEAD_DIM, band))
        s = jnp.where(ok, _dot(q_los[g], jnp.concatenate([k_t, zeros], axis=0)), NEG_BIG)
        p = jnp.exp(s - jnp.max(s, axis=-1, keepdims=True)).astype(BF16)
        ones = jnp.broadcast_to(ones_rows, (HEAD_DIM, band))
        o_w = _finish(_dot_nt(p, jnp.concatenate([v_t, ones], axis=0)))
        o_s = _finish(acc_ref[g])
        o_c = o_cs[g]
        pieces = []
        for h in range(NSA_HPG):
            hr = slice(h * Q_BLOCK, (h + 1) * Q_BLOCK)
            c0 = GATE_LANE0 + (g * NSA_HPG + h) * 3
            pieces.append(gates[:, c0:c0 + 1] * o_c[hr] + gates[:, c0 + 1:c0 + 2] * o_s[hr]
                          + gates[:, c0 + 2:c0 + 3] * o_w[hr])
        o_ref[:, g * NSA_HPG * HEAD_DIM:(g + 1) * NSA_HPG * HEAD_DIM] = jnp.concatenate(pieces, axis=1)


def _nsa_prompt(qa, misc, kcv, nsat, wint, batch, seq):
    m = qa.shape[0]
    nq = seq // Q_BLOCK
    n_cmp = kcv.shape[1]
    rows = NSA_HPG * Q_BLOCK
    return pl.pallas_call(
        functools.partial(_nsa_prompt_kernel, n_cmp=n_cmp),
        grid=(batch, nq),
        in_specs=[pl.BlockSpec((Q_BLOCK, 2 * WIDTH_A), lambda b, i: (b * nq + i, 0)),
                  pl.BlockSpec((Q_BLOCK, LANES), lambda b, i: (b * nq + i, 0)),
                  pl.BlockSpec((1, n_cmp, 2 * KV_A), lambda b, i: (b, 0, 0)),
                  pl.BlockSpec((1, 2 * KV_A, seq), lambda b, i: (b, 1, 0)),
                  pl.BlockSpec((1, 2 * KV_A, seq), lambda b, i: (b, 0, 0))],
        out_specs=pl.BlockSpec((Q_BLOCK, WIDTH_A), lambda b, i: (b * nq + i, 0)),
        out_shape=jax.ShapeDtypeStruct((m, WIDTH_A), F32),
        scratch_shapes=[pltpu.VMEM((NSA_GROUPS, rows, LANES), F32), pltpu.VMEM((NSA_GROUPS, rows, LANES), F32)],
        compiler_params=_cparams(("parallel", "arbitrary")),
        name="nsa_prompt",
    )(qa, misc, kcv, nsat, wint)


def _fox_prompt_kernel(q_ref, kv_ref, faug_ref, o_ref, acc_ref, m_ref, *, qb):
    i = pl.program_id(1)
    tok = i * qb + lax.broadcasted_iota(jnp.int32, (qb, 1), 0)
    lane = lax.broadcasted_iota(jnp.int32, (qb, LANES), 1)
    ones_lanes = jnp.where(lane < HEAD_DIM + 3, 1.0, 0.0).astype(BF16)
    q_augs = [jnp.where(lane < HEAD_DIM, q_ref[:, h * LANES:(h + 1) * LANES], ones_lanes) for h in range(FOX_HEADS)]
    pad_rows = LANES - HEAD_DIM - FAUG_ROWS
    acc_ref[...] = jnp.zeros_like(acc_ref)
    m_ref[...] = jnp.full_like(m_ref, -jnp.inf)

    def tile(kt, causal):
        ks = pl.ds(pl.multiple_of(kt * KEY_TILE, KEY_TILE), KEY_TILE)
        zeros = jnp.zeros((pad_rows, KEY_TILE), BF16)
        ones = jnp.ones((HEAD_DIM, KEY_TILE), BF16)
        for h in range(FOX_HEADS):
            k_t = kv_ref[0, h * HEAD_DIM:(h + 1) * HEAD_DIM, ks].astype(BF16)
            v_t = kv_ref[0, WIDTH_B + h * HEAD_DIM:WIDTH_B + (h + 1) * HEAD_DIM, ks].astype(BF16)
            fa = faug_ref[0, h * FAUG_ROWS:(h + 1) * FAUG_ROWS, ks]
            s = _dot(q_augs[h], jnp.concatenate([k_t, fa, zeros], axis=0))
            if causal:
                keypos = kt * KEY_TILE + lax.broadcasted_iota(jnp.int32, (1, KEY_TILE), 1)
                s = jnp.where(keypos <= tok, s, -MASK_BIG)
            _online_update(s, jnp.concatenate([v_t, ones], axis=0), acc_ref.at[h], m_ref.at[h])

    def body(kt, carry):
        tile(kt, False)
        return carry

    n_full = i * (qb // KEY_TILE)
    lax.fori_loop(0, n_full, body, 0)
    for d in range(qb // KEY_TILE):
        tile(n_full + d, True)
    for h in range(FOX_HEADS):
        o_ref[:, h * HEAD_DIM:(h + 1) * HEAD_DIM] = _finish(acc_ref[h])


def _fox_prompt(qb_arr, foxt, faug, batch, seq):
    m = qb_arr.shape[0]
    qb = min(FOX_Q_BLOCK, seq)
    nq = seq // qb
    return pl.pallas_call(
        functools.partial(_fox_prompt_kernel, qb=qb),
        grid=(batch, nq),
        in_specs=[pl.BlockSpec((qb, 2 * WIDTH_B), lambda b, i: (b * nq + i, 0)),
                  pl.BlockSpec((1, 2 * WIDTH_B, seq), lambda b, i: (b, 0, 0)),
                  pl.BlockSpec((1, FOX_HEADS * FAUG_ROWS, seq), lambda b, i: (b, 0, 0))],
        out_specs=pl.BlockSpec((qb, WIDTH_B), lambda b, i: (b * nq + i, 0)),
        out_shape=jax.ShapeDtypeStruct((m, WIDTH_B), F32),
        scratch_shapes=[pltpu.VMEM((FOX_HEADS, qb, LANES), F32), pltpu.VMEM((FOX_HEADS, qb, LANES), F32)],
        compiler_params=_cparams(("parallel", "arbitrary")),
        name="fox_prompt",
    )(qb_arr, foxt, faug)


def _cmp_pool_kernel(x_ref, pe_ref, w_ref, o_ref, rows_ref, *, nb):
    def put(p, carry):
        dst = pl.ds(pl.multiple_of(p * PAGE, PAGE), PAGE)
        for c in range(2):
            rows_ref[c, dst, :] = x_ref[0, p, c * KV_A:(c + 1) * KV_A, :].T
        return carry

    lax.fori_loop(0, nb, put, 0)
    n_blk = nb * (PAGE // L_CMP)
    for c in range(2):
        cs = slice(c * KV_A, (c + 1) * KV_A)
        acc = jnp.zeros((n_blk, KV_A), F32)
        for l in range(L_CMP):
            xl = rows_ref[c, pl.ds(l, n_blk, stride=L_CMP), :]
            acc = acc + _dot((xl + pe_ref[l:l + 1, cs]).astype(BF16), w_ref[l, cs, cs])
        o_ref[:, cs] = acc


def _compress_pool(cache_t, layer, pe_l, w_l, nb):
    n_pool = cache_t.shape[1]
    per = PAGE // L_CMP
    return pl.pallas_call(
        functools.partial(_cmp_pool_kernel, nb=nb),
        grid=(n_pool // nb,),
        in_specs=[pl.BlockSpec((1, nb, 2 * KV_A, PAGE), lambda i: (layer, i, 0, 0)),
                  pl.BlockSpec(pe_l.shape, lambda i: (0, 0)),
                  pl.BlockSpec(w_l.shape, lambda i: (0, 0, 0))],
        out_specs=pl.BlockSpec((nb * per, 2 * KV_A), lambda i: (i, 0)),
        out_shape=jax.ShapeDtypeStruct((n_pool * per, 2 * KV_A), F32),
        scratch_shapes=[pltpu.VMEM((2, nb * PAGE, KV_A), F32)],
        compiler_params=_cparams(("parallel",)),
        name="compress_pool",
    )(cache_t, pe_l, w_l)


def _softmax_pieces(pieces):
    mx = functools.reduce(jnp.maximum, [jnp.max(s, axis=-1, keepdims=True) for s in pieces])
    es = [jnp.exp(s - mx) for s in pieces]
    den = functools.reduce(lambda a, b: a + b, [jnp.sum(e, axis=-1, keepdims=True) for e in es])
    return es, den


def _nsa_sample_kernel(pt_ref, q_ref, misc_ref, new_ref, wnew_ref, state_ref, pair_ref, *rest, n_pages, past):
    del pt_ref
    pages = rest[:n_pages]
    cmps = rest[n_pages:2 * n_pages]
    o_ref, wout_ref = rest[2 * n_pages:]
    tn = q_ref.shape[0]
    rows = NSA_HEADS * tn
    rid = lax.broadcasted_iota(jnp.int32, (rows, 1), 0)
    qi = rid % tn
    tok = past + qi
    is_g1 = rid >= NSA_HPG * tn

    q = q_ref[...]
    blocks = []
    for g in range(NSA_GROUPS):
        for h in range(NSA_HPG):
            c = q[:, (g * NSA_HPG + h) * LANES:(g * NSA_HPG + h + 1) * LANES]
            blocks.append(pltpu.roll(c, HEAD_DIM, 1) if g == 1 else c)
    q64 = jnp.concatenate(blocks, axis=0).astype(BF16)

    def pick(x):
        return jnp.where(is_g1, x[:, HEAD_DIM:2 * HEAD_DIM], x[:, 0:HEAD_DIM])

    def pad_keys(x):
        return jnp.concatenate([x, jnp.zeros((2 * tn - x.shape[0], x.shape[1]), x.dtype)], axis=0)

    n_cmp = n_pages * (PAGE // L_CMP)
    kcv = jnp.concatenate([c[0] for c in cmps], axis=0).astype(BF16)
    s_c = _dot_nt(q64, kcv[:, 0:KV_A])
    blk_c = lax.broadcasted_iota(jnp.int32, (1, n_cmp), 1)
    s_c = jnp.where((blk_c + 1) * L_CMP - 1 <= tok, s_c, NEG_BIG)
    e_c = jnp.exp(s_c - jnp.max(s_c, axis=-1, keepdims=True))
    p_c = e_c / jnp.sum(e_c, axis=-1, keepdims=True)
    p_c = jnp.where(tok >= L_CMP - 1, p_c, 0.0)
    o_c = pick(_dot(p_c.astype(BF16), kcv[:, KV_A:2 * KV_A]))

    lane = lax.broadcasted_iota(jnp.int32, (tn, LANES), 1)
    cur = (past + lax.broadcasted_iota(jnp.int32, (tn, LANES), 0)) // L_SEL
    forced = (lane == 0) | (lane == cur) | (lane == cur - 1)
    bias_rows = []
    for g in range(NSA_GROUPS):
        imp = p_c[g * NSA_HPG * tn:g * NSA_HPG * tn + tn]
        for h in range(1, NSA_HPG):
            imp = imp + p_c[(g * NSA_HPG + h) * tn:(g * NSA_HPG + h + 1) * tn]
        imp_sel = jnp.zeros((tn, LANES), F32)
        for part in _split3(imp):
            imp_sel = imp_sel + _dot(part, pair_ref[...])
        score = jnp.where(lane <= cur, jnp.where(forced, FORCE_BASE - lane.astype(F32), imp_sel), -jnp.inf)
        sel = jnp.zeros((tn, LANES), F32)
        for _ in range(TOP_N):
            mx = jnp.max(score, axis=-1, keepdims=True)
            first = jnp.min(jnp.where(score == mx, lane, LANES), axis=-1, keepdims=True)
            hit = (lane == first) & (mx > -jnp.inf)
            sel = jnp.where(hit, 1.0, sel)
            score = jnp.where(hit, -jnp.inf, score)
        bias_rows.append(jnp.where(sel > 0.0, 0.0, -MASK_BIG))
    bias_sel = jnp.concatenate([bias_rows[0]] * NSA_HPG + [bias_rows[1]] * NSA_HPG, axis=0)

    lane_lo = lax.broadcasted_iota(jnp.int32, (rows, PAGE), 1) < L_SEL
    new_kv = pad_keys(new_ref[:, 2 * KV_A:4 * KV_A]).astype(BF16)
    scores = []
    for j in range(n_pages):
        bias = jnp.where(lane_lo, bias_sel[:, 2 * j:2 * j + 1], bias_sel[:, 2 * j + 1:2 * j + 2])
        scores.append(_dot(q64, pages[j][0, 0, 0:KV_A, :].astype(BF16)) + bias)
    kcol = lax.broadcasted_iota(jnp.int32, (1, 2 * tn), 1)
    s_new = _dot_nt(q64, new_kv[:, 0:KV_A]) + bias_sel[:, 2 * n_pages:2 * n_pages + 1]
    scores.append(jnp.where((kcol <= qi) & (kcol < tn), s_new, -MASK_BIG))
    probs, den = _softmax_pieces(scores)
    acc = _dot(probs[n_pages].astype(BF16), new_kv[:, KV_A:2 * KV_A])
    for j in range(n_pages):
        acc = acc + _dot_nt(probs[j].astype(BF16), pages[j][0, 0, KV_A:2 * KV_A, :].astype(BF16))
    o_s = pick(acc) / den

    wb = state_ref.shape[3]
    st = state_ref[0, 0]
    wnew = wnew_ref[...]
    wn_b = pad_keys(wnew).astype(BF16)
    kpos = past - wb + lax.broadcasted_iota(jnp.int32, (1, wb), 1)
    dpos = tok - kpos
    s1 = jnp.where((dpos >= 0) & (dpos < WINDOW) & (kpos >= 0), _dot(q64, st[0:KV_A].astype(BF16)), NEG_BIG)
    kcol = lax.broadcasted_iota(jnp.int32, (1, 2 * tn), 1)
    s2 = jnp.where((kcol <= qi) & (kcol < tn), _dot_nt(q64, wn_b[:, 0:KV_A]), NEG_BIG)
    (p1, p2), den_w = _softmax_pieces([s1, s2])
    o_w = pick(_dot_nt(p1.astype(BF16), st[KV_A:2 * KV_A].astype(BF16))
               + _dot(p2.astype(BF16), wn_b[:, KV_A:2 * KV_A])) / den_w

    new_t = jnp.concatenate([jnp.zeros((LANES - tn, 2 * KV_A), F32), wnew], axis=0).T
    shifted = pltpu.roll(st, wb - tn, 1)
    wout_ref[0, 0, :, 0:wb - LANES] = shifted[:, 0:wb - LANES]
    lane_w = lax.broadcasted_iota(jnp.int32, (2 * KV_A, LANES), 1)
    wout_ref[0, 0, :, wb - LANES:wb] = jnp.where(lane_w >= LANES - tn, new_t, shifted[:, wb - LANES:wb])

    gates = misc_ref[...]
    pieces = []
    for hh in range(NSA_HEADS):
        hr = slice(hh * tn, (hh + 1) * tn)
        c0 = GATE_LANE0 + hh * 3
        pieces.append(gates[:, c0:c0 + 1] * o_c[hr] + gates[:, c0 + 1:c0 + 2] * o_s[hr]
                      + gates[:, c0 + 2:c0 + 3] * o_w[hr])
    o_ref[...] = jnp.concatenate(pieces, axis=1)


def _nsa_sample(page_table, layer, qa_s, misc_s, nsa_s, win_s, state_t, pair, cache_t, cmp_pool3, oa_full, n_prompt):
    nseq, n_pages = page_table.shape
    tn = qa_s.shape[0] // nseq
    past = n_pages * PAGE
    wb = state_t.shape[3]
    row = lambda width: pl.BlockSpec((tn, width), lambda b, pt: (b, 0))
    page_specs = [pl.BlockSpec((1, 1, 2 * KV_A, PAGE), functools.partial(lambda b, pt, j: (layer, pt[b, j], 1, 0), j=j))
                  for j in range(n_pages)]
    cmp_specs = [pl.BlockSpec((1, PAGE // L_CMP, 2 * KV_A), functools.partial(lambda b, pt, j: (pt[b, j], 0, 0), j=j))
                 for j in range(n_pages)]
    grid_spec = pltpu.PrefetchScalarGridSpec(
        num_scalar_prefetch=1,
        grid=(nseq,),
        in_specs=[row(2 * WIDTH_A), row(LANES), row(4 * KV_A), row(2 * KV_A),
                  pl.BlockSpec((1, 1, 2 * KV_A, wb), lambda b, pt: (layer, b, 0, 0)),
                  pl.BlockSpec(pair.shape, lambda b, pt: (0, 0))] + page_specs + cmp_specs
                 + [pl.BlockSpec(memory_space=pl.ANY)],
        out_specs=[pl.BlockSpec((tn, WIDTH_A), lambda b, pt: (n_prompt // tn + b, 0)),
                   pl.BlockSpec((1, 1, 2 * KV_A, wb), lambda b, pt: (0, b, 0, 0))],
    )

    def body(pt_ref, *refs):
        ins = refs[:6 + 2 * n_pages]
        outs = refs[7 + 2 * n_pages:]
        _nsa_sample_kernel(pt_ref, *ins, *outs, n_pages=n_pages, past=past)

    n_in = 1 + 6 + 2 * n_pages
    return pl.pallas_call(
        body,
        grid_spec=grid_spec,
        out_shape=[jax.ShapeDtypeStruct(oa_full.shape, F32), jax.ShapeDtypeStruct((1, nseq, 2 * KV_A, wb), F32)],
        input_output_aliases={n_in: 0},
        compiler_params=_cparams(("arbitrary",)),
        name="nsa_sample",
    )(page_table, qa_s, misc_s, nsa_s, win_s, state_t, pair, *([cache_t] * n_pages), *([cmp_pool3] * n_pages), oa_full)


def _fox_sample_kernel(pt_ref, q_ref, new_ref, lfn_ref, ustrict_ref, *rest, n_pages):
    del pt_ref
    pages = rest[:n_pages]
    lfps = rest[n_pages:2 * n_pages]
    o_ref = rest[2 * n_pages]
    tn = q_ref.shape[0]
    rows = FOX_HEADS * tn
    qi = lax.broadcasted_iota(jnp.int32, (rows, 1), 0) % tn

    q = q_ref[...]
    zeros_half = jnp.zeros((tn, LANES), F32)
    blocks = []
    for h in range(FOX_HEADS):
        c = q[:, h * LANES:(h + 1) * LANES]
        if h % 2 == 1:
            c = pltpu.roll(c, HEAD_DIM, 1)
        blocks.append(jnp.concatenate([c, zeros_half] if h < 2 else [zeros_half, c], axis=1))
    q32 = jnp.concatenate(blocks, axis=0).astype(BF16)

    lf = jnp.concatenate([r[0, 0] for r in lfps], axis=0)
    within = jnp.zeros(lf.shape, F32)
    for part in _split3(lf):
        within = within + _dot(part, ustrict_ref[...])
    tot = jnp.sum(lf, axis=-1, keepdims=True)
    after = jnp.zeros((FOX_HEADS, 1), F32)
    page_bias = [None] * n_pages
    for j in reversed(range(n_pages)):
        b4 = within[j * FOX_HEADS:(j + 1) * FOX_HEADS] + after
        page_bias[j] = jnp.concatenate(
            [jnp.broadcast_to(b4[h:h + 1], (tn, PAGE)) for h in range(FOX_HEADS)], axis=0)
        after = after + tot[j * FOX_HEADS:(j + 1) * FOX_HEADS]

    lfn = lfn_ref[0]
    run = lfn[:, 0:1]
    cols = [run]
    for r in range(1, tn):
        run = run + lfn[:, r:r + 1]
        cols.append(run)
    cum = jnp.concatenate(cols + [jnp.zeros((FOX_HEADS, tn), F32)], axis=1)
    new_bias = jnp.concatenate(
        [jnp.broadcast_to(-cum[h:h + 1], (tn, 2 * tn)) for h in range(FOX_HEADS)], axis=0)

    new = jnp.concatenate([new_ref[...], jnp.zeros((tn, 2 * WIDTH_B), F32)], axis=0).astype(BF16)
    scores = [_dot(q32, pages[j][0, 0, 0:WIDTH_B, :].astype(BF16)) + page_bias[j] for j in range(n_pages)]
    kcol = lax.broadcasted_iota(jnp.int32, (1, 2 * tn), 1)
    scores.append(jnp.where((kcol <= qi) & (kcol < tn), _dot_nt(q32, new[:, :WIDTH_B]) + new_bias, -MASK_BIG))
    probs, den = _softmax_pieces(scores)
    acc = _dot(probs[n_pages].astype(BF16), new[:, WIDTH_B:])
    for j in range(n_pages):
        acc = acc + _dot_nt(probs[j].astype(BF16), pages[j][0, 0, WIDTH_B:2 * WIDTH_B, :].astype(BF16))
    o32 = acc / den
    lane_head = lax.broadcasted_iota(jnp.int32, (tn, WIDTH_B), 1) // HEAD_DIM
    out = jnp.zeros((tn, WIDTH_B), F32)
    for h in range(FOX_HEADS):
        out = out + jnp.where(lane_head == h, o32[h * tn:(h + 1) * tn], 0.0)
    o_ref[...] = out


def _fox_sample(page_table, layer, qb_s, fox_s, lfn_t, ustrict, cache_t, logf_t, ob_full, n_prompt):
    nseq, n_pages = page_table.shape
    tn = qb_s.shape[0] // nseq
    row = lambda width: pl.BlockSpec((tn, width), lambda b, pt: (b, 0))
    page_specs = [pl.BlockSpec((1, 1, 2 * WIDTH_B, PAGE), functools.partial(lambda b, pt, j: (layer, pt[b, j], 0, 0), j=j))
                  for j in range(n_pages)]
    lf_specs = [pl.BlockSpec((1, 1, FOX_HEADS, PAGE), functools.partial(lambda b, pt, j: (layer, pt[b, j], 0, 0), j=j))
                for j in range(n_pages)]
    grid_spec = pltpu.PrefetchScalarGridSpec(
        num_scalar_prefetch=1,
        grid=(nseq,),
        in_specs=[row(2 * WIDTH_B), row(2 * WIDTH_B),
                  pl.BlockSpec((1, FOX_HEADS, tn), lambda b, pt: (b, 0, 0)),
                  pl.BlockSpec(ustrict.shape, lambda b, pt: (0, 0))] + page_specs + lf_specs
                 + [pl.BlockSpec(memory_space=pl.ANY)],
        out_specs=pl.BlockSpec((tn, WIDTH_B), lambda b, pt: (n_prompt // tn + b, 0)),
    )

    def body(pt_ref, *refs):
        ins = refs[:4 + 2 * n_pages]
        outs = refs[5 + 2 * n_pages:]
        _fox_sample_kernel(pt_ref, *ins, *outs, n_pages=n_pages)

    n_in = 1 + 4 + 2 * n_pages
    return pl.pallas_call(
        body,
        grid_spec=grid_spec,
        out_shape=jax.ShapeDtypeStruct(ob_full.shape, F32),
        input_output_aliases={n_in: 0},
        compiler_params=_cparams(("arbitrary",)),
        name="fox_sample",
    )(page_table, qb_s, fox_s, lfn_t, ustrict, *([cache_t] * n_pages), *([logf_t] * n_pages), ob_full)


def _in_offsets():
    splits = (WIDTH_A, KV_A, KV_A, KV_A, KV_A, KV_A, KV_A, 3 * NSA_HEADS, WIDTH_B, WIDTH_B, WIDTH_B, FOX_HEADS,
              GM_WIDTH, GM_WIDTH)
    offs = np.concatenate([[0], np.cumsum(splits)])
    names = ("qa", "kc", "vc", "ks", "vs", "kw", "vw", "ga", "qb", "kb", "vb", "fl", "gu", "gv", "mg")
    return {n: int(o) for n, o in zip(names, offs)}


def _pack_params(w_in, qk_gain, gm_norm, b_forget, d_model):
    depth = w_in.shape[0]
    o = _in_offsets()
    cols = np.concatenate([
        np.arange(o["qa"], o["qa"] + WIDTH_A),
        np.arange(o["qb"], o["qb"] + WIDTH_B),
        np.arange(o["gu"], o["gu"] + 2 * GM_WIDTH),
        np.arange(o["fl"], o["fl"] + FOX_HEADS),
        np.arange(o["ga"], o["ga"] + 3 * NSA_HEADS)])
    w_p = jnp.take(w_in, jnp.asarray(cols), axis=2)
    w_p = jnp.pad(w_p, ((0, 0), (0, 0), (0, P_COLS - w_p.shape[2]))).astype(BF16)
    rows_t = np.concatenate([
        np.arange(o["kc"], o["kc"] + 6 * KV_A),
        np.arange(o["kb"], o["kb"] + 2 * WIDTH_B),
        np.arange(o["fl"], o["fl"] + FOX_HEADS)])
    w_t = jnp.swapaxes(jnp.take(w_in, jnp.asarray(rows_t), axis=2), 1, 2)
    w_t = jnp.pad(w_t, ((0, 0), (0, T_ROWS - w_t.shape[1]), (0, 0))).astype(BF16)
    w_mg = w_in[:, :, o["mg"]:o["mg"] + 3 * d_model].astype(BF16)
    tile = lambda g, n: jnp.tile(g, (1, n))
    prm = jnp.concatenate([
        tile(qk_gain[:, 0], NSA_HEADS), tile(qk_gain[:, 4], FOX_HEADS), gm_norm,
        b_forget, jnp.zeros((depth, LANES - FOX_HEADS), F32)], axis=1)
    assert prm.shape[1] == R_COLS
    ones = jnp.ones((depth, KV_A), F32)
    col = jnp.concatenate([
        tile(qk_gain[:, 1], NSA_GROUPS), ones, tile(qk_gain[:, 2], NSA_GROUPS), ones,
        tile(qk_gain[:, 3], NSA_GROUPS), ones, tile(qk_gain[:, 5], FOX_HEADS), jnp.ones((depth, WIDTH_B), F32),
        b_forget, jnp.zeros((depth, SUBLANES - FOX_HEADS), F32)], axis=1)
    assert col.shape[1] == T_ROWS
    return w_p, w_t, w_mg, prm[:, None, :], col[:, :, None]


def _rope_tables(pos):
    half = ROT_HALF
    inv = ROPE_THETA ** (-jnp.arange(half, dtype=F32) / half)
    ang = pos.astype(F32)[:, None] * inv[None, :]
    cos, sin = jnp.cos(ang), jnp.sin(ang)
    n = pos.shape[0]
    zero = jnp.zeros((n, HEAD_DIM - ROT_DIM), F32)
    zero8 = jnp.zeros((n, half), F32)
    c64 = jnp.concatenate([cos, cos, zero + 1.0], axis=1)
    lo64 = jnp.concatenate([-sin, zero8, zero], axis=1)
    hi64 = jnp.concatenate([zero8, sin, zero], axis=1)
    rep = lambda t: jnp.tile(t, (1, LANES // HEAD_DIM))
    token_major = jnp.concatenate([rep(c64), rep(lo64), rep(hi64)], axis=1)
    feature_major = jnp.concatenate([cos.T, sin.T], axis=0)
    return token_major, feature_major


def _gmlp_tables(tn, w_spatial, b_spatial):
    w_p = jnp.tril(w_spatial)
    eye = jnp.asarray(np.eye(CHUNK // tn), F32)
    w_s = jnp.einsum("ab,lgts->lgatbs", eye, jnp.tril(w_spatial[:, :, :tn, :tn]))
    w_s = w_s.reshape(w_spatial.shape[0], GM_GROUPS, CHUNK, CHUNK)
    wmix = jnp.stack([w_p, w_s], axis=1).astype(BF16)
    b_p = jnp.repeat(jnp.swapaxes(b_spatial, 1, 2), GM_DIM, axis=2)
    b_s = jnp.tile(b_p[:, :tn], (1, CHUNK // tn, 1))
    btab = jnp.stack([b_p, b_s], axis=1)
    return wmix, btab


def _compress_weights(w_cmp, pe_cmp):
    depth = w_cmp.shape[0]
    eye_g = jnp.asarray(np.eye(NSA_GROUPS), F32)
    w_l = jnp.einsum("kq,gh,zklde->zlkgdqhe", jnp.asarray(np.eye(2), F32), eye_g, w_cmp)
    w_l = w_l.reshape(depth, L_CMP, 2 * KV_A, 2 * KV_A)
    pe_l = jnp.broadcast_to(pe_cmp[:, :, :, None, :], (depth, 2, L_CMP, NSA_GROUPS, HEAD_DIM))
    pe_l = jnp.transpose(pe_l, (0, 2, 1, 3, 4)).reshape(depth, L_CMP, 2 * KV_A)
    return w_l.astype(BF16), pe_l


def _group_major(x):
    s = x.shape[:-1]
    return jnp.swapaxes(x.reshape(s + (2, NSA_GROUPS, HEAD_DIM)), -3, -2).reshape(s + (2 * KV_A,))


def _feature_major(x, n_feat):
    nd = x.ndim
    perm = tuple(range(nd - 4)) + (nd - 3, nd - 2, nd - 1, nd - 4)
    y = jnp.transpose(x, perm)
    return y.reshape(y.shape[:nd - 4] + (n_feat, y.shape[-1]))


def _token_major_view(y, dims):
    lead = y.shape[:-2]
    z = y.reshape(lead + tuple(dims) + (y.shape[-1],))
    nd = z.ndim
    perm = tuple(range(nd - 4)) + (nd - 1, nd - 4, nd - 3, nd - 2)
    return jnp.transpose(z, perm)


def kernel(x_prompt, x_sample, cache_nsa_kv, cache_fox_kv, cache_fox_logf, state_nsa_win, page_table,
           g_ffn_a, w_ffn_a_gu, w_ffn_a_down, g_mix, w_in, b_forget, qk_gain, w_cmp, pe_cmp,
           gm_norm, w_spatial, b_spatial, w_branch_a, w_branch_b, w_branch_c, w_out,
           g_ffn_b, w_ffn_b_gu, w_ffn_b_down):
    batch, seq, d_model = x_prompt.shape
    nseq, tn, _ = x_sample.shape
    depth = w_in.shape[0]
    n_pool = cache_nsa_kv.shape[1]
    n_pages = page_table.shape[1]
    past = n_pages * PAGE
    wb = state_nsa_win.shape[2]
    n_prompt = batch * seq
    n_sample = nseq * tn
    tm = next(t for t in (512, 256, 128) if seq % t == 0 and n_sample % t == 0)
    assert seq % FOX_Q_BLOCK == 0 or seq < FOX_Q_BLOCK
    assert seq >= WINDOW + Q_BLOCK and wb % LANES == 0 and wb > LANES
    assert CHUNK % tn == 0 and past % L_SEL == 0 and wb >= tn

    x = jnp.concatenate([x_prompt.reshape(n_prompt, d_model), x_sample.reshape(n_sample, d_model)], axis=0)
    pos = jnp.concatenate([jnp.tile(jnp.arange(seq), batch), jnp.tile(past + jnp.arange(tn), nseq)])
    rope, rope_t = _rope_tables(pos)

    w_p, w_t, w_mg, prm, col = _pack_params(w_in, qk_gain, gm_norm, b_forget, d_model)
    bd = jnp.asarray(np.kron(np.eye(LANES // HEAD_DIM), np.full((HEAD_DIM, HEAD_DIM), 1.0 / HEAD_DIM)), BF16)
    wmix, btab = _gmlp_tables(tn, w_spatial, b_spatial)
    w_l, pe_l = _compress_weights(w_cmp, pe_cmp)
    w2 = _group_major(w_l.reshape(depth, L_CMP * 2 * KV_A, 2 * KV_A))
    pe2 = pe_l.reshape(depth, 1, L_CMP * 2 * KV_A)
    n_cmp_p = seq // L_CMP

    bf = lambda a: a.astype(BF16)
    wgu_a, wd_a, wgu_b, wd_b = bf(w_ffn_a_gu), bf(w_ffn_a_down), bf(w_ffn_b_gu), bf(w_ffn_b_down)
    wa, wb_, wc, wo = bf(w_branch_a), bf(w_branch_b), bf(w_branch_c), bf(w_out)

    cache_nsa_t = _feature_major(cache_nsa_kv, 4 * KV_A)
    cache_fox_t = _feature_major(cache_fox_kv, 2 * WIDTH_B)
    state_t = _feature_major(state_nsa_win, 2 * KV_A)
    logf_t = jnp.swapaxes(cache_fox_logf, 2, 3)
    nb = next(t for t in (64, 32, 16, 8, 4, 2, 1) if n_pool % t == 0)
    n_sel = past // L_SEL + 1
    pair = np.zeros((n_pages * (PAGE // L_CMP), LANES))
    for c in range(pair.shape[0]):
        pair[c, c // 2] = 1.0
    pair = jnp.asarray(pair, BF16)
    ustrict = jnp.asarray(np.triu(np.ones((PAGE, PAGE)), 1).T, BF16)
    assert n_sel <= LANES

    outs = {k: [] for k in ("nsat", "nsas", "wint", "win_s", "foxt", "foxs", "logft", "logfs", "gmv")}
    for l in range(depth):
        x = _ffn(x, g_ffn_a[l][None], wgu_a[l], wd_a[l], tm)
        pr = _proj(x, g_mix[l][None], w_p[l], w_t[l], prm[l], col[l], rope, rope_t, bd, wmix[l], btab[l],
                   tm, batch, seq, n_sample)

        blocks = pr["kcvc"].reshape(batch, n_cmp_p // 2, 2, L_CMP * 2 * KV_A)
        x2 = jnp.swapaxes(blocks, 1, 2).reshape(batch, n_cmp_p, L_CMP * 2 * KV_A)
        kcv = _compress_prompt(x2, pe2[l], w2[l])
        oa = _nsa_prompt(pr["qa"], pr["misc"], kcv, pr["nsat"], pr["wint"], batch, seq)
        ob = _fox_prompt(pr["qb"], pr["foxt"], pr["faug"], batch, seq)

        cmp_pool = _compress_pool(cache_nsa_t, l, pe_l[l], w_l[l], nb)
        cmp_pool3 = cmp_pool.reshape(n_pool, PAGE // L_CMP, 2 * KV_A)
        sl = slice(n_prompt, None)
        oa, win_s = _nsa_sample(page_table, l, pr["qa"][sl].astype(F32), pr["misc"][sl], pr["nsas"],
                                pr["wins"], state_t, pair, cache_nsa_t, cmp_pool3, oa, n_prompt)
        lfn_t = jnp.swapaxes(pr["misc"][sl, :FOX_HEADS].reshape(nseq, tn, FOX_HEADS), 1, 2)
        ob = _fox_sample(page_table, l, pr["qb"][sl].astype(F32), pr["foxs"], lfn_t, ustrict, cache_fox_t,
                         logf_t, ob, n_prompt)

        x = _mix(x, g_mix[l][None], w_mg[l], oa, ob, pr["oc"], wa[l], wb_[l], wc[l], wo[l], tm)
        x = _ffn(x, g_ffn_b[l][None], wgu_b[l], wd_b[l], tm)

        outs["nsat"].append(pr["nsat"])
        outs["nsas"].append(pr["nsas"])
        outs["wint"].append(pr["wint"][:, :, seq - min(WINDOW, seq):])
        outs["win_s"].append(win_s[0])
        outs["foxt"].append(pr["foxt"])
        outs["foxs"].append(pr["foxs"])
        outs["logft"].append(pr["logft"][:, :FOX_HEADS])
        outs["logfs"].append(pr["misc"][sl, :FOX_HEADS])
        outs["gmv"].append(pr["vs"])

    st = {k: jnp.stack(v) for k, v in outs.items()}
    return (
        x[:n_prompt].reshape(batch, seq, d_model),
        x[n_prompt:].reshape(nseq, tn, d_model),
        _token_major_view(st["nsat"], (4, NSA_GROUPS, HEAD_DIM)),
        st["nsas"].reshape(depth, nseq, tn, 4, NSA_GROUPS, HEAD_DIM),
        _token_major_view(st["wint"], (2, NSA_GROUPS, HEAD_DIM)),
        _token_major_view(st["win_s"], (2, NSA_GROUPS, HEAD_DIM)),
        _token_major_view(st["foxt"], (2, FOX_HEADS, HEAD_DIM)),
        st["foxs"].reshape(depth, nseq, tn, 2, FOX_HEADS, HEAD_DIM),
        jnp.swapaxes(st["logft"], 2, 3),
        st["logfs"].reshape(depth, nseq, tn, FOX_HEADS),
        st["gmv"].reshape(depth, nseq, tn, GM_WIDTH),
    )
```

```python
import functools

import numpy as np
import jax
import jax.numpy as jnp
from jax import lax
from jax.experimental import pallas as pl
from jax.experimental.pallas import tpu as pltpu

F32 = jnp.float32
BF16 = jnp.bfloat16

HEAD_DIM = 64
ROT_DIM = HEAD_DIM // 4
ROT_HALF = ROT_DIM // 2
ROPE_THETA = 500000.0
NSA_HEADS = 8
NSA_GROUPS = 2
NSA_HPG = NSA_HEADS // NSA_GROUPS
L_CMP = 32
L_SEL = 64
TOP_N = 16
WINDOW = 512
FOX_HEADS = 4
GM_GROUPS = 4
GM_DIM = 64
GM_WIDTH = GM_GROUPS * GM_DIM
CHUNK = 128
PAGE = 128
EPS = 1e-6
NEG_BIG = -1e30
WIDTH_A = NSA_HEADS * HEAD_DIM
WIDTH_B = FOX_HEADS * HEAD_DIM
KV_A = NSA_GROUPS * HEAD_DIM
QK_SCALE = HEAD_DIM ** -0.5

LANES = 128
SUBLANES = 8
Q_BLOCK = 256
KEY_TILE = 256
FOX_Q_BLOCK = 512
FAUG_ROWS = 16
CMP_PITCH = L_CMP + 4
MASK_BIG = float(2 ** 30)
FORCE_BASE = 30000.0
VMEM_LIMIT = 56 * 1024 * 1024

P_QA = 0
P_QB = 512
P_GM = 768
P_MISC = 1280
P_COLS = 1408
T_KC, T_VC, T_KS, T_VS, T_KW, T_VW, T_KB, T_VB, T_FL, T_ROWS = 0, 128, 256, 384, 512, 640, 768, 1024, 1280, 1288
R_GQ, R_GQB, R_GMN, R_BF, R_COLS = 0, 512, 768, 1024, 1152
GATE_LANE0 = FOX_HEADS


def _dot(a, b):
    return jnp.dot(a, b, preferred_element_type=F32)


def _dot_nt(a, b):
    return lax.dot_general(a, b, (((1,), (1,)), ((), ())), preferred_element_type=F32)


def _split3(x):
    a = x.astype(BF16)
    r = x - a.astype(F32)
    b = r.astype(BF16)
    c = (r - b.astype(F32)).astype(BF16)
    return a, b, c


def _rms_rows(x, g):
    return x * lax.rsqrt(jnp.mean(x * x, axis=-1, keepdims=True) + EPS) * g


def _log_sigmoid(z):
    return jnp.minimum(z, 0.0) - jnp.log(1.0 + jnp.exp(-jnp.abs(z)))


def _cparams(sem):
    return pltpu.CompilerParams(dimension_semantics=sem, vmem_limit_bytes=VMEM_LIMIT)


def _ffn_kernel(x_ref, g_ref, wgu_ref, wd_ref, o_ref, acc_ref, *, d_ff, chunk):
    x = x_ref[...]
    h = _rms_rows(x, g_ref[...]).astype(BF16)
    acc_ref[...] = jnp.zeros_like(acc_ref)
    for c in range(d_ff // chunk):
        g = _dot(h, wgu_ref[:, c * chunk:(c + 1) * chunk])
        u = _dot(h, wgu_ref[:, d_ff + c * chunk:d_ff + (c + 1) * chunk])
        a = (jax.nn.silu(g) * u).astype(BF16)
        acc_ref[...] += _dot(a, wd_ref[c * chunk:(c + 1) * chunk, :])
    o_ref[...] = x + 0.5 * acc_ref[...]


def _ffn(x, g, wgu, wd, tm):
    m, d = x.shape
    d_ff = wd.shape[0]
    return pl.pallas_call(
        functools.partial(_ffn_kernel, d_ff=d_ff, chunk=256),
        grid=(m // tm,),
        in_specs=[pl.BlockSpec((tm, d), lambda i: (i, 0)),
                  pl.BlockSpec((1, d), lambda i: (0, 0)),
                  pl.BlockSpec((d, 2 * d_ff), lambda i: (0, 0)),
                  pl.BlockSpec((d_ff, d), lambda i: (0, 0))],
        out_specs=pl.BlockSpec((tm, d), lambda i: (i, 0)),
        out_shape=jax.ShapeDtypeStruct((m, d), F32),
        scratch_shapes=[pltpu.VMEM((tm, d), F32)],
        compiler_params=_cparams(("parallel",)),
        name="ffn",
    )(x, g, wgu, wd)


def _proj_kernel(x_ref, gmix_ref, w_ref, wt_ref, prm_ref, col_ref, rope_ref, ropet_ref, bd_ref, wmix_ref, btab_ref,
                 qa_ref, qb_ref, misc_ref, oc_ref, kcvc_ref, nsat_ref, wint_ref, foxt_ref, faug_ref, logft_ref,
                 nsas_ref, wins_ref, foxs_ref, vs_ref, v_scr, carry_ref, *, tm, tiles_per_seq, n_prompt_tiles):
    i = pl.program_id(0)
    x = x_ref[...]
    h = _rms_rows(x, gmix_ref[...]).astype(BF16)
    lane = lax.broadcasted_iota(jnp.int32, (tm, LANES), 1)
    lo64 = lane < HEAD_DIM

    cos = rope_ref[:, 0:128]
    sin_lo = rope_ref[:, 128:256]
    sin_hi = rope_ref[:, 256:384]
    bd = bd_ref[...]

    def seg(a, b):
        return _dot(h, w_ref[:, a:b])

    def headnorm(t, gain):
        hi, lo, _ = _split3(t * t)
        ms = _dot(hi, bd) + _dot(lo, bd)
        return t * lax.rsqrt(ms + EPS) * gain

    def rope(t):
        return t * cos + pltpu.roll(t, LANES - ROT_HALF, 1) * sin_lo + pltpu.roll(t, ROT_HALF, 1) * sin_hi

    def head_split(t):
        return (jnp.where(lo64, t, 0.0).astype(BF16),
                jnp.where(lo64, pltpu.roll(t, HEAD_DIM, 1), 0.0).astype(BF16))

    for c in range(WIDTH_A // LANES):
        t = seg(P_QA + c * LANES, P_QA + (c + 1) * LANES)
        t = rope(headnorm(t, prm_ref[:, R_GQ + c * LANES:R_GQ + (c + 1) * LANES])) * QK_SCALE
        a, b = head_split(t)
        qa_ref[:, (2 * c) * LANES:(2 * c + 1) * LANES] = a
        qa_ref[:, (2 * c + 1) * LANES:(2 * c + 2) * LANES] = b
    for c in range(WIDTH_B // LANES):
        t = seg(P_QB + c * LANES, P_QB + (c + 1) * LANES)
        t = headnorm(t, prm_ref[:, R_GQB + c * LANES:R_GQB + (c + 1) * LANES]) * QK_SCALE
        a, b = head_split(t)
        qb_ref[:, (2 * c) * LANES:(2 * c + 1) * LANES] = a
        qb_ref[:, (2 * c + 1) * LANES:(2 * c + 2) * LANES] = b

    lane_grp = lax.broadcasted_iota(jnp.int32, (CHUNK, GM_WIDTH), 1) // GM_DIM
    for c in range(GM_WIDTH // LANES):
        gv = jax.nn.gelu(seg(P_GM + GM_WIDTH + c * LANES, P_GM + GM_WIDTH + (c + 1) * LANES))
        v_scr[:, c * LANES:(c + 1) * LANES] = headnorm(gv, prm_ref[:, R_GMN + c * LANES:R_GMN + (c + 1) * LANES])
    for r in range(tm // CHUNK):
        rows = slice(r * CHUNK, (r + 1) * CHUNK)
        vsub = v_scr[rows, :]
        s = btab_ref[0]
        for g in range(GM_GROUPS):
            s = s + _dot(wmix_ref[0, g], jnp.where(lane_grp == g, vsub, 0.0).astype(BF16))
        u = jnp.concatenate(
            [jax.nn.gelu(_dot(h[rows, :], w_ref[:, P_GM + c * LANES:P_GM + (c + 1) * LANES]))
             for c in range(GM_WIDTH // LANES)], axis=1)
        oc_ref[rows, :] = (u * s).astype(BF16)

    t = seg(P_MISC, P_MISC + LANES)
    logf = _log_sigmoid(t + prm_ref[:, R_BF:R_BF + LANES])
    misc_ref[...] = jnp.where(lane < FOX_HEADS, logf, jax.nn.sigmoid(t))

    cos_t = ropet_ref[0:ROT_HALF, :]
    sin_t = ropet_ref[ROT_HALF:ROT_DIM, :]

    def seg_t(a, b):
        return _dot_nt(wt_ref[a:b, :], h)

    def headnorm_t(t, row0, rot):
        outs = []
        for hh in range(t.shape[0] // HEAD_DIM):
            blk = t[hh * HEAD_DIM:(hh + 1) * HEAD_DIM]
            ms = jnp.mean(blk * blk, axis=0, keepdims=True)
            n = blk * lax.rsqrt(ms + EPS) * col_ref[row0 + hh * HEAD_DIM:row0 + (hh + 1) * HEAD_DIM, :]
            if rot:
                x1, x2 = n[0:ROT_HALF], n[ROT_HALF:ROT_DIM]
                n = jnp.concatenate([x1 * cos_t - x2 * sin_t, x2 * cos_t + x1 * sin_t, n[ROT_DIM:]], axis=0)
            outs.append(n)
        return jnp.concatenate(outs, axis=0)

    kc = headnorm_t(seg_t(T_KC, T_KC + KV_A), T_KC, True)
    vc = seg_t(T_VC, T_VC + KV_A)
    ks = headnorm_t(seg_t(T_KS, T_KS + KV_A), T_KS, True)
    vs = seg_t(T_VS, T_VS + KV_A)
    kw = headnorm_t(seg_t(T_KW, T_KW + KV_A), T_KW, True)
    vw = seg_t(T_VW, T_VW + KV_A)
    kb = headnorm_t(seg_t(T_KB, T_KB + WIDTH_B), T_KB, False)
    vb = seg_t(T_VB, T_VB + WIDTH_B)

    @pl.when(i % tiles_per_seq == 0)
    def _():
        carry_ref[...] = jnp.zeros_like(carry_ref)

    zf = seg_t(T_FL, T_FL + SUBLANES) + col_ref[T_FL:T_FL + SUBLANES, :]
    sub = lax.broadcasted_iota(jnp.int32, (SUBLANES, tm), 0)
    lane_t = lax.broadcasted_iota(jnp.int32, (SUBLANES, tm), 1)
    logf_t = jnp.where(sub < FOX_HEADS, _log_sigmoid(zf), 0.0)
    cum = logf_t
    shift = 1
    while shift < tm:
        cum = cum + jnp.where(lane_t >= shift, pltpu.roll(cum, shift, 1), 0.0)
        shift *= 2
    cum = cum + carry_ref[:, 0:1]
    carry_ref[...] = jnp.broadcast_to(cum[:, tm - 1:tm], carry_ref.shape)
    hi, mid, lo = [p.astype(F32) for p in _split3(-cum)]
    sub16 = lax.broadcasted_iota(jnp.int32, (FAUG_ROWS, tm), 0)

    @pl.when(i < n_prompt_tiles)
    def _():
        nsat_ref[0, 0:128, :] = kc
        nsat_ref[0, 128:256, :] = vc
        nsat_ref[0, 256:384, :] = ks
        nsat_ref[0, 384:512, :] = vs
        wint_ref[0, 0:128, :] = kw
        wint_ref[0, 128:256, :] = vw
        foxt_ref[0, 0:WIDTH_B, :] = kb
        foxt_ref[0, WIDTH_B:2 * WIDTH_B, :] = vb
        logft_ref[0] = logf_t
        for hh in range(FOX_HEADS):
            blk = jnp.where(sub16 == 0, hi[hh:hh + 1],
                            jnp.where(sub16 == 1, mid[hh:hh + 1], jnp.where(sub16 == 2, lo[hh:hh + 1], 0.0)))
            faug_ref[0, hh * FAUG_ROWS:(hh + 1) * FAUG_ROWS, :] = blk.astype(BF16)
        kcvc_ref[:, 0:128] = kc.T
        kcvc_ref[:, 128:256] = vc.T

    @pl.when(i >= n_prompt_tiles)
    def _():
        for j, t_ in enumerate((kc, vc, ks, vs)):
            nsas_ref[:, j * KV_A:(j + 1) * KV_A] = t_.T
        wins_ref[:, 0:128] = kw.T
        wins_ref[:, 128:256] = vw.T
        for c in range(WIDTH_B // LANES):
            foxs_ref[:, c * LANES:(c + 1) * LANES] = kb[c * LANES:(c + 1) * LANES].T
            foxs_ref[:, WIDTH_B + c * LANES:WIDTH_B + (c + 1) * LANES] = vb[c * LANES:(c + 1) * LANES].T
        vs_ref[...] = v_scr[...]


def _proj(x, gmix, w, wt, prm, col, rope, ropet, bd, wmix, btab, tm, batch, seq, n_sample):
    m, d = x.shape
    tps = seq // tm
    npt = batch * tps
    row = lambda width: pl.BlockSpec((tm, width), lambda i: (i, 0))
    full = lambda a: pl.BlockSpec(a.shape, lambda i: (0,) * a.ndim)
    kind = lambda i: (i >= npt).astype(jnp.int32)

    def featmajor(rows):
        def idx(i):
            ii = jnp.minimum(i, npt - 1)
            return (ii // tps, 0, ii % tps)
        return pl.BlockSpec((1, rows, tm), idx)

    prow = lambda width: pl.BlockSpec((tm, width), lambda i: (jnp.minimum(i, npt - 1), 0))
    srow = lambda width: pl.BlockSpec((tm, width), lambda i: (jnp.maximum(i - npt, 0), 0))
    n_prompt = batch * seq
    outs = [
        ("qa", row(2 * WIDTH_A), (m, 2 * WIDTH_A), BF16),
        ("qb", row(2 * WIDTH_B), (m, 2 * WIDTH_B), BF16),
        ("misc", row(LANES), (m, LANES), F32),
        ("oc", row(GM_WIDTH), (m, GM_WIDTH), BF16),
        ("kcvc", prow(2 * KV_A), (n_prompt, 2 * KV_A), F32),
        ("nsat", featmajor(4 * KV_A), (batch, 4 * KV_A, seq), F32),
        ("wint", featmajor(2 * KV_A), (batch, 2 * KV_A, seq), F32),
        ("foxt", featmajor(2 * WIDTH_B), (batch, 2 * WIDTH_B, seq), F32),
        ("faug", featmajor(FOX_HEADS * FAUG_ROWS), (batch, FOX_HEADS * FAUG_ROWS, seq), BF16),
        ("logft", featmajor(SUBLANES), (batch, SUBLANES, seq), F32),
        ("nsas", srow(4 * KV_A), (n_sample, 4 * KV_A), F32),
        ("wins", srow(2 * KV_A), (n_sample, 2 * KV_A), F32),
        ("foxs", srow(2 * WIDTH_B), (n_sample, 2 * WIDTH_B), F32),
        ("vs", srow(GM_WIDTH), (n_sample, GM_WIDTH), F32),
    ]
    res = pl.pallas_call(
        functools.partial(_proj_kernel, tm=tm, tiles_per_seq=tps, n_prompt_tiles=npt),
        grid=(m // tm,),
        in_specs=[row(d), full(gmix), full(w), full(wt), full(prm), full(col), row(3 * LANES),
                  pl.BlockSpec((ROT_DIM, tm), lambda i: (0, i)), full(bd),
                  pl.BlockSpec((1,) + wmix.shape[1:], lambda i: (kind(i), 0, 0, 0)),
                  pl.BlockSpec((1,) + btab.shape[1:], lambda i: (kind(i), 0, 0))],
        out_specs=[o[1] for o in outs],
        out_shape=[jax.ShapeDtypeStruct(o[2], o[3]) for o in outs],
        scratch_shapes=[pltpu.VMEM((tm, GM_WIDTH), F32), pltpu.VMEM((SUBLANES, LANES), F32)],
        compiler_params=_cparams(("arbitrary",)),
        name="proj",
    )(x, gmix, w, wt, prm, col, rope, ropet, bd, wmix, btab)
    return dict(zip([o[0] for o in outs], res))


def _mix_kernel(x_ref, gmix_ref, wmg_ref, oa_ref, ob_ref, oc_ref, wa_ref, wb_ref, wc_ref, wout_ref, o_ref):
    x = x_ref[...]
    d = x.shape[1]
    h = _rms_rows(x, gmix_ref[...]).astype(BF16)
    m = jax.nn.sigmoid(_dot(h, wmg_ref[:, 0:d])) * _dot(oa_ref[...].astype(BF16), wa_ref[...])
    m = m + jax.nn.sigmoid(_dot(h, wmg_ref[:, d:2 * d])) * _dot(ob_ref[...].astype(BF16), wb_ref[...])
    m = m + jax.nn.sigmoid(_dot(h, wmg_ref[:, 2 * d:3 * d])) * _dot(oc_ref[...], wc_ref[...])
    o_ref[...] = x + _dot(m.astype(BF16), wout_ref[...])


def _mix(x, gmix, wmg, oa, ob, oc, wa, wb, wc, wout, tm):
    m, d = x.shape
    row = lambda a: pl.BlockSpec((tm, a.shape[1]), lambda i: (i, 0))
    full = lambda a: pl.BlockSpec(a.shape, lambda i: (0,) * a.ndim)
    return pl.pallas_call(
        _mix_kernel,
        grid=(m // tm,),
        in_specs=[row(x), full(gmix), full(wmg), row(oa), row(ob), row(oc), full(wa), full(wb), full(wc),
                  full(wout)],
        out_specs=row(x),
        out_shape=jax.ShapeDtypeStruct((m, d), F32),
        compiler_params=_cparams(("parallel",)),
        name="mix",
    )(x, gmix, wmg, oa, ob, oc, wa, wb, wc, wout)


def _compress_kernel(x_ref, pe_ref, w_ref, o_ref):
    o_ref[0] = _dot((x_ref[0] + pe_ref[...]).astype(BF16), w_ref[...])


def _compress_prompt(x2, pe2, w2):
    b, nc, k = x2.shape
    n = w2.shape[1]
    return pl.pallas_call(
        _compress_kernel,
        grid=(b,),
        in_specs=[pl.BlockSpec((1, nc, k), lambda i: (i, 0, 0)),
                  pl.BlockSpec((1, k), lambda i: (0, 0)),
                  pl.BlockSpec((k, n), lambda i: (0, 0))],
        out_specs=pl.BlockSpec((1, nc, n), lambda i: (i, 0, 0)),
        out_shape=jax.ShapeDtypeStruct((b, nc, n), F32),
        compiler_params=_cparams(("parallel",)),
        name="compress_prompt",
    )(x2, pe2, w2)


def _online_update(s, vt_aug, acc_ref, m_ref):
    n_chunk = s.shape[1] // LANES
    m_old = m_ref[...]
    mx = s[:, 0:LANES]
    for c in range(1, n_chunk):
        mx = jnp.maximum(mx, s[:, c * LANES:(c + 1) * LANES])
    m_new = jnp.maximum(m_old, jnp.max(mx, axis=-1, keepdims=True))
    alpha = jnp.exp(m_old - m_new)
    p = jnp.concatenate(
        [jnp.exp(s[:, c * LANES:(c + 1) * LANES] - m_new).astype(BF16) for c in range(n_chunk)], axis=1)
    acc_ref[...] = alpha * acc_ref[...] + _dot_nt(p, vt_aug)
    m_ref[...] = m_new


def _finish(acc):
    return (acc / pltpu.roll(acc, HEAD_DIM, 1))[:, :HEAD_DIM]


def _select_blocks(imp_t, tok_row):
    nb, nt = imp_t.shape
    blk = lax.broadcasted_iota(jnp.int32, (nb, nt), 0)
    cur = jnp.broadcast_to(tok_row, (nb, nt)) // L_SEL
    forced = (blk == 0) | (blk == cur) | (blk == cur - 1)
    score = jnp.where(blk <= cur, jnp.where(forced, FORCE_BASE - blk.astype(F32), imp_t), -jnp.inf)
    sel = jnp.zeros((nb, nt), F32)
    for _ in range(min(TOP_N, nb)):
        mx = jnp.max(score, axis=0, keepdims=True)
        first = jnp.min(jnp.where(score == mx, blk, nb), axis=0, keepdims=True)
        hit = (blk == first) & (mx > -jnp.inf)
        sel = jnp.where(hit, 1.0, sel)
        score = jnp.where(hit, -jnp.inf, score)
    return sel


def _nsa_prompt_kernel(q_ref, misc_ref, kcv_ref, kv_ref, win_ref, o_ref, acc_ref, m_ref, *, n_cmp):
    i = pl.program_id(1)
    rows = NSA_HPG * Q_BLOCK
    tok0 = i * Q_BLOCK
    tok = tok0 + (lax.broadcasted_iota(jnp.int32, (rows, 1), 0) % Q_BLOCK)
    half = n_cmp // 2
    ones_rows = jnp.ones((HEAD_DIM, 1), BF16)
    zero_rows = jnp.zeros((HEAD_DIM, 1), BF16)

    q_los, q_augs, o_cs = [], [], []
    for g in range(NSA_GROUPS):
        gl = slice(g * LANES, (g + 1) * LANES)
        q_lo = jnp.concatenate(
            [q_ref[:, (g * NSA_HPG + h) * LANES:(g * NSA_HPG + h + 1) * LANES] for h in range(NSA_HPG)], axis=0)
        kcv = kcv_ref[0, :, gl].astype(BF16)
        s_c = _dot_nt(q_lo, kcv)
        col = lax.broadcasted_iota(jnp.int32, (1, n_cmp), 1)
        blk_c = 2 * (col % half) + col // half
        s_c = jnp.where((blk_c + 1) * L_CMP - 1 <= tok, s_c, NEG_BIG)
        e_c = jnp.exp(s_c - jnp.max(s_c, axis=-1, keepdims=True))
        p_c = e_c / jnp.sum(e_c, axis=-1, keepdims=True)
        p_c = jnp.where(tok >= L_CMP - 1, p_c, 0.0)
        o_cs.append(_dot(p_c.astype(BF16), kcv)[:, HEAD_DIM:])
        imp = p_c[0:Q_BLOCK]
        for h in range(1, NSA_HPG):
            imp = imp + p_c[h * Q_BLOCK:(h + 1) * Q_BLOCK]
        imp_sel = imp[:, :half] + imp[:, half:]
        if half < LANES:
            imp_sel = jnp.concatenate([imp_sel, jnp.zeros((Q_BLOCK, LANES - half), F32)], axis=1)
        sel_t = _select_blocks(imp_sel.T, tok0 + lax.broadcasted_iota(jnp.int32, (1, Q_BLOCK), 1))
        nsel = (1.0 - sel_t).T.astype(BF16)
        q_los.append(q_lo)
        q_augs.append(jnp.concatenate([q_lo, jnp.concatenate([nsel] * NSA_HPG, axis=0)], axis=1))

    acc_ref[...] = jnp.zeros_like(acc_ref)
    m_ref[...] = jnp.full_like(m_ref, -jnp.inf)
    n_kt = (tok0 + Q_BLOCK + KEY_TILE - 1) // KEY_TILE

    def sel_tile(kt, causal):
        ks = pl.ds(pl.multiple_of(kt * KEY_TILE, KEY_TILE), KEY_TILE)
        crow = lax.broadcasted_iota(jnp.int32, (LANES, KEY_TILE), 0)
        kblk = (kt * KEY_TILE + lax.broadcasted_iota(jnp.int32, (LANES, KEY_TILE), 1)) // L_SEL
        aug = jnp.where(crow == kblk, -MASK_BIG, 0.0).astype(BF16)
        for g in range(NSA_GROUPS):
            k_t = kv_ref[0, g * HEAD_DIM:(g + 1) * HEAD_DIM, ks].astype(BF16)
            v_t = kv_ref[0, KV_A + g * HEAD_DIM:KV_A + (g + 1) * HEAD_DIM, ks].astype(BF16)
            zeros = jnp.broadcast_to(zero_rows, (HEAD_DIM, KEY_TILE))
            s = _dot(q_augs[g], jnp.concatenate([k_t, zeros, aug], axis=0))
            if causal:
                keypos = kt * KEY_TILE + lax.broadcasted_iota(jnp.int32, (1, KEY_TILE), 1)
                s = jnp.where(keypos <= tok, s, -MASK_BIG)
            ones = jnp.broadcast_to(ones_rows, (HEAD_DIM, KEY_TILE))
            _online_update(s, jnp.concatenate([v_t, ones], axis=0), acc_ref.at[g], m_ref.at[g])

    def sel_body(j, carry):
        sel_tile(2 * j, False)
        sel_tile(2 * j + 1, False)
        return carry

    n_pairs = (n_kt - 1) // 2
    lax.fori_loop(0, n_pairs, sel_body, 0)

    @pl.when(2 * n_pairs < n_kt - 1)
    def _():
        sel_tile(n_kt - 2, False)

    sel_tile(n_kt - 1, True)

    band = WINDOW + Q_BLOCK
    start = pl.multiple_of(jnp.maximum(tok0 - WINDOW, 0), Q_BLOCK)
    wk = pl.ds(start, band)
    dpos = tok - (start + lax.broadcasted_iota(jnp.int32, (1, band), 1))
    ok = (dpos >= 0) & (dpos < WINDOW)
    gates = misc_ref[...]
    for g in range(NSA_GROUPS):
        k_t = win_ref[0, g * HEAD_DIM:(g + 1) * HEAD_DIM, wk].astype(BF16)
        v_t = win_ref[0, KV_A + g * HEAD_DIM:KV_A + (g + 1) * HEAD_DIM, wk].astype(BF16)
        zeros = jnp.broadcast_to(zero_rows, (HEAD_DIM, band))
        s = jnp.where(ok, _dot(q_los[g], jnp.concatenate([k_t, zeros], axis=0)), NEG_BIG)
        p = jnp.exp(s - jnp.max(s, axis=-1, keepdims=True)).astype(BF16)
        ones = jnp.broadcast_to(ones_rows, (HEAD_DIM, band))
        o_w = _finish(_dot_nt(p, jnp.concatenate([v_t, ones], axis=0)))
        o_s = _finish(acc_ref[g])
        o_c = o_cs[g]
        pieces = []
        for h in range(NSA_HPG):
            hr = slice(h * Q_BLOCK, (h + 1) * Q_BLOCK)
            c0 = GATE_LANE0 + (g * NSA_HPG + h) * 3
            pieces.append(gates[:, c0:c0 + 1] * o_c[hr] + gates[:, c0 + 1:c0 + 2] * o_s[hr]
                          + gates[:, c0 + 2:c0 + 3] * o_w[hr])
        o_ref[:, g * NSA_HPG * HEAD_DIM:(g + 1) * NSA_HPG * HEAD_DIM] = jnp.concatenate(pieces, axis=1)


def _nsa_prompt(qa, misc, kcv, nsat, wint, batch, seq):
    m = qa.shape[0]
    nq = seq // Q_BLOCK
    n_cmp = kcv.shape[1]
    rows = NSA_HPG * Q_BLOCK
    return pl.pallas_call(
        functools.partial(_nsa_prompt_kernel, n_cmp=n_cmp),
        grid=(batch, nq),
        in_specs=[pl.BlockSpec((Q_BLOCK, 2 * WIDTH_A), lambda b, i: (b * nq + i, 0)),
                  pl.BlockSpec((Q_BLOCK, LANES), lambda b, i: (b * nq + i, 0)),
                  pl.BlockSpec((1, n_cmp, 2 * KV_A), lambda b, i: (b, 0, 0)),
                  pl.BlockSpec((1, 2 * KV_A, seq), lambda b, i: (b, 1, 0)),
                  pl.BlockSpec((1, 2 * KV_A, seq), lambda b, i: (b, 0, 0))],
        out_specs=pl.BlockSpec((Q_BLOCK, WIDTH_A), lambda b, i: (b * nq + i, 0)),
        out_shape=jax.ShapeDtypeStruct((m, WIDTH_A), F32),
        scratch_shapes=[pltpu.VMEM((NSA_GROUPS, rows, LANES), F32), pltpu.VMEM((NSA_GROUPS, rows, LANES), F32)],
        compiler_params=_cparams(("parallel", "arbitrary")),
        name="nsa_prompt",
    )(qa, misc, kcv, nsat, wint)


def _fox_prompt_kernel(q_ref, kv_ref, faug_ref, o_ref, acc_ref, m_ref, *, qb):
    i = pl.program_id(1)
    tok = i * qb + lax.broadcasted_iota(jnp.int32, (qb, 1), 0)
    lane = lax.broadcasted_iota(jnp.int32, (qb, LANES), 1)
    ones_lanes = jnp.where(lane < HEAD_DIM + 3, 1.0, 0.0).astype(BF16)
    q_augs = [jnp.where(lane < HEAD_DIM, q_ref[:, h * LANES:(h + 1) * LANES], ones_lanes) for h in range(FOX_HEADS)]
    pad_rows = LANES - HEAD_DIM - FAUG_ROWS
    acc_ref[...] = jnp.zeros_like(acc_ref)
    m_ref[...] = jnp.full_like(m_ref, -jnp.inf)

    def tile(kt, causal):
        ks = pl.ds(pl.multiple_of(kt * KEY_TILE, KEY_TILE), KEY_TILE)
        zeros = jnp.zeros((pad_rows, KEY_TILE), BF16)
        ones = jnp.ones((HEAD_DIM, KEY_TILE), BF16)
        for h in range(FOX_HEADS):
            k_t = kv_ref[0, h * HEAD_DIM:(h + 1) * HEAD_DIM, ks].astype(BF16)
            v_t = kv_ref[0, WIDTH_B + h * HEAD_DIM:WIDTH_B + (h + 1) * HEAD_DIM, ks].astype(BF16)
            fa = faug_ref[0, h * FAUG_ROWS:(h + 1) * FAUG_ROWS, ks]
            s = _dot(q_augs[h], jnp.concatenate([k_t, fa, zeros], axis=0))
            if causal:
                keypos = kt * KEY_TILE + lax.broadcasted_iota(jnp.int32, (1, KEY_TILE), 1)
                s = jnp.where(keypos <= tok, s, -MASK_BIG)
            _online_update(s, jnp.concatenate([v_t, ones], axis=0), acc_ref.at[h], m_ref.at[h])

    def body(kt, carry):
        tile(kt, False)
        return carry

    n_full = i * (qb // KEY_TILE)
    lax.fori_loop(0, n_full, body, 0)
    for d in range(qb // KEY_TILE):
        tile(n_full + d, True)
    for h in range(FOX_HEADS):
        o_ref[:, h * HEAD_DIM:(h + 1) * HEAD_DIM] = _finish(acc_ref[h])


def _fox_prompt(qb_arr, foxt, faug, batch, seq):
    m = qb_arr.shape[0]
    qb = min(FOX_Q_BLOCK, seq)
    nq = seq // qb
    return pl.pallas_call(
        functools.partial(_fox_prompt_kernel, qb=qb),
        grid=(batch, nq),
        in_specs=[pl.BlockSpec((qb, 2 * WIDTH_B), lambda b, i: (b * nq + i, 0)),
                  pl.BlockSpec((1, 2 * WIDTH_B, seq), lambda b, i: (b, 0, 0)),
                  pl.BlockSpec((1, FOX_HEADS * FAUG_ROWS, seq), lambda b, i: (b, 0, 0))],
        out_specs=pl.BlockSpec((qb, WIDTH_B), lambda b, i: (b * nq + i, 0)),
        out_shape=jax.ShapeDtypeStruct((m, WIDTH_B), F32),
        scratch_shapes=[pltpu.VMEM((FOX_HEADS, qb, LANES), F32), pltpu.VMEM((FOX_HEADS, qb, LANES), F32)],
        compiler_params=_cparams(("parallel", "arbitrary")),
        name="fox_prompt",
    )(qb_arr, foxt, faug)


def _cmp_pool_kernel(x_ref, pe_ref, w_ref, o_ref, rows_ref, *, nb):
    per = PAGE // L_CMP

    def put(p, carry):
        base = pl.multiple_of(p * (per * CMP_PITCH), SUBLANES)
        for c in range(2):
            rows = x_ref[0, p, c * KV_A:(c + 1) * KV_A, :].T
            for n in range(per):
                rows_ref[c, pl.ds(base + n * CMP_PITCH, L_CMP), :] = rows[n * L_CMP:(n + 1) * L_CMP]
        return carry

    lax.fori_loop(0, nb, put, 0, unroll=2)
    n_blk = nb * per
    for c in range(2):
        cs = slice(c * KV_A, (c + 1) * KV_A)
        acc = jnp.zeros((n_blk, KV_A), F32)
        for l in range(L_CMP):
            xl = rows_ref[c, pl.ds(l, n_blk, stride=CMP_PITCH), :]
            acc = acc + _dot((xl + pe_ref[l:l + 1, cs]).astype(BF16), w_ref[l, cs, cs])
        o_ref[:, cs] = acc


def _compress_pool(cache_t, layer, pe_l, w_l, nb):
    n_pool = cache_t.shape[1]
    per = PAGE // L_CMP
    return pl.pallas_call(
        functools.partial(_cmp_pool_kernel, nb=nb),
        grid=(n_pool // nb,),
        in_specs=[pl.BlockSpec((1, nb, 2 * KV_A, PAGE), lambda i: (layer, i, 0, 0)),
                  pl.BlockSpec(pe_l.shape, lambda i: (0, 0)),
                  pl.BlockSpec(w_l.shape, lambda i: (0, 0, 0))],
        out_specs=pl.BlockSpec((nb * per, 2 * KV_A), lambda i: (i, 0)),
        out_shape=jax.ShapeDtypeStruct((n_pool * per, 2 * KV_A), F32),
        scratch_shapes=[pltpu.VMEM((2, nb * per * CMP_PITCH, KV_A), F32)],
        compiler_params=_cparams(("parallel",)),
        name="compress_pool",
    )(cache_t, pe_l, w_l)


def _softmax_pieces(pieces):
    mx = functools.reduce(jnp.maximum, [jnp.max(s, axis=-1, keepdims=True) for s in pieces])
    es = [jnp.exp(s - mx) for s in pieces]
    den = functools.reduce(lambda a, b: a + b, [jnp.sum(e, axis=-1, keepdims=True) for e in es])
    return es, den


def _nsa_sample_kernel(pt_ref, q_ref, misc_ref, new_ref, wnew_ref, state_ref, pair_ref, *rest, n_pages, past):
    del pt_ref
    pages = rest[:n_pages]
    cmps = rest[n_pages:2 * n_pages]
    o_ref, wout_ref = rest[2 * n_pages:]
    tn = q_ref.shape[0]
    rows = NSA_HEADS * tn
    rid = lax.broadcasted_iota(jnp.int32, (rows, 1), 0)
    qi = rid % tn
    tok = past + qi
    is_g1 = rid >= NSA_HPG * tn

    q = q_ref[...]
    blocks = []
    for g in range(NSA_GROUPS):
        for h in range(NSA_HPG):
            c = q[:, (g * NSA_HPG + h) * LANES:(g * NSA_HPG + h + 1) * LANES]
            blocks.append(pltpu.roll(c, HEAD_DIM, 1) if g == 1 else c)
    q64 = jnp.concatenate(blocks, axis=0).astype(BF16)

    def pick(x):
        return jnp.where(is_g1, x[:, HEAD_DIM:2 * HEAD_DIM], x[:, 0:HEAD_DIM])

    def pad_keys(x):
        return jnp.concatenate([x, jnp.zeros((2 * tn - x.shape[0], x.shape[1]), x.dtype)], axis=0)

    n_cmp = n_pages * (PAGE // L_CMP)
    kcv = jnp.concatenate([c[0] for c in cmps], axis=0).astype(BF16)
    s_c = _dot_nt(q64, kcv[:, 0:KV_A])
    blk_c = lax.broadcasted_iota(jnp.int32, (1, n_cmp), 1)
    s_c = jnp.where((blk_c + 1) * L_CMP - 1 <= tok, s_c, NEG_BIG)
    e_c = jnp.exp(s_c - jnp.max(s_c, axis=-1, keepdims=True))
    p_c = e_c / jnp.sum(e_c, axis=-1, keepdims=True)
    p_c = jnp.where(tok >= L_CMP - 1, p_c, 0.0)
    o_c = pick(_dot(p_c.astype(BF16), kcv[:, KV_A:2 * KV_A]))

    imps = []
    for g in range(NSA_GROUPS):
        imp = p_c[g * NSA_HPG * tn:g * NSA_HPG * tn + tn]
        for h in range(1, NSA_HPG):
            imp = imp + p_c[(g * NSA_HPG + h) * tn:(g * NSA_HPG + h + 1) * tn]
        imps.append(imp)
    imp2 = jnp.concatenate(imps + [jnp.zeros((LANES - NSA_GROUPS * tn, n_cmp), F32)], axis=0)
    imp_sel = jnp.zeros((LANES, LANES), F32)
    for part in _split3(imp2):
        imp_sel = imp_sel + _dot(part, pair_ref[...])
    tok_row = past + lax.broadcasted_iota(jnp.int32, (1, LANES), 1) % tn
    sel = _select_blocks(imp_sel.T, tok_row).T
    bias2 = jnp.where(sel > 0.0, 0.0, -MASK_BIG)
    bias_sel = jnp.concatenate([bias2[0:tn]] * NSA_HPG + [bias2[tn:2 * tn]] * NSA_HPG, axis=0)

    lane_lo = lax.broadcasted_iota(jnp.int32, (rows, PAGE), 1) < L_SEL
    new_kv = pad_keys(new_ref[:, 2 * KV_A:4 * KV_A]).astype(BF16)
    scores = []
    for j in range(n_pages):
        bias = jnp.where(lane_lo, bias_sel[:, 2 * j:2 * j + 1], bias_sel[:, 2 * j + 1:2 * j + 2])
        scores.append(_dot(q64, pages[j][0, 0, 0:KV_A, :].astype(BF16)) + bias)
    kcol = lax.broadcasted_iota(jnp.int32, (1, 2 * tn), 1)
    s_new = _dot_nt(q64, new_kv[:, 0:KV_A]) + bias_sel[:, 2 * n_pages:2 * n_pages + 1]
    scores.append(jnp.where((kcol <= qi) & (kcol < tn), s_new, -MASK_BIG))
    probs, den = _softmax_pieces(scores)
    acc = _dot(probs[n_pages].astype(BF16), new_kv[:, KV_A:2 * KV_A])
    for j in range(n_pages):
        acc = acc + _dot_nt(probs[j].astype(BF16), pages[j][0, 0, KV_A:2 * KV_A, :].astype(BF16))
    o_s = pick(acc) / den

    wb = state_ref.shape[3]
    st = state_ref[0, 0]
    wnew = wnew_ref[...]
    wn_b = pad_keys(wnew).astype(BF16)
    kpos = past - wb + lax.broadcasted_iota(jnp.int32, (1, wb), 1)
    dpos = tok - kpos
    s1 = jnp.where((dpos >= 0) & (dpos < WINDOW) & (kpos >= 0), _dot(q64, st[0:KV_A].astype(BF16)), NEG_BIG)
    kcol = lax.broadcasted_iota(jnp.int32, (1, 2 * tn), 1)
    s2 = jnp.where((kcol <= qi) & (kcol < tn), _dot_nt(q64, wn_b[:, 0:KV_A]), NEG_BIG)
    (p1, p2), den_w = _softmax_pieces([s1, s2])
    o_w = pick(_dot_nt(p1.astype(BF16), st[KV_A:2 * KV_A].astype(BF16))
               + _dot(p2.astype(BF16), wn_b[:, KV_A:2 * KV_A])) / den_w

    new_t = jnp.concatenate([jnp.zeros((LANES - tn, 2 * KV_A), F32), wnew], axis=0).T
    shifted = pltpu.roll(st, wb - tn, 1)
    wout_ref[0, 0, :, 0:wb - LANES] = shifted[:, 0:wb - LANES]
    lane_w = lax.broadcasted_iota(jnp.int32, (2 * KV_A, LANES), 1)
    wout_ref[0, 0, :, wb - LANES:wb] = jnp.where(lane_w >= LANES - tn, new_t, shifted[:, wb - LANES:wb])

    gates = misc_ref[...]
    pieces = []
    for hh in range(NSA_HEADS):
        hr = slice(hh * tn, (hh + 1) * tn)
        c0 = GATE_LANE0 + hh * 3
        pieces.append(gates[:, c0:c0 + 1] * o_c[hr] + gates[:, c0 + 1:c0 + 2] * o_s[hr]
                      + gates[:, c0 + 2:c0 + 3] * o_w[hr])
    o_ref[...] = jnp.concatenate(pieces, axis=1)


def _nsa_sample(page_table, layer, qa_s, misc_s, nsa_s, win_s, state_t, pair, cache_t, cmp_pool3, oa_full, n_prompt):
    nseq, n_pages = page_table.shape
    tn = qa_s.shape[0] // nseq
    past = n_pages * PAGE
    wb = state_t.shape[3]
    row = lambda width: pl.BlockSpec((tn, width), lambda b, pt: (b, 0))
    page_specs = [pl.BlockSpec((1, 1, 2 * KV_A, PAGE), functools.partial(lambda b, pt, j: (layer, pt[b, j], 1, 0), j=j))
                  for j in range(n_pages)]
    cmp_specs = [pl.BlockSpec((1, PAGE // L_CMP, 2 * KV_A), functools.partial(lambda b, pt, j: (pt[b, j], 0, 0), j=j))
                 for j in range(n_pages)]
    grid_spec = pltpu.PrefetchScalarGridSpec(
        num_scalar_prefetch=1,
        grid=(nseq,),
        in_specs=[row(2 * WIDTH_A), row(LANES), row(4 * KV_A), row(2 * KV_A),
                  pl.BlockSpec((1, 1, 2 * KV_A, wb), lambda b, pt: (layer, b, 0, 0)),
                  pl.BlockSpec(pair.shape, lambda b, pt: (0, 0))] + page_specs + cmp_specs
                 + [pl.BlockSpec(memory_space=pl.ANY)],
        out_specs=[pl.BlockSpec((tn, WIDTH_A), lambda b, pt: (n_prompt // tn + b, 0)),
                   pl.BlockSpec((1, 1, 2 * KV_A, wb), lambda b, pt: (0, b, 0, 0))],
    )

    def body(pt_ref, *refs):
        ins = refs[:6 + 2 * n_pages]
        outs = refs[7 + 2 * n_pages:]
        _nsa_sample_kernel(pt_ref, *ins, *outs, n_pages=n_pages, past=past)

    n_in = 1 + 6 + 2 * n_pages
    return pl.pallas_call(
        body,
        grid_spec=grid_spec,
        out_shape=[jax.ShapeDtypeStruct(oa_full.shape, F32), jax.ShapeDtypeStruct((1, nseq, 2 * KV_A, wb), F32)],
        input_output_aliases={n_in: 0},
        compiler_params=_cparams(("arbitrary",)),
        name="nsa_sample",
    )(page_table, qa_s, misc_s, nsa_s, win_s, state_t, pair, *([cache_t] * n_pages), *([cmp_pool3] * n_pages), oa_full)


def _fox_sample_kernel(pt_ref, q_ref, new_ref, lfn_ref, ustrict_ref, *rest, n_pages):
    del pt_ref
    pages = rest[:n_pages]
    lfps = rest[n_pages:2 * n_pages]
    o_ref = rest[2 * n_pages]
    tn = q_ref.shape[0]
    rows = FOX_HEADS * tn
    qi = lax.broadcasted_iota(jnp.int32, (rows, 1), 0) % tn

    q = q_ref[...]
    zeros_half = jnp.zeros((tn, LANES), F32)
    blocks = []
    for h in range(FOX_HEADS):
        c = q[:, h * LANES:(h + 1) * LANES]
        if h % 2 == 1:
            c = pltpu.roll(c, HEAD_DIM, 1)
        blocks.append(jnp.concatenate([c, zeros_half] if h < 2 else [zeros_half, c], axis=1))
    q32 = jnp.concatenate(blocks, axis=0).astype(BF16)

    lf = jnp.concatenate([r[0, 0] for r in lfps], axis=0)
    within = jnp.zeros(lf.shape, F32)
    for part in _split3(lf):
        within = within + _dot(part, ustrict_ref[...])
    tot = jnp.sum(lf, axis=-1, keepdims=True)
    after = jnp.zeros((FOX_HEADS, 1), F32)
    page_bias = [None] * n_pages
    for j in reversed(range(n_pages)):
        b4 = within[j * FOX_HEADS:(j + 1) * FOX_HEADS] + after
        page_bias[j] = jnp.concatenate(
            [jnp.broadcast_to(b4[h:h + 1], (tn, PAGE)) for h in range(FOX_HEADS)], axis=0)
        after = after + tot[j * FOX_HEADS:(j + 1) * FOX_HEADS]

    lfn = lfn_ref[0]
    run = lfn[:, 0:1]
    cols = [run]
    for r in range(1, tn):
        run = run + lfn[:, r:r + 1]
        cols.append(run)
    cum = jnp.concatenate(cols + [jnp.zeros((FOX_HEADS, tn), F32)], axis=1)
    new_bias = jnp.concatenate(
        [jnp.broadcast_to(-cum[h:h + 1], (tn, 2 * tn)) for h in range(FOX_HEADS)], axis=0)

    new = jnp.concatenate([new_ref[...], jnp.zeros((tn, 2 * WIDTH_B), F32)], axis=0).astype(BF16)
    scores = [_dot(q32, pages[j][0, 0, 0:WIDTH_B, :].astype(BF16)) + page_bias[j] for j in range(n_pages)]
    kcol = lax.broadcasted_iota(jnp.int32, (1, 2 * tn), 1)
    scores.append(jnp.where((kcol <= qi) & (kcol < tn), _dot_nt(q32, new[:, :WIDTH_B]) + new_bias, -MASK_BIG))
    probs, den = _softmax_pieces(scores)
    acc = _dot(probs[n_pages].astype(BF16), new[:, WIDTH_B:])
    for j in range(n_pages):
        acc = acc + _dot_nt(probs[j].astype(BF16), pages[j][0, 0, WIDTH_B:2 * WIDTH_B, :].astype(BF16))
    o32 = acc / den
    lane_head = lax.broadcasted_iota(jnp.int32, (tn, WIDTH_B), 1) // HEAD_DIM
    out = jnp.zeros((tn, WIDTH_B), F32)
    for h in range(FOX_HEADS):
        out = out + jnp.where(lane_head == h, o32[h * tn:(h + 1) * tn], 0.0)
    o_ref[...] = out


def _fox_sample(page_table, layer, qb_s, fox_s, lfn_t, ustrict, cache_t, logf_t, ob_full, n_prompt):
    nseq, n_pages = page_table.shape
    tn = qb_s.shape[0] // nseq
    row = lambda width: pl.BlockSpec((tn, width), lambda b, pt: (b, 0))
    page_specs = [pl.BlockSpec((1, 1, 2 * WIDTH_B, PAGE), functools.partial(lambda b, pt, j: (layer, pt[b, j], 0, 0), j=j))
                  for j in range(n_pages)]
    lf_specs = [pl.BlockSpec((1, 1, FOX_HEADS, PAGE), functools.partial(lambda b, pt, j: (layer, pt[b, j], 0, 0), j=j))
                for j in range(n_pages)]
    grid_spec = pltpu.PrefetchScalarGridSpec(
        num_scalar_prefetch=1,
        grid=(nseq,),
        in_specs=[row(2 * WIDTH_B), row(2 * WIDTH_B),
                  pl.BlockSpec((1, FOX_HEADS, tn), lambda b, pt: (b, 0, 0)),
                  pl.BlockSpec(ustrict.shape, lambda b, pt: (0, 0))] + page_specs + lf_specs
                 + [pl.BlockSpec(memory_space=pl.ANY)],
        out_specs=pl.BlockSpec((tn, WIDTH_B), lambda b, pt: (n_prompt // tn + b, 0)),
    )

    def body(pt_ref, *refs):
        ins = refs[:4 + 2 * n_pages]
        outs = refs[5 + 2 * n_pages:]
        _fox_sample_kernel(pt_ref, *ins, *outs, n_pages=n_pages)

    n_in = 1 + 4 + 2 * n_pages
    return pl.pallas_call(
        body,
        grid_spec=grid_spec,
        out_shape=jax.ShapeDtypeStruct(ob_full.shape, F32),
        input_output_aliases={n_in: 0},
        compiler_params=_cparams(("arbitrary",)),
        name="fox_sample",
    )(page_table, qb_s, fox_s, lfn_t, ustrict, *([cache_t] * n_pages), *([logf_t] * n_pages), ob_full)


def _in_offsets():
    splits = (WIDTH_A, KV_A, KV_A, KV_A, KV_A, KV_A, KV_A, 3 * NSA_HEADS, WIDTH_B, WIDTH_B, WIDTH_B, FOX_HEADS,
              GM_WIDTH, GM_WIDTH)
    offs = np.concatenate([[0], np.cumsum(splits)])
    names = ("qa", "kc", "vc", "ks", "vs", "kw", "vw", "ga", "qb", "kb", "vb", "fl", "gu", "gv", "mg")
    return {n: int(o) for n, o in zip(names, offs)}


def _pack_params(w_in, qk_gain, gm_norm, b_forget, d_model):
    depth = w_in.shape[0]
    o = _in_offsets()
    cols = np.concatenate([
        np.arange(o["qa"], o["qa"] + WIDTH_A),
        np.arange(o["qb"], o["qb"] + WIDTH_B),
        np.arange(o["gu"], o["gu"] + 2 * GM_WIDTH),
        np.arange(o["fl"], o["fl"] + FOX_HEADS),
        np.arange(o["ga"], o["ga"] + 3 * NSA_HEADS)])
    w_p = jnp.take(w_in, jnp.asarray(cols), axis=2)
    w_p = jnp.pad(w_p, ((0, 0), (0, 0), (0, P_COLS - w_p.shape[2]))).astype(BF16)
    rows_t = np.concatenate([
        np.arange(o["kc"], o["kc"] + 6 * KV_A),
        np.arange(o["kb"], o["kb"] + 2 * WIDTH_B),
        np.arange(o["fl"], o["fl"] + FOX_HEADS)])
    w_t = jnp.swapaxes(jnp.take(w_in, jnp.asarray(rows_t), axis=2), 1, 2)
    w_t = jnp.pad(w_t, ((0, 0), (0, T_ROWS - w_t.shape[1]), (0, 0))).astype(BF16)
    w_mg = w_in[:, :, o["mg"]:o["mg"] + 3 * d_model].astype(BF16)
    tile = lambda g, n: jnp.tile(g, (1, n))
    prm = jnp.concatenate([
        tile(qk_gain[:, 0], NSA_HEADS), tile(qk_gain[:, 4], FOX_HEADS), gm_norm,
        b_forget, jnp.zeros((depth, LANES - FOX_HEADS), F32)], axis=1)
    assert prm.shape[1] == R_COLS
    ones = jnp.ones((depth, KV_A), F32)
    col = jnp.concatenate([
        tile(qk_gain[:, 1], NSA_GROUPS), ones, tile(qk_gain[:, 2], NSA_GROUPS), ones,
        tile(qk_gain[:, 3], NSA_GROUPS), ones, tile(qk_gain[:, 5], FOX_HEADS), jnp.ones((depth, WIDTH_B), F32),
        b_forget, jnp.zeros((depth, SUBLANES - FOX_HEADS), F32)], axis=1)
    assert col.shape[1] == T_ROWS
    return w_p, w_t, w_mg, prm[:, None, :], col[:, :, None]


def _rope_tables(pos):
    half = ROT_HALF
    inv = ROPE_THETA ** (-jnp.arange(half, dtype=F32) / half)
    ang = pos.astype(F32)[:, None] * inv[None, :]
    cos, sin = jnp.cos(ang), jnp.sin(ang)
    n = pos.shape[0]
    zero = jnp.zeros((n, HEAD_DIM - ROT_DIM), F32)
    zero8 = jnp.zeros((n, half), F32)
    c64 = jnp.concatenate([cos, cos, zero + 1.0], axis=1)
    lo64 = jnp.concatenate([-sin, zero8, zero], axis=1)
    hi64 = jnp.concatenate([zero8, sin, zero], axis=1)
    rep = lambda t: jnp.tile(t, (1, LANES // HEAD_DIM))
    token_major = jnp.concatenate([rep(c64), rep(lo64), rep(hi64)], axis=1)
    feature_major = jnp.concatenate([cos.T, sin.T], axis=0)
    return token_major, feature_major


def _gmlp_tables(tn, w_spatial, b_spatial):
    w_p = jnp.tril(w_spatial)
    eye = jnp.asarray(np.eye(CHUNK // tn), F32)
    w_s = jnp.einsum("ab,lgts->lgatbs", eye, jnp.tril(w_spatial[:, :, :tn, :tn]))
    w_s = w_s.reshape(w_spatial.shape[0], GM_GROUPS, CHUNK, CHUNK)
    wmix = jnp.stack([w_p, w_s], axis=1).astype(BF16)
    b_p = jnp.repeat(jnp.swapaxes(b_spatial, 1, 2), GM_DIM, axis=2)
    b_s = jnp.tile(b_p[:, :tn], (1, CHUNK // tn, 1))
    btab = jnp.stack([b_p, b_s], axis=1)
    return wmix, btab


def _compress_weights(w_cmp, pe_cmp):
    depth = w_cmp.shape[0]
    eye_g = jnp.asarray(np.eye(NSA_GROUPS), F32)
    w_l = jnp.einsum("kq,gh,zklde->zlkgdqhe", jnp.asarray(np.eye(2), F32), eye_g, w_cmp)
    w_l = w_l.reshape(depth, L_CMP, 2 * KV_A, 2 * KV_A)
    pe_l = jnp.broadcast_to(pe_cmp[:, :, :, None, :], (depth, 2, L_CMP, NSA_GROUPS, HEAD_DIM))
    pe_l = jnp.transpose(pe_l, (0, 2, 1, 3, 4)).reshape(depth, L_CMP, 2 * KV_A)
    return w_l.astype(BF16), pe_l


def _group_major(x):
    s = x.shape[:-1]
    return jnp.swapaxes(x.reshape(s + (2, NSA_GROUPS, HEAD_DIM)), -3, -2).reshape(s + (2 * KV_A,))


def _feature_major(x, n_feat):
    nd = x.ndim
    perm = tuple(range(nd - 4)) + (nd - 3, nd - 2, nd - 1, nd - 4)
    y = jnp.transpose(x, perm)
    return y.reshape(y.shape[:nd - 4] + (n_feat, y.shape[-1]))


def _token_major_view(y, dims):
    lead = y.shape[:-2]
    z = y.reshape(lead + tuple(dims) + (y.shape[-1],))
    nd = z.ndim
    perm = tuple(range(nd - 4)) + (nd - 1, nd - 4, nd - 3, nd - 2)
    return jnp.transpose(z, perm)


def kernel(x_prompt, x_sample, cache_nsa_kv, cache_fox_kv, cache_fox_logf, state_nsa_win, page_table,
           g_ffn_a, w_ffn_a_gu, w_ffn_a_down, g_mix, w_in, b_forget, qk_gain, w_cmp, pe_cmp,
           gm_norm, w_spatial, b_spatial, w_branch_a, w_branch_b, w_branch_c, w_out,
           g_ffn_b, w_ffn_b_gu, w_ffn_b_down):
    batch, seq, d_model = x_prompt.shape
    nseq, tn, _ = x_sample.shape
    depth = w_in.shape[0]
    n_pool = cache_nsa_kv.shape[1]
    n_pages = page_table.shape[1]
    past = n_pages * PAGE
    wb = state_nsa_win.shape[2]
    n_prompt = batch * seq
    n_sample = nseq * tn
    tm = next(t for t in (512, 256, 128) if seq % t == 0 and n_sample % t == 0)
    assert seq % FOX_Q_BLOCK == 0 or seq < FOX_Q_BLOCK
    assert seq >= WINDOW + Q_BLOCK and wb % LANES == 0 and wb > LANES
    assert CHUNK % tn == 0 and past % L_SEL == 0 and wb >= tn

    x = jnp.concatenate([x_prompt.reshape(n_prompt, d_model), x_sample.reshape(n_sample, d_model)], axis=0)
    pos = jnp.concatenate([jnp.tile(jnp.arange(seq), batch), jnp.tile(past + jnp.arange(tn), nseq)])
    rope, rope_t = _rope_tables(pos)

    w_p, w_t, w_mg, prm, col = _pack_params(w_in, qk_gain, gm_norm, b_forget, d_model)
    bd = jnp.asarray(np.kron(np.eye(LANES // HEAD_DIM), np.full((HEAD_DIM, HEAD_DIM), 1.0 / HEAD_DIM)), BF16)
    wmix, btab = _gmlp_tables(tn, w_spatial, b_spatial)
    w_l, pe_l = _compress_weights(w_cmp, pe_cmp)
    w2 = _group_major(w_l.reshape(depth, L_CMP * 2 * KV_A, 2 * KV_A))
    pe2 = pe_l.reshape(depth, 1, L_CMP * 2 * KV_A)
    n_cmp_p = seq // L_CMP

    bf = lambda a: a.astype(BF16)
    wgu_a, wd_a, wgu_b, wd_b = bf(w_ffn_a_gu), bf(w_ffn_a_down), bf(w_ffn_b_gu), bf(w_ffn_b_down)
    wa, wb_, wc, wo = bf(w_branch_a), bf(w_branch_b), bf(w_branch_c), bf(w_out)

    cache_nsa_t = _feature_major(cache_nsa_kv, 4 * KV_A)
    cache_fox_t = _feature_major(cache_fox_kv, 2 * WIDTH_B)
    state_t = _feature_major(state_nsa_win, 2 * KV_A)
    logf_t = jnp.swapaxes(cache_fox_logf, 2, 3)
    nb = next(t for t in (64, 32, 16, 8, 4, 2, 1) if n_pool % t == 0)
    n_sel = past // L_SEL + 1
    pair = np.zeros((n_pages * (PAGE // L_CMP), LANES))
    for c in range(pair.shape[0]):
        pair[c, c // 2] = 1.0
    pair = jnp.asarray(pair, BF16)
    ustrict = jnp.asarray(np.triu(np.ones((PAGE, PAGE)), 1).T, BF16)
    assert n_sel <= LANES

    outs = {k: [] for k in ("nsat", "nsas", "wint", "win_s", "foxt", "foxs", "logft", "logfs", "gmv")}
    for l in range(depth):
        x = _ffn(x, g_ffn_a[l][None], wgu_a[l], wd_a[l], tm)
        pr = _proj(x, g_mix[l][None], w_p[l], w_t[l], prm[l], col[l], rope, rope_t, bd, wmix[l], btab[l],
                   tm, batch, seq, n_sample)

        blocks = pr["kcvc"].reshape(batch, n_cmp_p // 2, 2, L_CMP * 2 * KV_A)
        x2 = jnp.swapaxes(blocks, 1, 2).reshape(batch, n_cmp_p, L_CMP * 2 * KV_A)
        kcv = _compress_prompt(x2, pe2[l], w2[l])
        oa = _nsa_prompt(pr["qa"], pr["misc"], kcv, pr["nsat"], pr["wint"], batch, seq)
        ob = _fox_prompt(pr["qb"], pr["foxt"], pr["faug"], batch, seq)

        cmp_pool = _compress_pool(cache_nsa_t, l, pe_l[l], w_l[l], nb)
        cmp_pool3 = cmp_pool.reshape(n_pool, PAGE // L_CMP, 2 * KV_A)
        sl = slice(n_prompt, None)
        oa, win_s = _nsa_sample(page_table, l, pr["qa"][sl].astype(F32), pr["misc"][sl], pr["nsas"],
                                pr["wins"], state_t, pair, cache_nsa_t, cmp_pool3, oa, n_prompt)
        lfn_t = jnp.swapaxes(pr["misc"][sl, :FOX_HEADS].reshape(nseq, tn, FOX_HEADS), 1, 2)
        ob = _fox_sample(page_table, l, pr["qb"][sl].astype(F32), pr["foxs"], lfn_t, ustrict, cache_fox_t,
                         logf_t, ob, n_prompt)

        x = _mix(x, g_mix[l][None], w_mg[l], oa, ob, pr["oc"], wa[l], wb_[l], wc[l], wo[l], tm)
        x = _ffn(x, g_ffn_b[l][None], wgu_b[l], wd_b[l], tm)

        outs["nsat"].append(pr["nsat"])
        outs["nsas"].append(pr["nsas"])
        outs["wint"].append(pr["wint"][:, :, seq - min(WINDOW, seq):])
        outs["win_s"].append(win_s[0])
        outs["foxt"].append(pr["foxt"])
        outs["foxs"].append(pr["foxs"])
        outs["logft"].append(pr["logft"][:, :FOX_HEADS])
        outs["logfs"].append(pr["misc"][sl, :FOX_HEADS])
        outs["gmv"].append(pr["vs"])

    st = {k: jnp.stack(v) for k, v in outs.items()}
    return (
        x[:n_prompt].reshape(batch, seq, d_model),
        x[n_prompt:].reshape(nseq, tn, d_model),
        _token_major_view(st["nsat"], (4, NSA_GROUPS, HEAD_DIM)),
        st["nsas"].reshape(depth, nseq, tn, 4, NSA_GROUPS, HEAD_DIM),
        _token_major_view(st["wint"], (2, NSA_GROUPS, HEAD_DIM)),
        _token_major_view(st["win_s"], (2, NSA_GROUPS, HEAD_DIM)),
        _token_major_view(st["foxt"], (2, FOX_HEADS, HEAD_DIM)),
        st["foxs"].reshape(depth, nseq, tn, 2, FOX_HEADS, HEAD_DIM),
        jnp.swapaxes(st["logft"], 2, 3),
        st["logfs"].reshape(depth, nseq, tn, FOX_HEADS),
        st["gmv"].reshape(depth, nseq, tn, GM_WIDTH),
    )
```

```python
import functools

import numpy as np
import jax
import jax.numpy as jnp
from jax import lax
from jax.experimental import pallas as pl
from jax.experimental.pallas import tpu as pltpu

F32 = jnp.float32
BF16 = jnp.bfloat16

HEAD_DIM = 64
ROT_DIM = HEAD_DIM // 4
ROT_HALF = ROT_DIM // 2
ROPE_THETA = 500000.0
NSA_HEADS = 8
NSA_GROUPS = 2
NSA_HPG = NSA_HEADS // NSA_GROUPS
L_CMP = 32
L_SEL = 64
TOP_N = 16
WINDOW = 512
FOX_HEADS = 4
GM_GROUPS = 4
GM_DIM = 64
GM_WIDTH = GM_GROUPS * GM_DIM
CHUNK = 128
PAGE = 128
EPS = 1e-6
NEG_BIG = -1e30
WIDTH_A = NSA_HEADS * HEAD_DIM
WIDTH_B = FOX_HEADS * HEAD_DIM
KV_A = NSA_GROUPS * HEAD_DIM
QK_SCALE = HEAD_DIM ** -0.5

LANES = 128
SUBLANES = 8
Q_BLOCK = 256
KEY_TILE = 256
FOX_Q_BLOCK = 1024
FAUG_ROWS = 16
SEQ_PER_STEP = 2
CMP_PITCH = L_CMP + 4
MASK_BIG = float(2 ** 30)
N_FORCED = 3
VMEM_LIMIT = 56 * 1024 * 1024

P_QA = 0
P_QB = 512
P_GM = 768
P_MISC = 1280
P_COLS = 1408
T_KC, T_VC, T_KS, T_VS, T_KW, T_VW, T_KB, T_VB, T_FL, T_ROWS = 0, 128, 256, 384, 512, 640, 768, 1024, 1280, 1288
R_GQ, R_GQB, R_GMN, R_BF, R_COLS = 0, 512, 768, 1024, 1152
GATE_LANE0 = FOX_HEADS


def _dot(a, b):
    return jnp.dot(a, b, preferred_element_type=F32)


def _dot_nt(a, b):
    return lax.dot_general(a, b, (((1,), (1,)), ((), ())), preferred_element_type=F32)


def _split3(x):
    a = x.astype(BF16)
    r = x - a.astype(F32)
    b = r.astype(BF16)
    c = (r - b.astype(F32)).astype(BF16)
    return a, b, c


def _rms_rows(x, g):
    return x * lax.rsqrt(jnp.mean(x * x, axis=-1, keepdims=True) + EPS) * g


def _log_sigmoid(z):
    return jnp.minimum(z, 0.0) - jnp.log(1.0 + jnp.exp(-jnp.abs(z)))


def _cparams(sem):
    return pltpu.CompilerParams(dimension_semantics=sem, vmem_limit_bytes=VMEM_LIMIT)


def _ffn_kernel(x_ref, g_ref, wgu_ref, wd_ref, o_ref, acc_ref, *, d_ff, chunk):
    x = x_ref[...]
    h = _rms_rows(x, g_ref[...]).astype(BF16)
    acc_ref[...] = jnp.zeros_like(acc_ref)
    for c in range(d_ff // chunk):
        g = _dot(h, wgu_ref[:, c * chunk:(c + 1) * chunk])
        u = _dot(h, wgu_ref[:, d_ff + c * chunk:d_ff + (c + 1) * chunk])
        a = (jax.nn.silu(g) * u).astype(BF16)
        acc_ref[...] += _dot(a, wd_ref[c * chunk:(c + 1) * chunk, :])
    o_ref[...] = x + 0.5 * acc_ref[...]


def _ffn(x, g, wgu, wd, tm):
    m, d = x.shape
    d_ff = wd.shape[0]
    return pl.pallas_call(
        functools.partial(_ffn_kernel, d_ff=d_ff, chunk=256),
        grid=(m // tm,),
        in_specs=[pl.BlockSpec((tm, d), lambda i: (i, 0)),
                  pl.BlockSpec((1, d), lambda i: (0, 0)),
                  pl.BlockSpec((d, 2 * d_ff), lambda i: (0, 0)),
                  pl.BlockSpec((d_ff, d), lambda i: (0, 0))],
        out_specs=pl.BlockSpec((tm, d), lambda i: (i, 0)),
        out_shape=jax.ShapeDtypeStruct((m, d), F32),
        scratch_shapes=[pltpu.VMEM((tm, d), F32)],
        compiler_params=_cparams(("parallel",)),
        name="ffn",
    )(x, g, wgu, wd)


def _proj_kernel(x_ref, gmix_ref, w_ref, wt_ref, prm_ref, col_ref, rope_ref, ropet_ref, bd_ref, wmix_ref, btab_ref,
                 qa_ref, qb_ref, misc_ref, oc_ref, kcvc_ref, nsat_ref, wint_ref, foxt_ref, faug_ref, logft_ref,
                 nsas_ref, wins_ref, foxs_ref, vs_ref, v_scr, carry_ref, *, tm, tiles_per_seq, n_prompt_tiles):
    i = pl.program_id(0)
    x = x_ref[...]
    h = _rms_rows(x, gmix_ref[...]).astype(BF16)
    lane = lax.broadcasted_iota(jnp.int32, (tm, LANES), 1)
    lo64 = lane < HEAD_DIM

    cos = rope_ref[:, 0:128]
    sin_lo = rope_ref[:, 128:256]
    sin_hi = rope_ref[:, 256:384]
    bd = bd_ref[...]

    def seg(a, b):
        return _dot(h, w_ref[:, a:b])

    def headnorm(t, gain):
        hi, lo, _ = _split3(t * t)
        ms = _dot(hi, bd) + _dot(lo, bd)
        return t * lax.rsqrt(ms + EPS) * gain

    def rope(t):
        return t * cos + pltpu.roll(t, LANES - ROT_HALF, 1) * sin_lo + pltpu.roll(t, ROT_HALF, 1) * sin_hi

    def head_split(t):
        return (jnp.where(lo64, t, 0.0).astype(BF16),
                jnp.where(lo64, pltpu.roll(t, HEAD_DIM, 1), 0.0).astype(BF16))

    for c in range(WIDTH_A // LANES):
        t = seg(P_QA + c * LANES, P_QA + (c + 1) * LANES)
        t = rope(headnorm(t, prm_ref[:, R_GQ + c * LANES:R_GQ + (c + 1) * LANES])) * QK_SCALE
        a, b = head_split(t)
        qa_ref[:, (2 * c) * LANES:(2 * c + 1) * LANES] = a
        qa_ref[:, (2 * c + 1) * LANES:(2 * c + 2) * LANES] = b
    for c in range(WIDTH_B // LANES):
        t = seg(P_QB + c * LANES, P_QB + (c + 1) * LANES)
        t = headnorm(t, prm_ref[:, R_GQB + c * LANES:R_GQB + (c + 1) * LANES]) * QK_SCALE
        a, b = head_split(t)
        qb_ref[:, (2 * c) * LANES:(2 * c + 1) * LANES] = a
        qb_ref[:, (2 * c + 1) * LANES:(2 * c + 2) * LANES] = b

    lane_grp = lax.broadcasted_iota(jnp.int32, (CHUNK, GM_WIDTH), 1) // GM_DIM
    for c in range(GM_WIDTH // LANES):
        gv = jax.nn.gelu(seg(P_GM + GM_WIDTH + c * LANES, P_GM + GM_WIDTH + (c + 1) * LANES))
        v_scr[:, c * LANES:(c + 1) * LANES] = headnorm(gv, prm_ref[:, R_GMN + c * LANES:R_GMN + (c + 1) * LANES])
    for r in range(tm // CHUNK):
        rows = slice(r * CHUNK, (r + 1) * CHUNK)
        vsub = v_scr[rows, :]
        s = btab_ref[0]
        for g in range(GM_GROUPS):
            s = s + _dot(wmix_ref[0, g], jnp.where(lane_grp == g, vsub, 0.0).astype(BF16))
        u = jnp.concatenate(
            [jax.nn.gelu(_dot(h[rows, :], w_ref[:, P_GM + c * LANES:P_GM + (c + 1) * LANES]))
             for c in range(GM_WIDTH // LANES)], axis=1)
        oc_ref[rows, :] = (u * s).astype(BF16)

    t = seg(P_MISC, P_MISC + LANES)
    logf = _log_sigmoid(t + prm_ref[:, R_BF:R_BF + LANES])
    misc_ref[...] = jnp.where(lane < FOX_HEADS, logf, jax.nn.sigmoid(t))

    cos_t = ropet_ref[0:ROT_HALF, :]
    sin_t = ropet_ref[ROT_HALF:ROT_DIM, :]

    def seg_t(a, b):
        return _dot_nt(wt_ref[a:b, :], h)

    def headnorm_t(t, row0, rot):
        outs = []
        for hh in range(t.shape[0] // HEAD_DIM):
            blk = t[hh * HEAD_DIM:(hh + 1) * HEAD_DIM]
            ms = jnp.mean(blk * blk, axis=0, keepdims=True)
            n = blk * lax.rsqrt(ms + EPS) * col_ref[row0 + hh * HEAD_DIM:row0 + (hh + 1) * HEAD_DIM, :]
            if rot:
                x1, x2 = n[0:ROT_HALF], n[ROT_HALF:ROT_DIM]
                n = jnp.concatenate([x1 * cos_t - x2 * sin_t, x2 * cos_t + x1 * sin_t, n[ROT_DIM:]], axis=0)
            outs.append(n)
        return jnp.concatenate(outs, axis=0)

    kc = headnorm_t(seg_t(T_KC, T_KC + KV_A), T_KC, True)
    vc = seg_t(T_VC, T_VC + KV_A)
    ks = headnorm_t(seg_t(T_KS, T_KS + KV_A), T_KS, True)
    vs = seg_t(T_VS, T_VS + KV_A)
    kw = headnorm_t(seg_t(T_KW, T_KW + KV_A), T_KW, True)
    vw = seg_t(T_VW, T_VW + KV_A)
    kb = headnorm_t(seg_t(T_KB, T_KB + WIDTH_B), T_KB, False)
    vb = seg_t(T_VB, T_VB + WIDTH_B)

    @pl.when(i % tiles_per_seq == 0)
    def _():
        carry_ref[...] = jnp.zeros_like(carry_ref)

    zf = seg_t(T_FL, T_FL + SUBLANES) + col_ref[T_FL:T_FL + SUBLANES, :]
    sub = lax.broadcasted_iota(jnp.int32, (SUBLANES, tm), 0)
    lane_t = lax.broadcasted_iota(jnp.int32, (SUBLANES, tm), 1)
    logf_t = jnp.where(sub < FOX_HEADS, _log_sigmoid(zf), 0.0)
    cum = logf_t
    shift = 1
    while shift < tm:
        cum = cum + jnp.where(lane_t >= shift, pltpu.roll(cum, shift, 1), 0.0)
        shift *= 2
    cum = cum + carry_ref[:, 0:1]
    carry_ref[...] = jnp.broadcast_to(cum[:, tm - 1:tm], carry_ref.shape)
    hi, mid, lo = [p.astype(F32) for p in _split3(-cum)]
    sub16 = lax.broadcasted_iota(jnp.int32, (FAUG_ROWS, tm), 0)

    @pl.when(i < n_prompt_tiles)
    def _():
        nsat_ref[0, 0:128, :] = kc
        nsat_ref[0, 128:256, :] = vc
        nsat_ref[0, 256:384, :] = ks
        nsat_ref[0, 384:512, :] = vs
        wint_ref[0, 0:128, :] = kw
        wint_ref[0, 128:256, :] = vw
        foxt_ref[0, 0:WIDTH_B, :] = kb
        foxt_ref[0, WIDTH_B:2 * WIDTH_B, :] = vb
        logft_ref[0] = logf_t
        for hh in range(FOX_HEADS):
            blk = jnp.where(sub16 == 0, hi[hh:hh + 1],
                            jnp.where(sub16 == 1, mid[hh:hh + 1], jnp.where(sub16 == 2, lo[hh:hh + 1], 0.0)))
            faug_ref[0, hh * FAUG_ROWS:(hh + 1) * FAUG_ROWS, :] = blk.astype(BF16)
        kcvc_ref[:, 0:128] = kc.T
        kcvc_ref[:, 128:256] = vc.T

    @pl.when(i >= n_prompt_tiles)
    def _():
        for j, t_ in enumerate((kc, vc, ks, vs)):
            nsas_ref[:, j * KV_A:(j + 1) * KV_A] = t_.T
        wins_ref[:, 0:128] = kw.T
        wins_ref[:, 128:256] = vw.T
        for c in range(WIDTH_B // LANES):
            foxs_ref[:, c * LANES:(c + 1) * LANES] = kb[c * LANES:(c + 1) * LANES].T
            foxs_ref[:, WIDTH_B + c * LANES:WIDTH_B + (c + 1) * LANES] = vb[c * LANES:(c + 1) * LANES].T
        vs_ref[...] = v_scr[...]


def _proj(x, gmix, w, wt, prm, col, rope, ropet, bd, wmix, btab, tm, batch, seq, n_sample):
    m, d = x.shape
    tps = seq // tm
    npt = batch * tps
    row = lambda width: pl.BlockSpec((tm, width), lambda i: (i, 0))
    full = lambda a: pl.BlockSpec(a.shape, lambda i: (0,) * a.ndim)
    kind = lambda i: (i >= npt).astype(jnp.int32)

    def featmajor(rows):
        def idx(i):
            ii = jnp.minimum(i, npt - 1)
            return (ii // tps, 0, ii % tps)
        return pl.BlockSpec((1, rows, tm), idx)

    prow = lambda width: pl.BlockSpec((tm, width), lambda i: (jnp.minimum(i, npt - 1), 0))
    srow = lambda width: pl.BlockSpec((tm, width), lambda i: (jnp.maximum(i - npt, 0), 0))
    n_prompt = batch * seq
    outs = [
        ("qa", row(2 * WIDTH_A), (m, 2 * WIDTH_A), BF16),
        ("qb", row(2 * WIDTH_B), (m, 2 * WIDTH_B), BF16),
        ("misc", row(LANES), (m, LANES), F32),
        ("oc", row(GM_WIDTH), (m, GM_WIDTH), BF16),
        ("kcvc", prow(2 * KV_A), (n_prompt, 2 * KV_A), F32),
        ("nsat", featmajor(4 * KV_A), (batch, 4 * KV_A, seq), F32),
        ("wint", featmajor(2 * KV_A), (batch, 2 * KV_A, seq), F32),
        ("foxt", featmajor(2 * WIDTH_B), (batch, 2 * WIDTH_B, seq), F32),
        ("faug", featmajor(FOX_HEADS * FAUG_ROWS), (batch, FOX_HEADS * FAUG_ROWS, seq), BF16),
        ("logft", featmajor(SUBLANES), (batch, SUBLANES, seq), F32),
        ("nsas", srow(4 * KV_A), (n_sample, 4 * KV_A), F32),
        ("wins", srow(2 * KV_A), (n_sample, 2 * KV_A), F32),
        ("foxs", srow(2 * WIDTH_B), (n_sample, 2 * WIDTH_B), F32),
        ("vs", srow(GM_WIDTH), (n_sample, GM_WIDTH), F32),
    ]
    res = pl.pallas_call(
        functools.partial(_proj_kernel, tm=tm, tiles_per_seq=tps, n_prompt_tiles=npt),
        grid=(m // tm,),
        in_specs=[row(d), full(gmix), full(w), full(wt), full(prm), full(col), row(3 * LANES),
                  pl.BlockSpec((ROT_DIM, tm), lambda i: (0, i)), full(bd),
                  pl.BlockSpec((1,) + wmix.shape[1:], lambda i: (kind(i), 0, 0, 0)),
                  pl.BlockSpec((1,) + btab.shape[1:], lambda i: (kind(i), 0, 0))],
        out_specs=[o[1] for o in outs],
        out_shape=[jax.ShapeDtypeStruct(o[2], o[3]) for o in outs],
        scratch_shapes=[pltpu.VMEM((tm, GM_WIDTH), F32), pltpu.VMEM((SUBLANES, LANES), F32)],
        compiler_params=_cparams(("arbitrary",)),
        name="proj",
    )(x, gmix, w, wt, prm, col, rope, ropet, bd, wmix, btab)
    return dict(zip([o[0] for o in outs], res))


def _mix_kernel(x_ref, gmix_ref, wmg_ref, oa_ref, ob_ref, oc_ref, wa_ref, wb_ref, wc_ref, wout_ref, o_ref):
    x = x_ref[...]
    d = x.shape[1]
    h = _rms_rows(x, gmix_ref[...]).astype(BF16)
    m = jax.nn.sigmoid(_dot(h, wmg_ref[:, 0:d])) * _dot(oa_ref[...].astype(BF16), wa_ref[...])
    m = m + jax.nn.sigmoid(_dot(h, wmg_ref[:, d:2 * d])) * _dot(ob_ref[...].astype(BF16), wb_ref[...])
    m = m + jax.nn.sigmoid(_dot(h, wmg_ref[:, 2 * d:3 * d])) * _dot(oc_ref[...], wc_ref[...])
    o_ref[...] = x + _dot(m.astype(BF16), wout_ref[...])


def _mix(x, gmix, wmg, oa, ob, oc, wa, wb, wc, wout, tm):
    m, d = x.shape
    row = lambda a: pl.BlockSpec((tm, a.shape[1]), lambda i: (i, 0))
    full = lambda a: pl.BlockSpec(a.shape, lambda i: (0,) * a.ndim)
    return pl.pallas_call(
        _mix_kernel,
        grid=(m // tm,),
        in_specs=[row(x), full(gmix), full(wmg), row(oa), row(ob), row(oc), full(wa), full(wb), full(wc),
                  full(wout)],
        out_specs=row(x),
        out_shape=jax.ShapeDtypeStruct((m, d), F32),
        compiler_params=_cparams(("parallel",)),
        name="mix",
    )(x, gmix, wmg, oa, ob, oc, wa, wb, wc, wout)


def _compress_kernel(x_ref, pe_ref, w_ref, o_ref):
    o_ref[0] = _dot((x_ref[0] + pe_ref[...]).astype(BF16), w_ref[...])


def _compress_prompt(x2, pe2, w2):
    b, nc, k = x2.shape
    n = w2.shape[1]
    return pl.pallas_call(
        _compress_kernel,
        grid=(b,),
        in_specs=[pl.BlockSpec((1, nc, k), lambda i: (i, 0, 0)),
                  pl.BlockSpec((1, k), lambda i: (0, 0)),
                  pl.BlockSpec((k, n), lambda i: (0, 0))],
        out_specs=pl.BlockSpec((1, nc, n), lambda i: (i, 0, 0)),
        out_shape=jax.ShapeDtypeStruct((b, nc, n), F32),
        compiler_params=_cparams(("parallel",)),
        name="compress_prompt",
    )(x2, pe2, w2)


def _online_update(s, vt_aug, acc_ref, m_ref):
    n_chunk = s.shape[1] // LANES
    m_old = m_ref[...]
    mx = s[:, 0:LANES]
    for c in range(1, n_chunk):
        mx = jnp.maximum(mx, s[:, c * LANES:(c + 1) * LANES])
    m_new = jnp.maximum(m_old, jnp.max(mx, axis=-1, keepdims=True))
    alpha = jnp.exp(m_old - m_new)
    p = jnp.concatenate(
        [jnp.exp(s[:, c * LANES:(c + 1) * LANES] - m_new).astype(BF16) for c in range(n_chunk)], axis=1)
    acc_ref[...] = alpha * acc_ref[...] + _dot_nt(p, vt_aug)
    m_ref[...] = m_new


def _finish(acc):
    return (acc / pltpu.roll(acc, HEAD_DIM, 1))[:, :HEAD_DIM]


def _select_blocks(imp_t, tok_row):
    nb, nt = imp_t.shape
    blk = lax.broadcasted_iota(jnp.int32, (nb, nt), 0)
    cur = jnp.broadcast_to(tok_row, (nb, nt)) // L_SEL
    forced = (blk == 0) | (blk == cur) | (blk == cur - 1)
    visible = blk <= cur
    score = jnp.where(visible & jnp.logical_not(forced), imp_t, -jnp.inf)
    sel = jnp.where(visible & forced, 1.0, 0.0)
    for _ in range(min(TOP_N - N_FORCED, nb)):
        mx = jnp.max(score, axis=0, keepdims=True)
        first = jnp.min(jnp.where(score == mx, blk, nb), axis=0, keepdims=True)
        hit = (blk == first) & (mx > -jnp.inf)
        sel = jnp.where(hit, 1.0, sel)
        score = jnp.where(hit, -jnp.inf, score)
    return sel


def _nsa_prompt_kernel(q_ref, misc_ref, kcv_ref, kv_ref, win_ref, o_ref, acc_ref, m_ref, *, n_cmp):
    i = pl.program_id(1)
    rows = NSA_HPG * Q_BLOCK
    tok0 = i * Q_BLOCK
    tok = tok0 + (lax.broadcasted_iota(jnp.int32, (rows, 1), 0) % Q_BLOCK)
    half = n_cmp // 2
    ones_rows = jnp.ones((HEAD_DIM, 1), BF16)
    zero_rows = jnp.zeros((HEAD_DIM, 1), BF16)

    q_los, q_augs, o_cs = [], [], []
    for g in range(NSA_GROUPS):
        gl = slice(g * LANES, (g + 1) * LANES)
        q_lo = jnp.concatenate(
            [q_ref[:, (g * NSA_HPG + h) * LANES:(g * NSA_HPG + h + 1) * LANES] for h in range(NSA_HPG)], axis=0)
        kcv = kcv_ref[0, :, gl].astype(BF16)
        s_c = _dot_nt(q_lo, kcv)
        col = lax.broadcasted_iota(jnp.int32, (1, n_cmp), 1)
        blk_c = 2 * (col % half) + col // half
        s_c = jnp.where((blk_c + 1) * L_CMP - 1 <= tok, s_c, NEG_BIG)
        e_c = jnp.exp(s_c - jnp.max(s_c, axis=-1, keepdims=True))
        p_c = e_c / jnp.sum(e_c, axis=-1, keepdims=True)
        p_c = jnp.where(tok >= L_CMP - 1, p_c, 0.0)
        o_cs.append(_dot(p_c.astype(BF16), kcv)[:, HEAD_DIM:])
        imp = p_c[0:Q_BLOCK]
        for h in range(1, NSA_HPG):
            imp = imp + p_c[h * Q_BLOCK:(h + 1) * Q_BLOCK]
        imp_sel = imp[:, :half] + imp[:, half:]
        if half < LANES:
            imp_sel = jnp.concatenate([imp_sel, jnp.zeros((Q_BLOCK, LANES - half), F32)], axis=1)
        sel_t = _select_blocks(imp_sel.T, tok0 + lax.broadcasted_iota(jnp.int32, (1, Q_BLOCK), 1))
        nsel = (1.0 - sel_t).T.astype(BF16)
        q_los.append(q_lo)
        q_augs.append(jnp.concatenate([q_lo, jnp.concatenate([nsel] * NSA_HPG, axis=0)], axis=1))

    acc_ref[...] = jnp.zeros_like(acc_ref)
    m_ref[...] = jnp.full_like(m_ref, -jnp.inf)
    n_kt = (tok0 + Q_BLOCK + KEY_TILE - 1) // KEY_TILE

    def sel_tile(kt, causal):
        ks = pl.ds(pl.multiple_of(kt * KEY_TILE, KEY_TILE), KEY_TILE)
        crow = lax.broadcasted_iota(jnp.int32, (LANES, KEY_TILE), 0)
        kblk = (kt * KEY_TILE + lax.broadcasted_iota(jnp.int32, (LANES, KEY_TILE), 1)) // L_SEL
        aug = jnp.where(crow == kblk, -MASK_BIG, 0.0).astype(BF16)
        for g in range(NSA_GROUPS):
            k_t = kv_ref[0, g * HEAD_DIM:(g + 1) * HEAD_DIM, ks].astype(BF16)
            v_t = kv_ref[0, KV_A + g * HEAD_DIM:KV_A + (g + 1) * HEAD_DIM, ks].astype(BF16)
            zeros = jnp.broadcast_to(zero_rows, (HEAD_DIM, KEY_TILE))
            s = _dot(q_augs[g], jnp.concatenate([k_t, zeros, aug], axis=0))
            if causal:
                keypos = kt * KEY_TILE + lax.broadcasted_iota(jnp.int32, (1, KEY_TILE), 1)
                s = jnp.where(keypos <= tok, s, -MASK_BIG)
            ones = jnp.broadcast_to(ones_rows, (HEAD_DIM, KEY_TILE))
            _online_update(s, jnp.concatenate([v_t, ones], axis=0), acc_ref.at[g], m_ref.at[g])

    def sel_body(j, carry):
        sel_tile(2 * j, False)
        sel_tile(2 * j + 1, False)
        return carry

    n_pairs = (n_kt - 1) // 2
    lax.fori_loop(0, n_pairs, sel_body, 0)

    @pl.when(2 * n_pairs < n_kt - 1)
    def _():
        sel_tile(n_kt - 2, False)

    sel_tile(n_kt - 1, True)

    band = WINDOW + Q_BLOCK
    start = pl.multiple_of(jnp.maximum(tok0 - WINDOW, 0), Q_BLOCK)
    wk = pl.ds(start, band)
    dpos = tok - (start + lax.broadcasted_iota(jnp.int32, (1, band), 1))
    ok = (dpos >= 0) & (dpos < WINDOW)
    gates = misc_ref[...]
    for g in range(NSA_GROUPS):
        k_t = win_ref[0, g * HEAD_DIM:(g + 1) * HEAD_DIM, wk].astype(BF16)
        v_t = win_ref[0, KV_A + g * HEAD_DIM:KV_A + (g + 1) * HEAD_DIM, wk].astype(BF16)
        zeros = jnp.broadcast_to(zero_rows, (HEAD_DIM, band))
        s = jnp.where(ok, _dot(q_los[g], jnp.concatenate([k_t, zeros], axis=0)), NEG_BIG)
        p = jnp.exp(s - jnp.max(s, axis=-1, keepdims=True)).astype(BF16)
        ones = jnp.broadcast_to(ones_rows, (HEAD_DIM, band))
        o_w = _finish(_dot_nt(p, jnp.concatenate([v_t, ones], axis=0)))
        o_s = _finish(acc_ref[g])
        o_c = o_cs[g]
        pieces = []
        for h in range(NSA_HPG):
            hr = slice(h * Q_BLOCK, (h + 1) * Q_BLOCK)
            c0 = GATE_LANE0 + (g * NSA_HPG + h) * 3
            pieces.append(gates[:, c0:c0 + 1] * o_c[hr] + gates[:, c0 + 1:c0 + 2] * o_s[hr]
                          + gates[:, c0 + 2:c0 + 3] * o_w[hr])
        o_ref[:, g * NSA_HPG * HEAD_DIM:(g + 1) * NSA_HPG * HEAD_DIM] = jnp.concatenate(pieces, axis=1)


def _nsa_prompt(qa, misc, kcv, nsat, wint, batch, seq):
    m = qa.shape[0]
    nq = seq // Q_BLOCK
    n_cmp = kcv.shape[1]
    rows = NSA_HPG * Q_BLOCK
    return pl.pallas_call(
        functools.partial(_nsa_prompt_kernel, n_cmp=n_cmp),
        grid=(batch, nq),
        in_specs=[pl.BlockSpec((Q_BLOCK, 2 * WIDTH_A), lambda b, i: (b * nq + i, 0)),
                  pl.BlockSpec((Q_BLOCK, LANES), lambda b, i: (b * nq + i, 0)),
                  pl.BlockSpec((1, n_cmp, 2 * KV_A), lambda b, i: (b, 0, 0)),
                  pl.BlockSpec((1, 2 * KV_A, seq), lambda b, i: (b, 1, 0)),
                  pl.BlockSpec((1, 2 * KV_A, seq), lambda b, i: (b, 0, 0))],
        out_specs=pl.BlockSpec((Q_BLOCK, WIDTH_A), lambda b, i: (b * nq + i, 0)),
        out_shape=jax.ShapeDtypeStruct((m, WIDTH_A), F32),
        scratch_shapes=[pltpu.VMEM((NSA_GROUPS, rows, LANES), F32), pltpu.VMEM((NSA_GROUPS, rows, LANES), F32)],
        compiler_params=_cparams(("parallel", "arbitrary")),
        name="nsa_prompt",
    )(qa, misc, kcv, nsat, wint)


def _fox_prompt_kernel(q_ref, kv_ref, faug_ref, o_ref, acc_ref, m_ref, *, qb):
    i = pl.program_id(1)
    tok = i * qb + lax.broadcasted_iota(jnp.int32, (qb, 1), 0)
    lane = lax.broadcasted_iota(jnp.int32, (qb, LANES), 1)
    ones_lanes = jnp.where(lane < HEAD_DIM + 3, 1.0, 0.0).astype(BF16)
    q_augs = [jnp.where(lane < HEAD_DIM, q_ref[:, h * LANES:(h + 1) * LANES], ones_lanes) for h in range(FOX_HEADS)]
    pad_rows = LANES - HEAD_DIM - FAUG_ROWS
    acc_ref[...] = jnp.zeros_like(acc_ref)
    m_ref[...] = jnp.full_like(m_ref, -jnp.inf)

    def tile(kt, causal):
        ks = pl.ds(pl.multiple_of(kt * KEY_TILE, KEY_TILE), KEY_TILE)
        zeros = jnp.zeros((pad_rows, KEY_TILE), BF16)
        ones = jnp.ones((HEAD_DIM, KEY_TILE), BF16)
        for h in range(FOX_HEADS):
            k_t = kv_ref[0, h * HEAD_DIM:(h + 1) * HEAD_DIM, ks].astype(BF16)
            v_t = kv_ref[0, WIDTH_B + h * HEAD_DIM:WIDTH_B + (h + 1) * HEAD_DIM, ks].astype(BF16)
            fa = faug_ref[0, h * FAUG_ROWS:(h + 1) * FAUG_ROWS, ks]
            s = _dot(q_augs[h], jnp.concatenate([k_t, fa, zeros], axis=0))
            if causal:
                keypos = kt * KEY_TILE + lax.broadcasted_iota(jnp.int32, (1, KEY_TILE), 1)
                s = jnp.where(keypos <= tok, s, -MASK_BIG)
            _online_update(s, jnp.concatenate([v_t, ones], axis=0), acc_ref.at[h], m_ref.at[h])

    def body(kt, carry):
        tile(kt, False)
        return carry

    n_full = i * (qb // KEY_TILE)
    lax.fori_loop(0, n_full, body, 0)
    for d in range(qb // KEY_TILE):
        tile(n_full + d, True)
    for h in range(FOX_HEADS):
        o_ref[:, h * HEAD_DIM:(h + 1) * HEAD_DIM] = _finish(acc_ref[h])


def _fox_prompt(qb_arr, foxt, faug, batch, seq):
    m = qb_arr.shape[0]
    qb = min(FOX_Q_BLOCK, seq)
    nq = seq // qb
    return pl.pallas_call(
        functools.partial(_fox_prompt_kernel, qb=qb),
        grid=(batch, nq),
        in_specs=[pl.BlockSpec((qb, 2 * WIDTH_B), lambda b, i: (b * nq + i, 0)),
                  pl.BlockSpec((1, 2 * WIDTH_B, seq), lambda b, i: (b, 0, 0)),
                  pl.BlockSpec((1, FOX_HEADS * FAUG_ROWS, seq), lambda b, i: (b, 0, 0))],
        out_specs=pl.BlockSpec((qb, WIDTH_B), lambda b, i: (b * nq + i, 0)),
        out_shape=jax.ShapeDtypeStruct((m, WIDTH_B), F32),
        scratch_shapes=[pltpu.VMEM((FOX_HEADS, qb, LANES), F32), pltpu.VMEM((FOX_HEADS, qb, LANES), F32)],
        compiler_params=_cparams(("parallel", "arbitrary")),
        name="fox_prompt",
    )(qb_arr, foxt, faug)


def _cmp_pool_kernel(x_ref, pe_ref, w_ref, o_ref, rows_ref, *, nb):
    per = PAGE // L_CMP

    def put(p, carry):
        base = pl.multiple_of(p * (per * CMP_PITCH), SUBLANES)
        for c in range(2):
            rows = x_ref[0, p, c * KV_A:(c + 1) * KV_A, :].T
            for n in range(per):
                rows_ref[c, pl.ds(base + n * CMP_PITCH, L_CMP), :] = rows[n * L_CMP:(n + 1) * L_CMP]
        return carry

    lax.fori_loop(0, nb, put, 0, unroll=2)
    n_blk = nb * per
    for c in range(2):
        cs = slice(c * KV_A, (c + 1) * KV_A)
        acc = jnp.zeros((n_blk, KV_A), F32)
        for l in range(L_CMP):
            xl = rows_ref[c, pl.ds(l, n_blk, stride=CMP_PITCH), :]
            acc = acc + _dot((xl + pe_ref[l:l + 1, cs]).astype(BF16), w_ref[l, cs, cs])
        o_ref[:, cs] = acc


def _compress_pool(cache_t, layer, pe_l, w_l, nb):
    n_pool = cache_t.shape[1]
    per = PAGE // L_CMP
    return pl.pallas_call(
        functools.partial(_cmp_pool_kernel, nb=nb),
        grid=(n_pool // nb,),
        in_specs=[pl.BlockSpec((1, nb, 2 * KV_A, PAGE), lambda i: (layer, i, 0, 0)),
                  pl.BlockSpec(pe_l.shape, lambda i: (0, 0)),
                  pl.BlockSpec(w_l.shape, lambda i: (0, 0, 0))],
        out_specs=pl.BlockSpec((nb * per, 2 * KV_A), lambda i: (i, 0)),
        out_shape=jax.ShapeDtypeStruct((n_pool * per, 2 * KV_A), F32),
        scratch_shapes=[pltpu.VMEM((2, nb * per * CMP_PITCH, KV_A), F32)],
        compiler_params=_cparams(("parallel",)),
        name="compress_pool",
    )(cache_t, pe_l, w_l)


def _softmax_pieces(pieces):
    mx = functools.reduce(jnp.maximum, [jnp.max(s, axis=-1, keepdims=True) for s in pieces])
    es = [jnp.exp(s - mx) for s in pieces]
    den = functools.reduce(lambda a, b: a + b, [jnp.sum(e, axis=-1, keepdims=True) for e in es])
    return es, den


def _nsa_sample_kernel(pt_ref, q_ref, misc_ref, new_ref, wnew_ref, state_ref, pair_ref, *rest, n_pages, past):
    del pt_ref
    pages = rest[:n_pages]
    cmps = rest[n_pages:2 * n_pages]
    o_ref, wout_ref = rest[2 * n_pages:]
    tn = q_ref.shape[0]
    rows = NSA_HEADS * tn
    rid = lax.broadcasted_iota(jnp.int32, (rows, 1), 0)
    qi = rid % tn
    tok = past + qi
    is_g1 = rid >= NSA_HPG * tn

    q = q_ref[...]
    blocks = []
    for g in range(NSA_GROUPS):
        for h in range(NSA_HPG):
            c = q[:, (g * NSA_HPG + h) * LANES:(g * NSA_HPG + h + 1) * LANES]
            blocks.append(pltpu.roll(c, HEAD_DIM, 1) if g == 1 else c)
    q64 = jnp.concatenate(blocks, axis=0).astype(BF16)

    def pick(x):
        return jnp.where(is_g1, x[:, HEAD_DIM:2 * HEAD_DIM], x[:, 0:HEAD_DIM])

    def pad_keys(x):
        return jnp.concatenate([x, jnp.zeros((2 * tn - x.shape[0], x.shape[1]), x.dtype)], axis=0)

    n_cmp = n_pages * (PAGE // L_CMP)
    kcv = jnp.concatenate([c[0] for c in cmps], axis=0).astype(BF16)
    s_c = _dot_nt(q64, kcv[:, 0:KV_A])
    blk_c = lax.broadcasted_iota(jnp.int32, (1, n_cmp), 1)
    s_c = jnp.where((blk_c + 1) * L_CMP - 1 <= tok, s_c, NEG_BIG)
    e_c = jnp.exp(s_c - jnp.max(s_c, axis=-1, keepdims=True))
    p_c = e_c / jnp.sum(e_c, axis=-1, keepdims=True)
    p_c = jnp.where(tok >= L_CMP - 1, p_c, 0.0)
    o_c = pick(_dot(p_c.astype(BF16), kcv[:, KV_A:2 * KV_A]))

    imps = []
    for g in range(NSA_GROUPS):
        imp = p_c[g * NSA_HPG * tn:g * NSA_HPG * tn + tn]
        for h in range(1, NSA_HPG):
            imp = imp + p_c[(g * NSA_HPG + h) * tn:(g * NSA_HPG + h + 1) * tn]
        imps.append(imp)
    imp2 = jnp.concatenate(imps + [jnp.zeros((LANES - NSA_GROUPS * tn, n_cmp), F32)], axis=0)
    imp_sel = jnp.zeros((LANES, LANES), F32)
    for part in _split3(imp2):
        imp_sel = imp_sel + _dot(part, pair_ref[...])
    tok_row = past + lax.broadcasted_iota(jnp.int32, (1, LANES), 1) % tn
    sel = _select_blocks(imp_sel.T, tok_row).T
    bias2 = jnp.where(sel > 0.0, 0.0, -MASK_BIG)
    bias_sel = jnp.concatenate([bias2[0:tn]] * NSA_HPG + [bias2[tn:2 * tn]] * NSA_HPG, axis=0)

    lane_lo = lax.broadcasted_iota(jnp.int32, (rows, PAGE), 1) < L_SEL
    new_kv = pad_keys(new_ref[:, 2 * KV_A:4 * KV_A]).astype(BF16)
    scores = []
    for j in range(n_pages):
        bias = jnp.where(lane_lo, bias_sel[:, 2 * j:2 * j + 1], bias_sel[:, 2 * j + 1:2 * j + 2])
        scores.append(_dot(q64, pages[j][0, 0, 0:KV_A, :].astype(BF16)) + bias)
    kcol = lax.broadcasted_iota(jnp.int32, (1, 2 * tn), 1)
    s_new = _dot_nt(q64, new_kv[:, 0:KV_A]) + bias_sel[:, 2 * n_pages:2 * n_pages + 1]
    scores.append(jnp.where((kcol <= qi) & (kcol < tn), s_new, -MASK_BIG))
    probs, den = _softmax_pieces(scores)
    acc = _dot(probs[n_pages].astype(BF16), new_kv[:, KV_A:2 * KV_A])
    for j in range(n_pages):
        acc = acc + _dot_nt(probs[j].astype(BF16), pages[j][0, 0, KV_A:2 * KV_A, :].astype(BF16))
    o_s = pick(acc) / den

    wb = state_ref.shape[3]
    st = state_ref[0, 0]
    wnew = wnew_ref[...]
    wn_b = pad_keys(wnew).astype(BF16)
    kpos = past - wb + lax.broadcasted_iota(jnp.int32, (1, wb), 1)
    dpos = tok - kpos
    s1 = jnp.where((dpos >= 0) & (dpos < WINDOW) & (kpos >= 0), _dot(q64, st[0:KV_A].astype(BF16)), NEG_BIG)
    kcol = lax.broadcasted_iota(jnp.int32, (1, 2 * tn), 1)
    s2 = jnp.where((kcol <= qi) & (kcol < tn), _dot_nt(q64, wn_b[:, 0:KV_A]), NEG_BIG)
    (p1, p2), den_w = _softmax_pieces([s1, s2])
    o_w = pick(_dot_nt(p1.astype(BF16), st[KV_A:2 * KV_A].astype(BF16))
               + _dot(p2.astype(BF16), wn_b[:, KV_A:2 * KV_A])) / den_w

    new_t = jnp.concatenate([jnp.zeros((LANES - tn, 2 * KV_A), F32), wnew], axis=0).T
    shifted = pltpu.roll(st, wb - tn, 1)
    wout_ref[0, 0, :, 0:wb - LANES] = shifted[:, 0:wb - LANES]
    lane_w = lax.broadcasted_iota(jnp.int32, (2 * KV_A, LANES), 1)
    wout_ref[0, 0, :, wb - LANES:wb] = jnp.where(lane_w >= LANES - tn, new_t, shifted[:, wb - LANES:wb])

    gates = misc_ref[...]
    pieces = []
    for hh in range(NSA_HEADS):
        hr = slice(hh * tn, (hh + 1) * tn)
        c0 = GATE_LANE0 + hh * 3
        pieces.append(gates[:, c0:c0 + 1] * o_c[hr] + gates[:, c0 + 1:c0 + 2] * o_s[hr]
                      + gates[:, c0 + 2:c0 + 3] * o_w[hr])
    o_ref[...] = jnp.concatenate(pieces, axis=1)


def _nsa_sample(page_table, layer, qa_s, misc_s, nsa_s, win_s, state_t, pair, cache_t, cmp_pool3, oa_full, n_prompt):
    nseq, n_pages = page_table.shape
    tn = qa_s.shape[0] // nseq
    past = n_pages * PAGE
    wb = state_t.shape[3]
    ns = SEQ_PER_STEP
    assert nseq % ns == 0 and n_prompt % (ns * tn) == 0
    row = lambda width: pl.BlockSpec((ns * tn, width), lambda b, pt: (b, 0))
    page_specs = [pl.BlockSpec((1, 1, 2 * KV_A, PAGE),
                               functools.partial(lambda b, pt, s, j: (layer, pt[ns * b + s, j], 1, 0), s=s, j=j))
                  for s in range(ns) for j in range(n_pages)]
    cmp_specs = [pl.BlockSpec((1, PAGE // L_CMP, 2 * KV_A),
                              functools.partial(lambda b, pt, s, j: (pt[ns * b + s, j], 0, 0), s=s, j=j))
                 for s in range(ns) for j in range(n_pages)]
    grid_spec = pltpu.PrefetchScalarGridSpec(
        num_scalar_prefetch=1,
        grid=(nseq // ns,),
        in_specs=[row(2 * WIDTH_A), row(LANES), row(4 * KV_A), row(2 * KV_A),
                  pl.BlockSpec((1, ns, 2 * KV_A, wb), lambda b, pt: (layer, b, 0, 0)),
                  pl.BlockSpec(pair.shape, lambda b, pt: (0, 0))] + page_specs + cmp_specs
                 + [pl.BlockSpec(memory_space=pl.ANY)],
        out_specs=[pl.BlockSpec((ns * tn, WIDTH_A), lambda b, pt: (n_prompt // (ns * tn) + b, 0)),
                   pl.BlockSpec((1, ns, 2 * KV_A, wb), lambda b, pt: (0, b, 0, 0))],
    )
    n_paged = ns * n_pages

    def body(pt_ref, q_ref, misc_ref, new_ref, wnew_ref, state_ref, pair_ref, *rest):
        pages, cmps = rest[:n_paged], rest[n_paged:2 * n_paged]
        o_ref, wout_ref = rest[2 * n_paged + 1:]
        for s in range(ns):
            rs, one = pl.ds(s * tn, tn), pl.ds(s, 1)
            _nsa_sample_kernel(pt_ref, q_ref.at[rs], misc_ref.at[rs], new_ref.at[rs], wnew_ref.at[rs],
                               state_ref.at[:, one], pair_ref, *pages[s * n_pages:(s + 1) * n_pages],
                               *cmps[s * n_pages:(s + 1) * n_pages], o_ref.at[rs], wout_ref.at[:, one],
                               n_pages=n_pages, past=past)

    n_in = 1 + 6 + 2 * n_paged
    return pl.pallas_call(
        body,
        grid_spec=grid_spec,
        out_shape=[jax.ShapeDtypeStruct(oa_full.shape, F32), jax.ShapeDtypeStruct((1, nseq, 2 * KV_A, wb), F32)],
        input_output_aliases={n_in: 0},
        compiler_params=_cparams(("arbitrary",)),
        name="nsa_sample",
    )(page_table, qa_s, misc_s, nsa_s, win_s, state_t, pair, *([cache_t] * n_paged), *([cmp_pool3] * n_paged), oa_full)


def _fox_sample_kernel(pt_ref, q_ref, new_ref, lfn_ref, ustrict_ref, *rest, n_pages):
    del pt_ref
    pages = rest[:n_pages]
    lfps = rest[n_pages:2 * n_pages]
    o_ref = rest[2 * n_pages]
    tn = q_ref.shape[0]
    rows = FOX_HEADS * tn
    qi = lax.broadcasted_iota(jnp.int32, (rows, 1), 0) % tn

    q = q_ref[...]
    zeros_half = jnp.zeros((tn, LANES), F32)
    blocks = []
    for h in range(FOX_HEADS):
        c = q[:, h * LANES:(h + 1) * LANES]
        if h % 2 == 1:
            c = pltpu.roll(c, HEAD_DIM, 1)
        blocks.append(jnp.concatenate([c, zeros_half] if h < 2 else [zeros_half, c], axis=1))
    q32 = jnp.concatenate(blocks, axis=0).astype(BF16)

    lf = jnp.concatenate([r[0, 0] for r in lfps], axis=0)
    within = jnp.zeros(lf.shape, F32)
    for part in _split3(lf):
        within = within + _dot(part, ustrict_ref[...])
    tot = jnp.sum(lf, axis=-1, keepdims=True)
    after = jnp.zeros((FOX_HEADS, 1), F32)
    page_bias = [None] * n_pages
    for j in reversed(range(n_pages)):
        b4 = within[j * FOX_HEADS:(j + 1) * FOX_HEADS] + after
        page_bias[j] = jnp.concatenate(
            [jnp.broadcast_to(b4[h:h + 1], (tn, PAGE)) for h in range(FOX_HEADS)], axis=0)
        after = after + tot[j * FOX_HEADS:(j + 1) * FOX_HEADS]

    lfn = lfn_ref[0]
    run = lfn[:, 0:1]
    cols = [run]
    for r in range(1, tn):
        run = run + lfn[:, r:r + 1]
        cols.append(run)
    cum = jnp.concatenate(cols + [jnp.zeros((FOX_HEADS, tn), F32)], axis=1)
    new_bias = jnp.concatenate(
        [jnp.broadcast_to(-cum[h:h + 1], (tn, 2 * tn)) for h in range(FOX_HEADS)], axis=0)

    new = jnp.concatenate([new_ref[...], jnp.zeros((tn, 2 * WIDTH_B), F32)], axis=0).astype(BF16)
    scores = [_dot(q32, pages[j][0, 0, 0:WIDTH_B, :].astype(BF16)) + page_bias[j] for j in range(n_pages)]
    kcol = lax.broadcasted_iota(jnp.int32, (1, 2 * tn), 1)
    scores.append(jnp.where((kcol <= qi) & (kcol < tn), _dot_nt(q32, new[:, :WIDTH_B]) + new_bias, -MASK_BIG))
    probs, den = _softmax_pieces(scores)
    acc = _dot(probs[n_pages].astype(BF16), new[:, WIDTH_B:])
    for j in range(n_pages):
        acc = acc + _dot_nt(probs[j].astype(BF16), pages[j][0, 0, WIDTH_B:2 * WIDTH_B, :].astype(BF16))
    o32 = acc / den
    lane_head = lax.broadcasted_iota(jnp.int32, (tn, WIDTH_B), 1) // HEAD_DIM
    out = jnp.zeros((tn, WIDTH_B), F32)
    for h in range(FOX_HEADS):
        out = out + jnp.where(lane_head == h, o32[h * tn:(h + 1) * tn], 0.0)
    o_ref[...] = out


def _fox_sample(page_table, layer, qb_s, fox_s, lfn_t, ustrict, cache_t, logf_t, ob_full, n_prompt):
    nseq, n_pages = page_table.shape
    tn = qb_s.shape[0] // nseq
    ns = SEQ_PER_STEP
    assert nseq % ns == 0 and n_prompt % (ns * tn) == 0
    row = lambda width: pl.BlockSpec((ns * tn, width), lambda b, pt: (b, 0))
    page_specs = [pl.BlockSpec((1, 1, 2 * WIDTH_B, PAGE),
                               functools.partial(lambda b, pt, s, j: (layer, pt[ns * b + s, j], 0, 0), s=s, j=j))
                  for s in range(ns) for j in range(n_pages)]
    lf_specs = [pl.BlockSpec((1, 1, FOX_HEADS, PAGE),
                             functools.partial(lambda b, pt, s, j: (layer, pt[ns * b + s, j], 0, 0), s=s, j=j))
                for s in range(ns) for j in range(n_pages)]
    grid_spec = pltpu.PrefetchScalarGridSpec(
        num_scalar_prefetch=1,
        grid=(nseq // ns,),
        in_specs=[row(2 * WIDTH_B), row(2 * WIDTH_B),
                  pl.BlockSpec((ns, FOX_HEADS, tn), lambda b, pt: (b, 0, 0)),
                  pl.BlockSpec(ustrict.shape, lambda b, pt: (0, 0))] + page_specs + lf_specs
                 + [pl.BlockSpec(memory_space=pl.ANY)],
        out_specs=pl.BlockSpec((ns * tn, WIDTH_B), lambda b, pt: (n_prompt // (ns * tn) + b, 0)),
    )
    n_paged = ns * n_pages

    def body(pt_ref, q_ref, new_ref, lfn_ref, ustrict_ref, *rest):
        pages, lfps = rest[:n_paged], rest[n_paged:2 * n_paged]
        o_ref = rest[2 * n_paged + 1]
        for s in range(ns):
            rs = pl.ds(s * tn, tn)
            _fox_sample_kernel(pt_ref, q_ref.at[rs], new_ref.at[rs], lfn_ref.at[pl.ds(s, 1)], ustrict_ref,
                               *pages[s * n_pages:(s + 1) * n_pages], *lfps[s * n_pages:(s + 1) * n_pages],
                               o_ref.at[rs], n_pages=n_pages)

    n_in = 1 + 4 + 2 * n_paged
    return pl.pallas_call(
        body,
        grid_spec=grid_spec,
        out_shape=jax.ShapeDtypeStruct(ob_full.shape, F32),
        input_output_aliases={n_in: 0},
        compiler_params=_cparams(("arbitrary",)),
        name="fox_sample",
    )(page_table, qb_s, fox_s, lfn_t, ustrict, *([cache_t] * n_paged), *([logf_t] * n_paged), ob_full)


def _in_offsets():
    splits = (WIDTH_A, KV_A, KV_A, KV_A, KV_A, KV_A, KV_A, 3 * NSA_HEADS, WIDTH_B, WIDTH_B, WIDTH_B, FOX_HEADS,
              GM_WIDTH, GM_WIDTH)
    offs = np.concatenate([[0], np.cumsum(splits)])
    names = ("qa", "kc", "vc", "ks", "vs", "kw", "vw", "ga", "qb", "kb", "vb", "fl", "gu", "gv", "mg")
    return {n: int(o) for n, o in zip(names, offs)}


def _pack_params(w_in, qk_gain, gm_norm, b_forget, d_model):
    depth = w_in.shape[0]
    o = _in_offsets()
    cols = np.concatenate([
        np.arange(o["qa"], o["qa"] + WIDTH_A),
        np.arange(o["qb"], o["qb"] + WIDTH_B),
        np.arange(o["gu"], o["gu"] + 2 * GM_WIDTH),
        np.arange(o["fl"], o["fl"] + FOX_HEADS),
        np.arange(o["ga"], o["ga"] + 3 * NSA_HEADS)])
    w_p = jnp.take(w_in, jnp.asarray(cols), axis=2)
    w_p = jnp.pad(w_p, ((0, 0), (0, 0), (0, P_COLS - w_p.shape[2]))).astype(BF16)
    rows_t = np.concatenate([
        np.arange(o["kc"], o["kc"] + 6 * KV_A),
        np.arange(o["kb"], o["kb"] + 2 * WIDTH_B),
        np.arange(o["fl"], o["fl"] + FOX_HEADS)])
    w_t = jnp.swapaxes(jnp.take(w_in, jnp.asarray(rows_t), axis=2), 1, 2)
    w_t = jnp.pad(w_t, ((0, 0), (0, T_ROWS - w_t.shape[1]), (0, 0))).astype(BF16)
    w_mg = w_in[:, :, o["mg"]:o["mg"] + 3 * d_model].astype(BF16)
    tile = lambda g, n: jnp.tile(g, (1, n))
    prm = jnp.concatenate([
        tile(qk_gain[:, 0], NSA_HEADS), tile(qk_gain[:, 4], FOX_HEADS), gm_norm,
        b_forget, jnp.zeros((depth, LANES - FOX_HEADS), F32)], axis=1)
    assert prm.shape[1] == R_COLS
    ones = jnp.ones((depth, KV_A), F32)
    col = jnp.concatenate([
        tile(qk_gain[:, 1], NSA_GROUPS), ones, tile(qk_gain[:, 2], NSA_GROUPS), ones,
        tile(qk_gain[:, 3], NSA_GROUPS), ones, tile(qk_gain[:, 5], FOX_HEADS), jnp.ones((depth, WIDTH_B), F32),
        b_forget, jnp.zeros((depth, SUBLANES - FOX_HEADS), F32)], axis=1)
    assert col.shape[1] == T_ROWS
    return w_p, w_t, w_mg, prm[:, None, :], col[:, :, None]


def _rope_tables(pos):
    half = ROT_HALF
    inv = ROPE_THETA ** (-jnp.arange(half, dtype=F32) / half)
    ang = pos.astype(F32)[:, None] * inv[None, :]
    cos, sin = jnp.cos(ang), jnp.sin(ang)
    n = pos.shape[0]
    zero = jnp.zeros((n, HEAD_DIM - ROT_DIM), F32)
    zero8 = jnp.zeros((n, half), F32)
    c64 = jnp.concatenate([cos, cos, zero + 1.0], axis=1)
    lo64 = jnp.concatenate([-sin, zero8, zero], axis=1)
    hi64 = jnp.concatenate([zero8, sin, zero], axis=1)
    rep = lambda t: jnp.tile(t, (1, LANES // HEAD_DIM))
    token_major = jnp.concatenate([rep(c64), rep(lo64), rep(hi64)], axis=1)
    feature_major = jnp.concatenate([cos.T, sin.T], axis=0)
    return token_major, feature_major


def _gmlp_tables(tn, w_spatial, b_spatial):
    w_p = jnp.tril(w_spatial)
    eye = jnp.asarray(np.eye(CHUNK // tn), F32)
    w_s = jnp.einsum("ab,lgts->lgatbs", eye, jnp.tril(w_spatial[:, :, :tn, :tn]))
    w_s = w_s.reshape(w_spatial.shape[0], GM_GROUPS, CHUNK, CHUNK)
    wmix = jnp.stack([w_p, w_s], axis=1).astype(BF16)
    b_p = jnp.repeat(jnp.swapaxes(b_spatial, 1, 2), GM_DIM, axis=2)
    b_s = jnp.tile(b_p[:, :tn], (1, CHUNK // tn, 1))
    btab = jnp.stack([b_p, b_s], axis=1)
    return wmix, btab


def _compress_weights(w_cmp, pe_cmp):
    depth = w_cmp.shape[0]
    eye_g = jnp.asarray(np.eye(NSA_GROUPS), F32)
    w_l = jnp.einsum("kq,gh,zklde->zlkgdqhe", jnp.asarray(np.eye(2), F32), eye_g, w_cmp)
    w_l = w_l.reshape(depth, L_CMP, 2 * KV_A, 2 * KV_A)
    pe_l = jnp.broadcast_to(pe_cmp[:, :, :, None, :], (depth, 2, L_CMP, NSA_GROUPS, HEAD_DIM))
    pe_l = jnp.transpose(pe_l, (0, 2, 1, 3, 4)).reshape(depth, L_CMP, 2 * KV_A)
    return w_l.astype(BF16), pe_l


def _group_major(x):
    s = x.shape[:-1]
    return jnp.swapaxes(x.reshape(s + (2, NSA_GROUPS, HEAD_DIM)), -3, -2).reshape(s + (2 * KV_A,))


def _feature_major(x, n_feat):
    nd = x.ndim
    perm = tuple(range(nd - 4)) + (nd - 3, nd - 2, nd - 1, nd - 4)
    y = jnp.transpose(x, perm)
    return y.reshape(y.shape[:nd - 4] + (n_feat, y.shape[-1]))


def _token_major_view(y, dims):
    lead = y.shape[:-2]
    z = y.reshape(lead + tuple(dims) + (y.shape[-1],))
    nd = z.ndim
    perm = tuple(range(nd - 4)) + (nd - 1, nd - 4, nd - 3, nd - 2)
    return jnp.transpose(z, perm)


def kernel(x_prompt, x_sample, cache_nsa_kv, cache_fox_kv, cache_fox_logf, state_nsa_win, page_table,
           g_ffn_a, w_ffn_a_gu, w_ffn_a_down, g_mix, w_in, b_forget, qk_gain, w_cmp, pe_cmp,
           gm_norm, w_spatial, b_spatial, w_branch_a, w_branch_b, w_branch_c, w_out,
           g_ffn_b, w_ffn_b_gu, w_ffn_b_down):
    batch, seq, d_model = x_prompt.shape
    nseq, tn, _ = x_sample.shape
    depth = w_in.shape[0]
    n_pool = cache_nsa_kv.shape[1]
    n_pages = page_table.shape[1]
    past = n_pages * PAGE
    wb = state_nsa_win.shape[2]
    n_prompt = batch * seq
    n_sample = nseq * tn
    tm = next(t for t in (512, 256, 128) if seq % t == 0 and n_sample % t == 0)
    assert seq % FOX_Q_BLOCK == 0 or seq < FOX_Q_BLOCK
    assert seq >= WINDOW + Q_BLOCK and wb % LANES == 0 and wb > LANES
    assert CHUNK % tn == 0 and past % L_SEL == 0 and wb >= tn

    x = jnp.concatenate([x_prompt.reshape(n_prompt, d_model), x_sample.reshape(n_sample, d_model)], axis=0)
    pos = jnp.concatenate([jnp.tile(jnp.arange(seq), batch), jnp.tile(past + jnp.arange(tn), nseq)])
    rope, rope_t = _rope_tables(pos)

    w_p, w_t, w_mg, prm, col = _pack_params(w_in, qk_gain, gm_norm, b_forget, d_model)
    bd = jnp.asarray(np.kron(np.eye(LANES // HEAD_DIM), np.full((HEAD_DIM, HEAD_DIM), 1.0 / HEAD_DIM)), BF16)
    wmix, btab = _gmlp_tables(tn, w_spatial, b_spatial)
    w_l, pe_l = _compress_weights(w_cmp, pe_cmp)
    w2 = _group_major(w_l.reshape(depth, L_CMP * 2 * KV_A, 2 * KV_A))
    pe2 = pe_l.reshape(depth, 1, L_CMP * 2 * KV_A)
    n_cmp_p = seq // L_CMP

    bf = lambda a: a.astype(BF16)
    wgu_a, wd_a, wgu_b, wd_b = bf(w_ffn_a_gu), bf(w_ffn_a_down), bf(w_ffn_b_gu), bf(w_ffn_b_down)
    wa, wb_, wc, wo = bf(w_branch_a), bf(w_branch_b), bf(w_branch_c), bf(w_out)

    cache_nsa_t = _feature_major(cache_nsa_kv, 4 * KV_A)
    cache_fox_t = _feature_major(cache_fox_kv, 2 * WIDTH_B)
    state_t = _feature_major(state_nsa_win, 2 * KV_A)
    logf_t = jnp.swapaxes(cache_fox_logf, 2, 3)
    nb = next(t for t in (64, 32, 16, 8, 4, 2, 1) if n_pool % t == 0)
    n_sel = past // L_SEL + 1
    pair = np.zeros((n_pages * (PAGE // L_CMP), LANES))
    for c in range(pair.shape[0]):
        pair[c, c // 2] = 1.0
    pair = jnp.asarray(pair, BF16)
    ustrict = jnp.asarray(np.triu(np.ones((PAGE, PAGE)), 1).T, BF16)
    assert n_sel <= LANES

    outs = {k: [] for k in ("nsat", "nsas", "wint", "win_s", "foxt", "foxs", "logft", "logfs", "gmv")}
    for l in range(depth):
        x = _ffn(x, g_ffn_a[l][None], wgu_a[l], wd_a[l], tm)
        pr = _proj(x, g_mix[l][None], w_p[l], w_t[l], prm[l], col[l], rope, rope_t, bd, wmix[l], btab[l],
                   tm, batch, seq, n_sample)

        blocks = pr["kcvc"].reshape(batch, n_cmp_p // 2, 2, L_CMP * 2 * KV_A)
        x2 = jnp.swapaxes(blocks, 1, 2).reshape(batch, n_cmp_p, L_CMP * 2 * KV_A)
        kcv = _compress_prompt(x2, pe2[l], w2[l])
        oa = _nsa_prompt(pr["qa"], pr["misc"], kcv, pr["nsat"], pr["wint"], batch, seq)
        ob = _fox_prompt(pr["qb"], pr["foxt"], pr["faug"], batch, seq)

        cmp_pool = _compress_pool(cache_nsa_t, l, pe_l[l], w_l[l], nb)
        cmp_pool3 = cmp_pool.reshape(n_pool, PAGE // L_CMP, 2 * KV_A)
        sl = slice(n_prompt, None)
        oa, win_s = _nsa_sample(page_table, l, pr["qa"][sl].astype(F32), pr["misc"][sl], pr["nsas"],
                                pr["wins"], state_t, pair, cache_nsa_t, cmp_pool3, oa, n_prompt)
        lfn_t = jnp.swapaxes(pr["misc"][sl, :FOX_HEADS].reshape(nseq, tn, FOX_HEADS), 1, 2)
        ob = _fox_sample(page_table, l, pr["qb"][sl].astype(F32), pr["foxs"], lfn_t, ustrict, cache_fox_t,
                         logf_t, ob, n_prompt)

        x = _mix(x, g_mix[l][None], w_mg[l], oa, ob, pr["oc"], wa[l], wb_[l], wc[l], wo[l], tm)
        x = _ffn(x, g_ffn_b[l][None], wgu_b[l], wd_b[l], tm)

        outs["nsat"].append(pr["nsat"])
        outs["nsas"].append(pr["nsas"])
        outs["wint"].append(pr["wint"][:, :, seq - min(WINDOW, seq):])
        outs["win_s"].append(win_s[0])
        outs["foxt"].append(pr["foxt"])
        outs["foxs"].append(pr["foxs"])
        outs["logft"].append(pr["logft"][:, :FOX_HEADS])
        outs["logfs"].append(pr["misc"][sl, :FOX_HEADS])
        outs["gmv"].append(pr["vs"])

    st = {k: jnp.stack(v) for k, v in outs.items()}
    return (
        x[:n_prompt].reshape(batch, seq, d_model),
        x[n_prompt:].reshape(nseq, tn, d_model),
        _token_major_view(st["nsat"], (4, NSA_GROUPS, HEAD_DIM)),
        st["nsas"].reshape(depth, nseq, tn, 4, NSA_GROUPS, HEAD_DIM),
        _token_major_view(st["wint"], (2, NSA_GROUPS, HEAD_DIM)),
        _token_major_view(st["win_s"], (2, NSA_GROUPS, HEAD_DIM)),
        _token_major_view(st["foxt"], (2, FOX_HEADS, HEAD_DIM)),
        st["foxs"].reshape(depth, nseq, tn, 2, FOX_HEADS, HEAD_DIM),
        jnp.swapaxes(st["logft"], 2, 3),
        st["logfs"].reshape(depth, nseq, tn, FOX_HEADS),
        st["gmv"].reshape(depth, nseq, tn, GM_WIDTH),
    )
```

```python
import functools

import numpy as np
import jax
import jax.numpy as jnp
from jax import lax
from jax.experimental import pallas as pl
from jax.experimental.pallas import tpu as pltpu

F32 = jnp.float32
BF16 = jnp.bfloat16

HEAD_DIM = 64
ROT_DIM = HEAD_DIM // 4
ROT_HALF = ROT_DIM // 2
ROPE_THETA = 500000.0
NSA_HEADS = 8
NSA_GROUPS = 2
NSA_HPG = NSA_HEADS // NSA_GROUPS
L_CMP = 32
L_SEL = 64
TOP_N = 16
WINDOW = 512
FOX_HEADS = 4
GM_GROUPS = 4
GM_DIM = 64
GM_WIDTH = GM_GROUPS * GM_DIM
CHUNK = 128
PAGE = 128
EPS = 1e-6
NEG_BIG = -1e30
WIDTH_A = NSA_HEADS * HEAD_DIM
WIDTH_B = FOX_HEADS * HEAD_DIM
KV_A = NSA_GROUPS * HEAD_DIM
QK_SCALE = HEAD_DIM ** -0.5

LANES = 128
SUBLANES = 8
Q_BLOCK = 256
KEY_TILE = 256
FOX_Q_BLOCK = 1024
FAUG_ROWS = 16
SEQ_PER_STEP = 2
CMP_PITCH = L_CMP + 4
MASK_BIG = float(2 ** 30)
N_FORCED = 3
VMEM_LIMIT = 56 * 1024 * 1024

P_QA = 0
P_QB = 512
P_GM = 768
P_MISC = 1280
P_COLS = 1408
T_KC, T_VC, T_KS, T_VS, T_KW, T_VW, T_KB, T_VB, T_FL, T_ROWS = 0, 128, 256, 384, 512, 640, 768, 1024, 1280, 1288
R_GQ, R_GQB, R_GMN, R_BF, R_COLS = 0, 512, 768, 1024, 1152
GATE_LANE0 = FOX_HEADS


def _dot(a, b):
    return jnp.dot(a, b, preferred_element_type=F32)


def _dot_nt(a, b):
    return lax.dot_general(a, b, (((1,), (1,)), ((), ())), preferred_element_type=F32)


def _split3(x):
    a = x.astype(BF16)
    r = x - a.astype(F32)
    b = r.astype(BF16)
    c = (r - b.astype(F32)).astype(BF16)
    return a, b, c


def _rms_rows(x, g):
    return x * lax.rsqrt(jnp.mean(x * x, axis=-1, keepdims=True) + EPS) * g


def _log_sigmoid(z):
    return jnp.minimum(z, 0.0) - jnp.log(1.0 + jnp.exp(-jnp.abs(z)))


def _cparams(sem):
    return pltpu.CompilerParams(dimension_semantics=sem, vmem_limit_bytes=VMEM_LIMIT)


def _ffn_kernel(x_ref, g_ref, wgu_ref, wd_ref, o_ref, acc_ref, *, d_ff, chunk):
    x = x_ref[...]
    h = _rms_rows(x, g_ref[...]).astype(BF16)
    acc_ref[...] = jnp.zeros_like(acc_ref)
    for c in range(d_ff // chunk):
        g = _dot(h, wgu_ref[:, c * chunk:(c + 1) * chunk])
        u = _dot(h, wgu_ref[:, d_ff + c * chunk:d_ff + (c + 1) * chunk])
        a = (jax.nn.silu(g) * u).astype(BF16)
        acc_ref[...] += _dot(a, wd_ref[c * chunk:(c + 1) * chunk, :])
    o_ref[...] = x + 0.5 * acc_ref[...]


def _ffn(x, g, wgu, wd, tm):
    m, d = x.shape
    d_ff = wd.shape[0]
    return pl.pallas_call(
        functools.partial(_ffn_kernel, d_ff=d_ff, chunk=256),
        grid=(m // tm,),
        in_specs=[pl.BlockSpec((tm, d), lambda i: (i, 0)),
                  pl.BlockSpec((1, d), lambda i: (0, 0)),
                  pl.BlockSpec((d, 2 * d_ff), lambda i: (0, 0)),
                  pl.BlockSpec((d_ff, d), lambda i: (0, 0))],
        out_specs=pl.BlockSpec((tm, d), lambda i: (i, 0)),
        out_shape=jax.ShapeDtypeStruct((m, d), F32),
        scratch_shapes=[pltpu.VMEM((tm, d), F32)],
        compiler_params=_cparams(("parallel",)),
        name="ffn",
    )(x, g, wgu, wd)


def _proj_kernel(x_ref, gmix_ref, w_ref, wt_ref, prm_ref, col_ref, rope_ref, ropet_ref, bd_ref, wmix_ref, btab_ref,
                 qa_ref, qb_ref, misc_ref, oc_ref, kcvc_ref, nsat_ref, wint_ref, foxt_ref, faug_ref, logft_ref,
                 nsas_ref, wins_ref, foxs_ref, vs_ref, v_scr, carry_ref, *, tm, tiles_per_seq, n_prompt_tiles):
    i = pl.program_id(0)
    x = x_ref[...]
    h = _rms_rows(x, gmix_ref[...]).astype(BF16)
    lane = lax.broadcasted_iota(jnp.int32, (tm, LANES), 1)
    lo64 = lane < HEAD_DIM

    cos = rope_ref[:, 0:128]
    sin_lo = rope_ref[:, 128:256]
    sin_hi = rope_ref[:, 256:384]
    bd = bd_ref[...]

    def seg(a, b):
        return _dot(h, w_ref[:, a:b])

    def headnorm(t, gain):
        hi, lo, _ = _split3(t * t)
        ms = _dot(hi, bd) + _dot(lo, bd)
        return t * lax.rsqrt(ms + EPS) * gain

    def rope(t):
        return t * cos + pltpu.roll(t, LANES - ROT_HALF, 1) * sin_lo + pltpu.roll(t, ROT_HALF, 1) * sin_hi

    def head_split(t):
        return (jnp.where(lo64, t, 0.0).astype(BF16),
                jnp.where(lo64, pltpu.roll(t, HEAD_DIM, 1), 0.0).astype(BF16))

    for c in range(WIDTH_A // LANES):
        t = seg(P_QA + c * LANES, P_QA + (c + 1) * LANES)
        t = rope(headnorm(t, prm_ref[:, R_GQ + c * LANES:R_GQ + (c + 1) * LANES])) * QK_SCALE
        a, b = head_split(t)
        qa_ref[:, (2 * c) * LANES:(2 * c + 1) * LANES] = a
        qa_ref[:, (2 * c + 1) * LANES:(2 * c + 2) * LANES] = b
    for c in range(WIDTH_B // LANES):
        t = seg(P_QB + c * LANES, P_QB + (c + 1) * LANES)
        t = headnorm(t, prm_ref[:, R_GQB + c * LANES:R_GQB + (c + 1) * LANES]) * QK_SCALE
        a, b = head_split(t)
        qb_ref[:, (2 * c) * LANES:(2 * c + 1) * LANES] = a
        qb_ref[:, (2 * c + 1) * LANES:(2 * c + 2) * LANES] = b

    lane_grp = lax.broadcasted_iota(jnp.int32, (CHUNK, GM_WIDTH), 1) // GM_DIM
    for c in range(GM_WIDTH // LANES):
        gv = jax.nn.gelu(seg(P_GM + GM_WIDTH + c * LANES, P_GM + GM_WIDTH + (c + 1) * LANES))
        v_scr[:, c * LANES:(c + 1) * LANES] = headnorm(gv, prm_ref[:, R_GMN + c * LANES:R_GMN + (c + 1) * LANES])
    for r in range(tm // CHUNK):
        rows = slice(r * CHUNK, (r + 1) * CHUNK)
        vsub = v_scr[rows, :]
        s = btab_ref[0]
        for g in range(GM_GROUPS):
            s = s + _dot(wmix_ref[0, g], jnp.where(lane_grp == g, vsub, 0.0).astype(BF16))
        u = jnp.concatenate(
            [jax.nn.gelu(_dot(h[rows, :], w_ref[:, P_GM + c * LANES:P_GM + (c + 1) * LANES]))
             for c in range(GM_WIDTH // LANES)], axis=1)
        oc_ref[rows, :] = (u * s).astype(BF16)

    t = seg(P_MISC, P_MISC + LANES)
    logf = _log_sigmoid(t + prm_ref[:, R_BF:R_BF + LANES])
    misc_ref[...] = jnp.where(lane < FOX_HEADS, logf, jax.nn.sigmoid(t))

    cos_t = ropet_ref[0:ROT_HALF, :]
    sin_t = ropet_ref[ROT_HALF:ROT_DIM, :]

    def seg_t(a, b):
        return _dot_nt(wt_ref[a:b, :], h)

    def headnorm_t(t, row0, rot):
        outs = []
        for hh in range(t.shape[0] // HEAD_DIM):
            blk = t[hh * HEAD_DIM:(hh + 1) * HEAD_DIM]
            ms = jnp.mean(blk * blk, axis=0, keepdims=True)
            n = blk * lax.rsqrt(ms + EPS) * col_ref[row0 + hh * HEAD_DIM:row0 + (hh + 1) * HEAD_DIM, :]
            if rot:
                x1, x2 = n[0:ROT_HALF], n[ROT_HALF:ROT_DIM]
                n = jnp.concatenate([x1 * cos_t - x2 * sin_t, x2 * cos_t + x1 * sin_t, n[ROT_DIM:]], axis=0)
            outs.append(n)
        return jnp.concatenate(outs, axis=0)

    kc = headnorm_t(seg_t(T_KC, T_KC + KV_A), T_KC, True)
    vc = seg_t(T_VC, T_VC + KV_A)
    ks = headnorm_t(seg_t(T_KS, T_KS + KV_A), T_KS, True)
    vs = seg_t(T_VS, T_VS + KV_A)
    kw = headnorm_t(seg_t(T_KW, T_KW + KV_A), T_KW, True)
    vw = seg_t(T_VW, T_VW + KV_A)
    kb = headnorm_t(seg_t(T_KB, T_KB + WIDTH_B), T_KB, False)
    vb = seg_t(T_VB, T_VB + WIDTH_B)

    @pl.when(i % tiles_per_seq == 0)
    def _():
        carry_ref[...] = jnp.zeros_like(carry_ref)

    zf = seg_t(T_FL, T_FL + SUBLANES) + col_ref[T_FL:T_FL + SUBLANES, :]
    sub = lax.broadcasted_iota(jnp.int32, (SUBLANES, tm), 0)
    lane_t = lax.broadcasted_iota(jnp.int32, (SUBLANES, tm), 1)
    logf_t = jnp.where(sub < FOX_HEADS, _log_sigmoid(zf), 0.0)
    cum = logf_t
    shift = 1
    while shift < tm:
        cum = cum + jnp.where(lane_t >= shift, pltpu.roll(cum, shift, 1), 0.0)
        shift *= 2
    cum = cum + carry_ref[:, 0:1]
    carry_ref[...] = jnp.broadcast_to(cum[:, tm - 1:tm], carry_ref.shape)
    hi, mid, lo = [p.astype(F32) for p in _split3(-cum)]
    sub16 = lax.broadcasted_iota(jnp.int32, (FAUG_ROWS, tm), 0)

    @pl.when(i < n_prompt_tiles)
    def _():
        nsat_ref[0, 0:128, :] = kc
        nsat_ref[0, 128:256, :] = vc
        nsat_ref[0, 256:384, :] = ks
        nsat_ref[0, 384:512, :] = vs
        wint_ref[0, 0:128, :] = kw
        wint_ref[0, 128:256, :] = vw
        foxt_ref[0, 0:WIDTH_B, :] = kb
        foxt_ref[0, WIDTH_B:2 * WIDTH_B, :] = vb
        logft_ref[0] = logf_t
        for hh in range(FOX_HEADS):
            blk = jnp.where(sub16 == 0, hi[hh:hh + 1],
                            jnp.where(sub16 == 1, mid[hh:hh + 1], jnp.where(sub16 == 2, lo[hh:hh + 1], 0.0)))
            faug_ref[0, hh * FAUG_ROWS:(hh + 1) * FAUG_ROWS, :] = blk.astype(BF16)
        kcvc_ref[:, 0:128] = kc.T
        kcvc_ref[:, 128:256] = vc.T

    @pl.when(i >= n_prompt_tiles)
    def _():
        for j, t_ in enumerate((kc, vc, ks, vs)):
            nsas_ref[:, j * KV_A:(j + 1) * KV_A] = t_.T
        wins_ref[:, 0:128] = kw.T
        wins_ref[:, 128:256] = vw.T
        for c in range(WIDTH_B // LANES):
            foxs_ref[:, c * LANES:(c + 1) * LANES] = kb[c * LANES:(c + 1) * LANES].T
            foxs_ref[:, WIDTH_B + c * LANES:WIDTH_B + (c + 1) * LANES] = vb[c * LANES:(c + 1) * LANES].T
        vs_ref[...] = v_scr[...]


def _proj(x, gmix, w, wt, prm, col, rope, ropet, bd, wmix, btab, tm, batch, seq, n_sample, layer, depth, stacked):
    m, d = x.shape
    tps = seq // tm
    npt = batch * tps
    row = lambda width: pl.BlockSpec((tm, width), lambda i: (i, 0))
    full = lambda a: pl.BlockSpec(a.shape, lambda i: (0,) * a.ndim)
    kind = lambda i: (i >= npt).astype(jnp.int32)

    def featmajor(rows):
        def idx(i):
            ii = jnp.minimum(i, npt - 1)
            return (ii // tps, 0, ii % tps)
        return pl.BlockSpec((1, rows, tm), idx)

    def featmajor_stacked(rows):
        def idx(i):
            ii = jnp.minimum(i, npt - 1)
            return (layer, ii // tps, 0, ii % tps)
        return pl.BlockSpec((None, 1, rows, tm), idx)

    prow = lambda width: pl.BlockSpec((tm, width), lambda i: (jnp.minimum(i, npt - 1), 0))
    srow = lambda width: pl.BlockSpec((tm, width), lambda i: (jnp.maximum(i - npt, 0), 0))
    n_prompt = batch * seq
    outs = [
        ("qa", row(2 * WIDTH_A), (m, 2 * WIDTH_A), BF16),
        ("qb", row(2 * WIDTH_B), (m, 2 * WIDTH_B), BF16),
        ("misc", row(LANES), (m, LANES), F32),
        ("oc", row(GM_WIDTH), (m, GM_WIDTH), BF16),
        ("kcvc", prow(2 * KV_A), (n_prompt, 2 * KV_A), F32),
        ("nsat", featmajor_stacked(4 * KV_A), (depth, batch, 4 * KV_A, seq), F32),
        ("wint", featmajor(2 * KV_A), (batch, 2 * KV_A, seq), F32),
        ("foxt", featmajor_stacked(2 * WIDTH_B), (depth, batch, 2 * WIDTH_B, seq), F32),
        ("faug", featmajor(FOX_HEADS * FAUG_ROWS), (batch, FOX_HEADS * FAUG_ROWS, seq), BF16),
        ("logft", featmajor(SUBLANES), (batch, SUBLANES, seq), F32),
        ("nsas", srow(4 * KV_A), (n_sample, 4 * KV_A), F32),
        ("wins", srow(2 * KV_A), (n_sample, 2 * KV_A), F32),
        ("foxs", srow(2 * WIDTH_B), (n_sample, 2 * WIDTH_B), F32),
        ("vs", srow(GM_WIDTH), (n_sample, GM_WIDTH), F32),
    ]
    names = [o[0] for o in outs]
    n_in = 11
    carried = () if stacked is None else tuple(stacked)
    aliases = {n_in + k: names.index(n) for k, n in enumerate(("nsat", "foxt")[:len(carried)])}
    kern = functools.partial(_proj_kernel, tm=tm, tiles_per_seq=tps, n_prompt_tiles=npt)

    def body(*refs):
        kern(*refs[:n_in], *refs[n_in + len(carried):])

    res = pl.pallas_call(
        body,
        grid=(m // tm,),
        in_specs=[row(d), full(gmix), full(w), full(wt), full(prm), full(col), row(3 * LANES),
                  pl.BlockSpec((ROT_DIM, tm), lambda i: (0, i)), full(bd),
                  pl.BlockSpec((1,) + wmix.shape[1:], lambda i: (kind(i), 0, 0, 0)),
                  pl.BlockSpec((1,) + btab.shape[1:], lambda i: (kind(i), 0, 0))]
                 + [pl.BlockSpec(memory_space=pl.ANY)] * len(carried),
        out_specs=[o[1] for o in outs],
        out_shape=[jax.ShapeDtypeStruct(o[2], o[3]) for o in outs],
        scratch_shapes=[pltpu.VMEM((tm, GM_WIDTH), F32), pltpu.VMEM((SUBLANES, LANES), F32)],
        input_output_aliases=aliases,
        compiler_params=_cparams(("arbitrary",)),
        name="proj",
    )(x, gmix, w, wt, prm, col, rope, ropet, bd, wmix, btab, *carried)
    return dict(zip(names, res))


def _mix_kernel(x_ref, gmix_ref, wmg_ref, oa_ref, ob_ref, oc_ref, wa_ref, wb_ref, wc_ref, wout_ref, o_ref):
    x = x_ref[...]
    d = x.shape[1]
    h = _rms_rows(x, gmix_ref[...]).astype(BF16)
    m = jax.nn.sigmoid(_dot(h, wmg_ref[:, 0:d])) * _dot(oa_ref[...].astype(BF16), wa_ref[...])
    m = m + jax.nn.sigmoid(_dot(h, wmg_ref[:, d:2 * d])) * _dot(ob_ref[...].astype(BF16), wb_ref[...])
    m = m + jax.nn.sigmoid(_dot(h, wmg_ref[:, 2 * d:3 * d])) * _dot(oc_ref[...], wc_ref[...])
    o_ref[...] = x + _dot(m.astype(BF16), wout_ref[...])


def _mix(x, gmix, wmg, oa, ob, oc, wa, wb, wc, wout, tm):
    m, d = x.shape
    row = lambda a: pl.BlockSpec((tm, a.shape[1]), lambda i: (i, 0))
    full = lambda a: pl.BlockSpec(a.shape, lambda i: (0,) * a.ndim)
    return pl.pallas_call(
        _mix_kernel,
        grid=(m // tm,),
        in_specs=[row(x), full(gmix), full(wmg), row(oa), row(ob), row(oc), full(wa), full(wb), full(wc),
                  full(wout)],
        out_specs=row(x),
        out_shape=jax.ShapeDtypeStruct((m, d), F32),
        compiler_params=_cparams(("parallel",)),
        name="mix",
    )(x, gmix, wmg, oa, ob, oc, wa, wb, wc, wout)


def _compress_kernel(x_ref, pe_ref, w_ref, o_ref):
    o_ref[0] = _dot((x_ref[0] + pe_ref[...]).astype(BF16), w_ref[...])


def _compress_prompt(x2, pe2, w2):
    b, nc, k = x2.shape
    n = w2.shape[1]
    return pl.pallas_call(
        _compress_kernel,
        grid=(b,),
        in_specs=[pl.BlockSpec((1, nc, k), lambda i: (i, 0, 0)),
                  pl.BlockSpec((1, k), lambda i: (0, 0)),
                  pl.BlockSpec((k, n), lambda i: (0, 0))],
        out_specs=pl.BlockSpec((1, nc, n), lambda i: (i, 0, 0)),
        out_shape=jax.ShapeDtypeStruct((b, nc, n), F32),
        compiler_params=_cparams(("parallel",)),
        name="compress_prompt",
    )(x2, pe2, w2)


def _online_update(s, vt_aug, acc_ref, m_ref):
    n_chunk = s.shape[1] // LANES
    m_old = m_ref[...]
    mx = s[:, 0:LANES]
    for c in range(1, n_chunk):
        mx = jnp.maximum(mx, s[:, c * LANES:(c + 1) * LANES])
    m_new = jnp.maximum(m_old, jnp.max(mx, axis=-1, keepdims=True))
    alpha = jnp.exp(m_old - m_new)
    p = jnp.concatenate(
        [jnp.exp(s[:, c * LANES:(c + 1) * LANES] - m_new).astype(BF16) for c in range(n_chunk)], axis=1)
    acc_ref[...] = alpha * acc_ref[...] + _dot_nt(p, vt_aug)
    m_ref[...] = m_new


def _finish(acc):
    return (acc / pltpu.roll(acc, HEAD_DIM, 1))[:, :HEAD_DIM]


def _select_blocks(imp_t, tok_row):
    nb, nt = imp_t.shape
    blk = lax.broadcasted_iota(jnp.int32, (nb, nt), 0)
    cur = jnp.broadcast_to(tok_row, (nb, nt)) // L_SEL
    forced = (blk == 0) | (blk == cur) | (blk == cur - 1)
    visible = blk <= cur
    score = jnp.where(visible & jnp.logical_not(forced), imp_t, -jnp.inf)
    sel = jnp.where(visible & forced, 1.0, 0.0)
    for _ in range(min(TOP_N - N_FORCED, nb)):
        mx = jnp.max(score, axis=0, keepdims=True)
        first = jnp.min(jnp.where(score == mx, blk, nb), axis=0, keepdims=True)
        hit = (blk == first) & (mx > -jnp.inf)
        sel = jnp.where(hit, 1.0, sel)
        score = jnp.where(hit, -jnp.inf, score)
    return sel


def _nsa_prompt_kernel(q_ref, misc_ref, kcv_ref, kv_ref, win_ref, o_ref, acc_ref, m_ref, *, n_cmp):
    i = pl.program_id(1)
    rows = NSA_HPG * Q_BLOCK
    tok0 = i * Q_BLOCK
    tok = tok0 + (lax.broadcasted_iota(jnp.int32, (rows, 1), 0) % Q_BLOCK)
    half = n_cmp // 2
    ones_rows = jnp.ones((HEAD_DIM, 1), BF16)
    zero_rows = jnp.zeros((HEAD_DIM, 1), BF16)

    q_los, q_augs, o_cs = [], [], []
    for g in range(NSA_GROUPS):
        gl = slice(g * LANES, (g + 1) * LANES)
        q_lo = jnp.concatenate(
            [q_ref[:, (g * NSA_HPG + h) * LANES:(g * NSA_HPG + h + 1) * LANES] for h in range(NSA_HPG)], axis=0)
        kcv = kcv_ref[0, :, gl].astype(BF16)
        s_c = _dot_nt(q_lo, kcv)
        col = lax.broadcasted_iota(jnp.int32, (1, n_cmp), 1)
        blk_c = 2 * (col % half) + col // half
        s_c = jnp.where((blk_c + 1) * L_CMP - 1 <= tok, s_c, NEG_BIG)
        e_c = jnp.exp(s_c - jnp.max(s_c, axis=-1, keepdims=True))
        p_c = e_c / jnp.sum(e_c, axis=-1, keepdims=True)
        p_c = jnp.where(tok >= L_CMP - 1, p_c, 0.0)
        o_cs.append(_dot(p_c.astype(BF16), kcv)[:, HEAD_DIM:])
        imp = p_c[0:Q_BLOCK]
        for h in range(1, NSA_HPG):
            imp = imp + p_c[h * Q_BLOCK:(h + 1) * Q_BLOCK]
        imp_sel = imp[:, :half] + imp[:, half:]
        if half < LANES:
            imp_sel = jnp.concatenate([imp_sel, jnp.zeros((Q_BLOCK, LANES - half), F32)], axis=1)
        sel_t = _select_blocks(imp_sel.T, tok0 + lax.broadcasted_iota(jnp.int32, (1, Q_BLOCK), 1))
        nsel = (1.0 - sel_t).T.astype(BF16)
        q_los.append(q_lo)
        q_augs.append(jnp.concatenate([q_lo, jnp.concatenate([nsel] * NSA_HPG, axis=0)], axis=1))

    acc_ref[...] = jnp.zeros_like(acc_ref)
    m_ref[...] = jnp.full_like(m_ref, -jnp.inf)
    n_kt = (tok0 + Q_BLOCK + KEY_TILE - 1) // KEY_TILE

    def sel_tile(kt, causal, width=KEY_TILE):
        key0 = pl.multiple_of(kt * KEY_TILE, KEY_TILE)
        ks = pl.ds(key0, width)
        crow = lax.broadcasted_iota(jnp.int32, (LANES, width), 0)
        kblk = (key0 + lax.broadcasted_iota(jnp.int32, (LANES, width), 1)) // L_SEL
        aug = jnp.where(crow == kblk, -MASK_BIG, 0.0).astype(BF16)
        for g in range(NSA_GROUPS):
            k_t = kv_ref[0, g * HEAD_DIM:(g + 1) * HEAD_DIM, ks].astype(BF16)
            v_t = kv_ref[0, KV_A + g * HEAD_DIM:KV_A + (g + 1) * HEAD_DIM, ks].astype(BF16)
            zeros = jnp.broadcast_to(zero_rows, (HEAD_DIM, width))
            s = _dot(q_augs[g], jnp.concatenate([k_t, zeros, aug], axis=0))
            if causal:
                keypos = key0 + lax.broadcasted_iota(jnp.int32, (1, width), 1)
                s = jnp.where(keypos <= tok, s, -MASK_BIG)
            ones = jnp.broadcast_to(ones_rows, (HEAD_DIM, width))
            _online_update(s, jnp.concatenate([v_t, ones], axis=0), acc_ref.at[g], m_ref.at[g])

    def sel_body(j, carry):
        sel_tile(2 * j, False, 2 * KEY_TILE)
        return carry

    n_pairs = (n_kt - 1) // 2
    lax.fori_loop(0, n_pairs, sel_body, 0)

    @pl.when(2 * n_pairs < n_kt - 1)
    def _():
        sel_tile(n_kt - 2, False)

    sel_tile(n_kt - 1, True)

    band = WINDOW + Q_BLOCK
    start = pl.multiple_of(jnp.maximum(tok0 - WINDOW, 0), Q_BLOCK)
    wk = pl.ds(start, band)
    dpos = tok - (start + lax.broadcasted_iota(jnp.int32, (1, band), 1))
    ok = (dpos >= 0) & (dpos < WINDOW)
    gates = misc_ref[...]
    for g in range(NSA_GROUPS):
        k_t = win_ref[0, g * HEAD_DIM:(g + 1) * HEAD_DIM, wk].astype(BF16)
        v_t = win_ref[0, KV_A + g * HEAD_DIM:KV_A + (g + 1) * HEAD_DIM, wk].astype(BF16)
        zeros = jnp.broadcast_to(zero_rows, (HEAD_DIM, band))
        s = jnp.where(ok, _dot(q_los[g], jnp.concatenate([k_t, zeros], axis=0)), NEG_BIG)
        p = jnp.exp(s - jnp.max(s, axis=-1, keepdims=True)).astype(BF16)
        ones = jnp.broadcast_to(ones_rows, (HEAD_DIM, band))
        o_w = _finish(_dot_nt(p, jnp.concatenate([v_t, ones], axis=0)))
        o_s = _finish(acc_ref[g])
        o_c = o_cs[g]
        pieces = []
        for h in range(NSA_HPG):
            hr = slice(h * Q_BLOCK, (h + 1) * Q_BLOCK)
            c0 = GATE_LANE0 + (g * NSA_HPG + h) * 3
            pieces.append(gates[:, c0:c0 + 1] * o_c[hr] + gates[:, c0 + 1:c0 + 2] * o_s[hr]
                          + gates[:, c0 + 2:c0 + 3] * o_w[hr])
        o_ref[:, g * NSA_HPG * HEAD_DIM:(g + 1) * NSA_HPG * HEAD_DIM] = jnp.concatenate(pieces, axis=1)


def _nsa_prompt(qa, misc, kcv, nsat, wint, batch, seq, layer):
    m = qa.shape[0]
    nq = seq // Q_BLOCK
    n_cmp = kcv.shape[1]
    rows = NSA_HPG * Q_BLOCK
    return pl.pallas_call(
        functools.partial(_nsa_prompt_kernel, n_cmp=n_cmp),
        grid=(batch, nq),
        in_specs=[pl.BlockSpec((Q_BLOCK, 2 * WIDTH_A), lambda b, i: (b * nq + i, 0)),
                  pl.BlockSpec((Q_BLOCK, LANES), lambda b, i: (b * nq + i, 0)),
                  pl.BlockSpec((1, n_cmp, 2 * KV_A), lambda b, i: (b, 0, 0)),
                  pl.BlockSpec((None, 1, 2 * KV_A, seq), lambda b, i: (layer, b, 1, 0)),
                  pl.BlockSpec((1, 2 * KV_A, seq), lambda b, i: (b, 0, 0))],
        out_specs=pl.BlockSpec((Q_BLOCK, WIDTH_A), lambda b, i: (b * nq + i, 0)),
        out_shape=jax.ShapeDtypeStruct((m, WIDTH_A), F32),
        scratch_shapes=[pltpu.VMEM((NSA_GROUPS, rows, LANES), F32), pltpu.VMEM((NSA_GROUPS, rows, LANES), F32)],
        compiler_params=_cparams(("parallel", "arbitrary")),
        name="nsa_prompt",
    )(qa, misc, kcv, nsat, wint)


def _fox_prompt_kernel(q_ref, kv_ref, faug_ref, o_ref, acc_ref, m_ref, *, qb):
    i = pl.program_id(1)
    tok = i * qb + lax.broadcasted_iota(jnp.int32, (qb, 1), 0)
    lane = lax.broadcasted_iota(jnp.int32, (qb, LANES), 1)
    ones_lanes = jnp.where(lane < HEAD_DIM + 3, 1.0, 0.0).astype(BF16)
    q_augs = [jnp.where(lane < HEAD_DIM, q_ref[:, h * LANES:(h + 1) * LANES], ones_lanes) for h in range(FOX_HEADS)]
    pad_rows = LANES - HEAD_DIM - FAUG_ROWS
    acc_ref[...] = jnp.zeros_like(acc_ref)
    m_ref[...] = jnp.full_like(m_ref, -jnp.inf)

    def tile(kt, causal, width=KEY_TILE):
        key0 = pl.multiple_of(kt * KEY_TILE, KEY_TILE)
        ks = pl.ds(key0, width)
        zeros = jnp.zeros((pad_rows, width), BF16)
        ones = jnp.ones((HEAD_DIM, width), BF16)
        for h in range(FOX_HEADS):
            k_t = kv_ref[0, h * HEAD_DIM:(h + 1) * HEAD_DIM, ks].astype(BF16)
            v_t = kv_ref[0, WIDTH_B + h * HEAD_DIM:WIDTH_B + (h + 1) * HEAD_DIM, ks].astype(BF16)
            fa = faug_ref[0, h * FAUG_ROWS:(h + 1) * FAUG_ROWS, ks]
            s = _dot(q_augs[h], jnp.concatenate([k_t, fa, zeros], axis=0))
            if causal:
                keypos = key0 + lax.broadcasted_iota(jnp.int32, (1, width), 1)
                s = jnp.where(keypos <= tok, s, -MASK_BIG)
            _online_update(s, jnp.concatenate([v_t, ones], axis=0), acc_ref.at[h], m_ref.at[h])

    def body(j, carry):
        tile(2 * j, False, 2 * KEY_TILE)
        return carry

    n_full = i * (qb // KEY_TILE)
    assert (qb // KEY_TILE) % 2 == 0
    lax.fori_loop(0, n_full // 2, body, 0)
    for d in range(qb // KEY_TILE):
        tile(n_full + d, True)
    for h in range(FOX_HEADS):
        o_ref[:, h * HEAD_DIM:(h + 1) * HEAD_DIM] = _finish(acc_ref[h])


def _fox_prompt(qb_arr, foxt, faug, batch, seq, layer):
    m = qb_arr.shape[0]
    qb = min(FOX_Q_BLOCK, seq)
    nq = seq // qb
    return pl.pallas_call(
        functools.partial(_fox_prompt_kernel, qb=qb),
        grid=(batch, nq),
        in_specs=[pl.BlockSpec((qb, 2 * WIDTH_B), lambda b, i: (b * nq + i, 0)),
                  pl.BlockSpec((None, 1, 2 * WIDTH_B, seq), lambda b, i: (layer, b, 0, 0)),
                  pl.BlockSpec((1, FOX_HEADS * FAUG_ROWS, seq), lambda b, i: (b, 0, 0))],
        out_specs=pl.BlockSpec((qb, WIDTH_B), lambda b, i: (b * nq + i, 0)),
        out_shape=jax.ShapeDtypeStruct((m, WIDTH_B), F32),
        scratch_shapes=[pltpu.VMEM((FOX_HEADS, qb, LANES), F32), pltpu.VMEM((FOX_HEADS, qb, LANES), F32)],
        compiler_params=_cparams(("parallel", "arbitrary")),
        name="fox_prompt",
    )(qb_arr, foxt, faug)


def _cmp_pool_kernel(x_ref, pe_ref, w_ref, o_ref, rows_ref, *, nb):
    per = PAGE // L_CMP

    def put(p, carry):
        base = pl.multiple_of(p * (per * CMP_PITCH), SUBLANES)
        for c in range(2):
            rows = x_ref[0, p, c * KV_A:(c + 1) * KV_A, :].T
            for n in range(per):
                rows_ref[c, pl.ds(base + n * CMP_PITCH, L_CMP), :] = rows[n * L_CMP:(n + 1) * L_CMP]
        return carry

    lax.fori_loop(0, nb, put, 0, unroll=2)
    n_blk = nb * per
    for c in range(2):
        cs = slice(c * KV_A, (c + 1) * KV_A)
        acc = jnp.zeros((n_blk, KV_A), F32)
        for l in range(L_CMP):
            xl = rows_ref[c, pl.ds(l, n_blk, stride=CMP_PITCH), :]
            acc = acc + _dot((xl + pe_ref[l:l + 1, cs]).astype(BF16), w_ref[l, cs, cs])
        o_ref[:, cs] = acc


def _compress_pool(cache_t, layer, pe_l, w_l, nb):
    n_pool = cache_t.shape[1]
    per = PAGE // L_CMP
    return pl.pallas_call(
        functools.partial(_cmp_pool_kernel, nb=nb),
        grid=(n_pool // nb,),
        in_specs=[pl.BlockSpec((1, nb, 2 * KV_A, PAGE), lambda i: (layer, i, 0, 0)),
                  pl.BlockSpec(pe_l.shape, lambda i: (0, 0)),
                  pl.BlockSpec(w_l.shape, lambda i: (0, 0, 0))],
        out_specs=pl.BlockSpec((nb * per, 2 * KV_A), lambda i: (i, 0)),
        out_shape=jax.ShapeDtypeStruct((n_pool * per, 2 * KV_A), F32),
        scratch_shapes=[pltpu.VMEM((2, nb * per * CMP_PITCH, KV_A), F32)],
        compiler_params=_cparams(("parallel",)),
        name="compress_pool",
    )(cache_t, pe_l, w_l)


def _softmax_pieces(pieces):
    mx = functools.reduce(jnp.maximum, [jnp.max(s, axis=-1, keepdims=True) for s in pieces])
    es = [jnp.exp(s - mx) for s in pieces]
    den = functools.reduce(lambda a, b: a + b, [jnp.sum(e, axis=-1, keepdims=True) for e in es])
    return es, den


def _nsa_sample_kernel(pt_ref, q_ref, misc_ref, new_ref, wnew_ref, state_ref, pair_ref, *rest, n_pages, past):
    del pt_ref
    pages = rest[:n_pages]
    cmps = rest[n_pages:2 * n_pages]
    o_ref, wout_ref = rest[2 * n_pages:]
    tn = q_ref.shape[0]
    rows = NSA_HEADS * tn
    rid = lax.broadcasted_iota(jnp.int32, (rows, 1), 0)
    qi = rid % tn
    tok = past + qi
    is_g1 = rid >= NSA_HPG * tn

    q = q_ref[...]
    blocks = []
    for g in range(NSA_GROUPS):
        for h in range(NSA_HPG):
            c = q[:, (g * NSA_HPG + h) * LANES:(g * NSA_HPG + h + 1) * LANES]
            blocks.append(pltpu.roll(c, HEAD_DIM, 1) if g == 1 else c)
    q64 = jnp.concatenate(blocks, axis=0).astype(BF16)

    def pick(x):
        return jnp.where(is_g1, x[:, HEAD_DIM:2 * HEAD_DIM], x[:, 0:HEAD_DIM])

    def pad_keys(x):
        return jnp.concatenate([x, jnp.zeros((2 * tn - x.shape[0], x.shape[1]), x.dtype)], axis=0)

    n_cmp = n_pages * (PAGE // L_CMP)
    kcv = jnp.concatenate([c[0] for c in cmps], axis=0).astype(BF16)
    s_c = _dot_nt(q64, kcv[:, 0:KV_A])
    blk_c = lax.broadcasted_iota(jnp.int32, (1, n_cmp), 1)
    s_c = jnp.where((blk_c + 1) * L_CMP - 1 <= tok, s_c, NEG_BIG)
    e_c = jnp.exp(s_c - jnp.max(s_c, axis=-1, keepdims=True))
    p_c = e_c / jnp.sum(e_c, axis=-1, keepdims=True)
    p_c = jnp.where(tok >= L_CMP - 1, p_c, 0.0)
    o_c = pick(_dot(p_c.astype(BF16), kcv[:, KV_A:2 * KV_A]))

    imps = []
    for g in range(NSA_GROUPS):
        imp = p_c[g * NSA_HPG * tn:g * NSA_HPG * tn + tn]
        for h in range(1, NSA_HPG):
            imp = imp + p_c[(g * NSA_HPG + h) * tn:(g * NSA_HPG + h + 1) * tn]
        imps.append(imp)
    imp2 = jnp.concatenate(imps + [jnp.zeros((LANES - NSA_GROUPS * tn, n_cmp), F32)], axis=0)
    imp_sel = jnp.zeros((LANES, LANES), F32)
    for part in _split3(imp2):
        imp_sel = imp_sel + _dot(part, pair_ref[...])
    tok_row = past + lax.broadcasted_iota(jnp.int32, (1, LANES), 1) % tn
    sel = _select_blocks(imp_sel.T, tok_row).T
    bias2 = jnp.where(sel > 0.0, 0.0, -MASK_BIG)
    bias_sel = jnp.concatenate([bias2[0:tn]] * NSA_HPG + [bias2[tn:2 * tn]] * NSA_HPG, axis=0)

    lane_lo = lax.broadcasted_iota(jnp.int32, (rows, PAGE), 1) < L_SEL
    new_kv = pad_keys(new_ref[:, 2 * KV_A:4 * KV_A]).astype(BF16)
    scores = []
    for j in range(n_pages):
        bias = jnp.where(lane_lo, bias_sel[:, 2 * j:2 * j + 1], bias_sel[:, 2 * j + 1:2 * j + 2])
        scores.append(_dot(q64, pages[j][0, 0, 0:KV_A, :].astype(BF16)) + bias)
    kcol = lax.broadcasted_iota(jnp.int32, (1, 2 * tn), 1)
    s_new = _dot_nt(q64, new_kv[:, 0:KV_A]) + bias_sel[:, 2 * n_pages:2 * n_pages + 1]
    scores.append(jnp.where((kcol <= qi) & (kcol < tn), s_new, -MASK_BIG))
    probs, den = _softmax_pieces(scores)
    acc = _dot(probs[n_pages].astype(BF16), new_kv[:, KV_A:2 * KV_A])
    for j in range(n_pages):
        acc = acc + _dot_nt(probs[j].astype(BF16), pages[j][0, 0, KV_A:2 * KV_A, :].astype(BF16))
    o_s = pick(acc) / den

    wb = state_ref.shape[3]
    st = state_ref[0, 0]
    wnew = wnew_ref[...]
    wn_b = pad_keys(wnew).astype(BF16)
    kpos = past - wb + lax.broadcasted_iota(jnp.int32, (1, wb), 1)
    dpos = tok - kpos
    s1 = jnp.where((dpos >= 0) & (dpos < WINDOW) & (kpos >= 0), _dot(q64, st[0:KV_A].astype(BF16)), NEG_BIG)
    kcol = lax.broadcasted_iota(jnp.int32, (1, 2 * tn), 1)
    s2 = jnp.where((kcol <= qi) & (kcol < tn), _dot_nt(q64, wn_b[:, 0:KV_A]), NEG_BIG)
    (p1, p2), den_w = _softmax_pieces([s1, s2])
    o_w = pick(_dot_nt(p1.astype(BF16), st[KV_A:2 * KV_A].astype(BF16))
               + _dot(p2.astype(BF16), wn_b[:, KV_A:2 * KV_A])) / den_w

    new_t = jnp.concatenate([jnp.zeros((LANES - tn, 2 * KV_A), F32), wnew], axis=0).T
    shifted = pltpu.roll(st, wb - tn, 1)
    wout_ref[0, 0, :, 0:wb - LANES] = shifted[:, 0:wb - LANES]
    lane_w = lax.broadcasted_iota(jnp.int32, (2 * KV_A, LANES), 1)
    wout_ref[0, 0, :, wb - LANES:wb] = jnp.where(lane_w >= LANES - tn, new_t, shifted[:, wb - LANES:wb])

    gates = misc_ref[...]
    pieces = []
    for hh in range(NSA_HEADS):
        hr = slice(hh * tn, (hh + 1) * tn)
        c0 = GATE_LANE0 + hh * 3
        pieces.append(gates[:, c0:c0 + 1] * o_c[hr] + gates[:, c0 + 1:c0 + 2] * o_s[hr]
                      + gates[:, c0 + 2:c0 + 3] * o_w[hr])
    o_ref[...] = jnp.concatenate(pieces, axis=1)


def _nsa_sample(page_table, layer, qa_s, misc_s, nsa_s, win_s, state_t, pair, cache_t, cmp_pool3, oa_full, n_prompt,
                win_all):
    nseq, n_pages = page_table.shape
    depth = state_t.shape[0]
    tn = qa_s.shape[0] // nseq
    past = n_pages * PAGE
    wb = state_t.shape[3]
    ns = SEQ_PER_STEP
    assert nseq % ns == 0 and n_prompt % (ns * tn) == 0
    carried = (oa_full,) if win_all is None else (oa_full, win_all)
    n_carried = len(carried)
    row = lambda width: pl.BlockSpec((ns * tn, width), lambda b, pt: (b, 0))
    page_specs = [pl.BlockSpec((1, 1, 2 * KV_A, PAGE),
                               functools.partial(lambda b, pt, s, j: (layer, pt[ns * b + s, j], 1, 0), s=s, j=j))
                  for s in range(ns) for j in range(n_pages)]
    cmp_specs = [pl.BlockSpec((1, PAGE // L_CMP, 2 * KV_A),
                              functools.partial(lambda b, pt, s, j: (pt[ns * b + s, j], 0, 0), s=s, j=j))
                 for s in range(ns) for j in range(n_pages)]
    grid_spec = pltpu.PrefetchScalarGridSpec(
        num_scalar_prefetch=1,
        grid=(nseq // ns,),
        in_specs=[row(2 * WIDTH_A), row(LANES), row(4 * KV_A), row(2 * KV_A),
                  pl.BlockSpec((1, ns, 2 * KV_A, wb), lambda b, pt: (layer, b, 0, 0)),
                  pl.BlockSpec(pair.shape, lambda b, pt: (0, 0))] + page_specs + cmp_specs
                 + [pl.BlockSpec(memory_space=pl.ANY)] * n_carried,
        out_specs=[pl.BlockSpec((ns * tn, WIDTH_A), lambda b, pt: (n_prompt // (ns * tn) + b, 0)),
                   pl.BlockSpec((1, ns, 2 * KV_A, wb), lambda b, pt: (layer, b, 0, 0))],
    )
    n_paged = ns * n_pages

    def body(pt_ref, q_ref, misc_ref, new_ref, wnew_ref, state_ref, pair_ref, *rest):
        pages, cmps = rest[:n_paged], rest[n_paged:2 * n_paged]
        o_ref, wout_ref = rest[2 * n_paged + n_carried:]
        for s in range(ns):
            rs, one = pl.ds(s * tn, tn), pl.ds(s, 1)
            _nsa_sample_kernel(pt_ref, q_ref.at[rs], misc_ref.at[rs], new_ref.at[rs], wnew_ref.at[rs],
                               state_ref.at[:, one], pair_ref, *pages[s * n_pages:(s + 1) * n_pages],
                               *cmps[s * n_pages:(s + 1) * n_pages], o_ref.at[rs], wout_ref.at[:, one],
                               n_pages=n_pages, past=past)

    n_in = 1 + 6 + 2 * n_paged
    aliases = {n_in: 0}
    if win_all is not None:
        aliases[n_in + 1] = 1
    return pl.pallas_call(
        body,
        grid_spec=grid_spec,
        out_shape=[jax.ShapeDtypeStruct(oa_full.shape, F32),
                   jax.ShapeDtypeStruct((depth, nseq, 2 * KV_A, wb), F32)],
        input_output_aliases=aliases,
        compiler_params=_cparams(("arbitrary",)),
        name="nsa_sample",
    )(page_table, qa_s, misc_s, nsa_s, win_s, state_t, pair, *([cache_t] * n_paged), *([cmp_pool3] * n_paged),
      *carried)


def _fox_sample_kernel(pt_ref, q_ref, new_ref, lfn_ref, ustrict_ref, *rest, n_pages):
    del pt_ref
    pages = rest[:n_pages]
    lfps = rest[n_pages:2 * n_pages]
    o_ref = rest[2 * n_pages]
    tn = q_ref.shape[0]
    rows = FOX_HEADS * tn
    qi = lax.broadcasted_iota(jnp.int32, (rows, 1), 0) % tn

    q = q_ref[...]
    zeros_half = jnp.zeros((tn, LANES), F32)
    blocks = []
    for h in range(FOX_HEADS):
        c = q[:, h * LANES:(h + 1) * LANES]
        if h % 2 == 1:
            c = pltpu.roll(c, HEAD_DIM, 1)
        blocks.append(jnp.concatenate([c, zeros_half] if h < 2 else [zeros_half, c], axis=1))
    q32 = jnp.concatenate(blocks, axis=0).astype(BF16)

    lf = jnp.concatenate([r[0, 0] for r in lfps], axis=0)
    within = jnp.zeros(lf.shape, F32)
    for part in _split3(lf):
        within = within + _dot(part, ustrict_ref[...])
    tot = jnp.sum(lf, axis=-1, keepdims=True)
    after = jnp.zeros((FOX_HEADS, 1), F32)
    page_bias = [None] * n_pages
    for j in reversed(range(n_pages)):
        b4 = within[j * FOX_HEADS:(j + 1) * FOX_HEADS] + after
        page_bias[j] = jnp.concatenate(
            [jnp.broadcast_to(b4[h:h + 1], (tn, PAGE)) for h in range(FOX_HEADS)], axis=0)
        after = after + tot[j * FOX_HEADS:(j + 1) * FOX_HEADS]

    lfn = lfn_ref[0]
    run = lfn[:, 0:1]
    cols = [run]
    for r in range(1, tn):
        run = run + lfn[:, r:r + 1]
        cols.append(run)
    cum = jnp.concatenate(cols + [jnp.zeros((FOX_HEADS, tn), F32)], axis=1)
    new_bias = jnp.concatenate(
        [jnp.broadcast_to(-cum[h:h + 1], (tn, 2 * tn)) for h in range(FOX_HEADS)], axis=0)

    new = jnp.concatenate([new_ref[...], jnp.zeros((tn, 2 * WIDTH_B), F32)], axis=0).astype(BF16)
    scores = [_dot(q32, pages[j][0, 0, 0:WIDTH_B, :].astype(BF16)) + page_bias[j] for j in range(n_pages)]
    kcol = lax.broadcasted_iota(jnp.int32, (1, 2 * tn), 1)
    scores.append(jnp.where((kcol <= qi) & (kcol < tn), _dot_nt(q32, new[:, :WIDTH_B]) + new_bias, -MASK_BIG))
    probs, den = _softmax_pieces(scores)
    acc = _dot(probs[n_pages].astype(BF16), new[:, WIDTH_B:])
    for j in range(n_pages):
        acc = acc + _dot_nt(probs[j].astype(BF16), pages[j][0, 0, WIDTH_B:2 * WIDTH_B, :].astype(BF16))
    o32 = acc / den
    lane_head = lax.broadcasted_iota(jnp.int32, (tn, WIDTH_B), 1) // HEAD_DIM
    out = jnp.zeros((tn, WIDTH_B), F32)
    for h in range(FOX_HEADS):
        out = out + jnp.where(lane_head == h, o32[h * tn:(h + 1) * tn], 0.0)
    o_ref[...] = out


def _fox_sample(page_table, layer, qb_s, fox_s, lfn_t, ustrict, cache_t, logf_t, ob_full, n_prompt):
    nseq, n_pages = page_table.shape
    tn = qb_s.shape[0] // nseq
    ns = SEQ_PER_STEP
    assert nseq % ns == 0 and n_prompt % (ns * tn) == 0
    row = lambda width: pl.BlockSpec((ns * tn, width), lambda b, pt: (b, 0))
    page_specs = [pl.BlockSpec((1, 1, 2 * WIDTH_B, PAGE),
                               functools.partial(lambda b, pt, s, j: (layer, pt[ns * b + s, j], 0, 0), s=s, j=j))
                  for s in range(ns) for j in range(n_pages)]
    lf_specs = [pl.BlockSpec((1, 1, FOX_HEADS, PAGE),
                             functools.partial(lambda b, pt, s, j: (layer, pt[ns * b + s, j], 0, 0), s=s, j=j))
                for s in range(ns) for j in range(n_pages)]
    grid_spec = pltpu.PrefetchScalarGridSpec(
        num_scalar_prefetch=1,
        grid=(nseq // ns,),
        in_specs=[row(2 * WIDTH_B), row(2 * WIDTH_B),
                  pl.BlockSpec((ns, FOX_HEADS, tn), lambda b, pt: (b, 0, 0)),
                  pl.BlockSpec(ustrict.shape, lambda b, pt: (0, 0))] + page_specs + lf_specs
                 + [pl.BlockSpec(memory_space=pl.ANY)],
        out_specs=pl.BlockSpec((ns * tn, WIDTH_B), lambda b, pt: (n_prompt // (ns * tn) + b, 0)),
    )
    n_paged = ns * n_pages

    def body(pt_ref, q_ref, new_ref, lfn_ref, ustrict_ref, *rest):
        pages, lfps = rest[:n_paged], rest[n_paged:2 * n_paged]
        o_ref = rest[2 * n_paged + 1]
        for s in range(ns):
            rs = pl.ds(s * tn, tn)
            _fox_sample_kernel(pt_ref, q_ref.at[rs], new_ref.at[rs], lfn_ref.at[pl.ds(s, 1)], ustrict_ref,
                               *pages[s * n_pages:(s + 1) * n_pages], *lfps[s * n_pages:(s + 1) * n_pages],
                               o_ref.at[rs], n_pages=n_pages)

    n_in = 1 + 4 + 2 * n_paged
    return pl.pallas_call(
        body,
        grid_spec=grid_spec,
        out_shape=jax.ShapeDtypeStruct(ob_full.shape, F32),
        input_output_aliases={n_in: 0},
        compiler_params=_cparams(("arbitrary",)),
        name="fox_sample",
    )(page_table, qb_s, fox_s, lfn_t, ustrict, *([cache_t] * n_paged), *([logf_t] * n_paged), ob_full)


def _in_offsets():
    splits = (WIDTH_A, KV_A, KV_A, KV_A, KV_A, KV_A, KV_A, 3 * NSA_HEADS, WIDTH_B, WIDTH_B, WIDTH_B, FOX_HEADS,
              GM_WIDTH, GM_WIDTH)
    offs = np.concatenate([[0], np.cumsum(splits)])
    names = ("qa", "kc", "vc", "ks", "vs", "kw", "vw", "ga", "qb", "kb", "vb", "fl", "gu", "gv", "mg")
    return {n: int(o) for n, o in zip(names, offs)}


def _pack_params(w_in, qk_gain, gm_norm, b_forget, d_model):
    depth = w_in.shape[0]
    o = _in_offsets()
    cols = np.concatenate([
        np.arange(o["qa"], o["qa"] + WIDTH_A),
        np.arange(o["qb"], o["qb"] + WIDTH_B),
        np.arange(o["gu"], o["gu"] + 2 * GM_WIDTH),
        np.arange(o["fl"], o["fl"] + FOX_HEADS),
        np.arange(o["ga"], o["ga"] + 3 * NSA_HEADS)])
    w_p = jnp.take(w_in, jnp.asarray(cols), axis=2)
    w_p = jnp.pad(w_p, ((0, 0), (0, 0), (0, P_COLS - w_p.shape[2]))).astype(BF16)
    rows_t = np.concatenate([
        np.arange(o["kc"], o["kc"] + 6 * KV_A),
        np.arange(o["kb"], o["kb"] + 2 * WIDTH_B),
        np.arange(o["fl"], o["fl"] + FOX_HEADS)])
    w_t = jnp.swapaxes(jnp.take(w_in, jnp.asarray(rows_t), axis=2), 1, 2)
    w_t = jnp.pad(w_t, ((0, 0), (0, T_ROWS - w_t.shape[1]), (0, 0))).astype(BF16)
    w_mg = w_in[:, :, o["mg"]:o["mg"] + 3 * d_model].astype(BF16)
    tile = lambda g, n: jnp.tile(g, (1, n))
    prm = jnp.concatenate([
        tile(qk_gain[:, 0], NSA_HEADS), tile(qk_gain[:, 4], FOX_HEADS), gm_norm,
        b_forget, jnp.zeros((depth, LANES - FOX_HEADS), F32)], axis=1)
    assert prm.shape[1] == R_COLS
    ones = jnp.ones((depth, KV_A), F32)
    col = jnp.concatenate([
        tile(qk_gain[:, 1], NSA_GROUPS), ones, tile(qk_gain[:, 2], NSA_GROUPS), ones,
        tile(qk_gain[:, 3], NSA_GROUPS), ones, tile(qk_gain[:, 5], FOX_HEADS), jnp.ones((depth, WIDTH_B), F32),
        b_forget, jnp.zeros((depth, SUBLANES - FOX_HEADS), F32)], axis=1)
    assert col.shape[1] == T_ROWS
    return w_p, w_t, w_mg, prm[:, None, :], col[:, :, None]


def _rope_tables(pos):
    half = ROT_HALF
    inv = ROPE_THETA ** (-jnp.arange(half, dtype=F32) / half)
    ang = pos.astype(F32)[:, None] * inv[None, :]
    cos, sin = jnp.cos(ang), jnp.sin(ang)
    n = pos.shape[0]
    zero = jnp.zeros((n, HEAD_DIM - ROT_DIM), F32)
    zero8 = jnp.zeros((n, half), F32)
    c64 = jnp.concatenate([cos, cos, zero + 1.0], axis=1)
    lo64 = jnp.concatenate([-sin, zero8, zero], axis=1)
    hi64 = jnp.concatenate([zero8, sin, zero], axis=1)
    rep = lambda t: jnp.tile(t, (1, LANES // HEAD_DIM))
    token_major = jnp.concatenate([rep(c64), rep(lo64), rep(hi64)], axis=1)
    feature_major = jnp.concatenate([cos.T, sin.T], axis=0)
    return token_major, feature_major


def _gmlp_tables(tn, w_spatial, b_spatial):
    w_p = jnp.tril(w_spatial)
    eye = jnp.asarray(np.eye(CHUNK // tn), F32)
    w_s = jnp.einsum("ab,lgts->lgatbs", eye, jnp.tril(w_spatial[:, :, :tn, :tn]))
    w_s = w_s.reshape(w_spatial.shape[0], GM_GROUPS, CHUNK, CHUNK)
    wmix = jnp.stack([w_p, w_s], axis=1).astype(BF16)
    b_p = jnp.repeat(jnp.swapaxes(b_spatial, 1, 2), GM_DIM, axis=2)
    b_s = jnp.tile(b_p[:, :tn], (1, CHUNK // tn, 1))
    btab = jnp.stack([b_p, b_s], axis=1)
    return wmix, btab


def _compress_weights(w_cmp, pe_cmp):
    depth = w_cmp.shape[0]
    eye_g = jnp.asarray(np.eye(NSA_GROUPS), F32)
    w_l = jnp.einsum("kq,gh,zklde->zlkgdqhe", jnp.asarray(np.eye(2), F32), eye_g, w_cmp)
    w_l = w_l.reshape(depth, L_CMP, 2 * KV_A, 2 * KV_A)
    pe_l = jnp.broadcast_to(pe_cmp[:, :, :, None, :], (depth, 2, L_CMP, NSA_GROUPS, HEAD_DIM))
    pe_l = jnp.transpose(pe_l, (0, 2, 1, 3, 4)).reshape(depth, L_CMP, 2 * KV_A)
    return w_l.astype(BF16), pe_l


def _group_major(x):
    s = x.shape[:-1]
    return jnp.swapaxes(x.reshape(s + (2, NSA_GROUPS, HEAD_DIM)), -3, -2).reshape(s + (2 * KV_A,))


def _feature_major(x, n_feat):
    nd = x.ndim
    perm = tuple(range(nd - 4)) + (nd - 3, nd - 2, nd - 1, nd - 4)
    y = jnp.transpose(x, perm)
    return y.reshape(y.shape[:nd - 4] + (n_feat, y.shape[-1]))


def _token_major_view(y, dims):
    lead = y.shape[:-2]
    z = y.reshape(lead + tuple(dims) + (y.shape[-1],))
    nd = z.ndim
    perm = tuple(range(nd - 4)) + (nd - 1, nd - 4, nd - 3, nd - 2)
    return jnp.transpose(z, perm)


def kernel(x_prompt, x_sample, cache_nsa_kv, cache_fox_kv, cache_fox_logf, state_nsa_win, page_table,
           g_ffn_a, w_ffn_a_gu, w_ffn_a_down, g_mix, w_in, b_forget, qk_gain, w_cmp, pe_cmp,
           gm_norm, w_spatial, b_spatial, w_branch_a, w_branch_b, w_branch_c, w_out,
           g_ffn_b, w_ffn_b_gu, w_ffn_b_down):
    batch, seq, d_model = x_prompt.shape
    nseq, tn, _ = x_sample.shape
    depth = w_in.shape[0]
    n_pool = cache_nsa_kv.shape[1]
    n_pages = page_table.shape[1]
    past = n_pages * PAGE
    wb = state_nsa_win.shape[2]
    n_prompt = batch * seq
    n_sample = nseq * tn
    tm = next(t for t in (512, 256, 128) if seq % t == 0 and n_sample % t == 0)
    assert seq % FOX_Q_BLOCK == 0 or seq < FOX_Q_BLOCK
    assert seq >= WINDOW + Q_BLOCK and wb % LANES == 0 and wb > LANES
    assert CHUNK % tn == 0 and past % L_SEL == 0 and wb >= tn

    x = jnp.concatenate([x_prompt.reshape(n_prompt, d_model), x_sample.reshape(n_sample, d_model)], axis=0)
    pos = jnp.concatenate([jnp.tile(jnp.arange(seq), batch), jnp.tile(past + jnp.arange(tn), nseq)])
    rope, rope_t = _rope_tables(pos)

    w_p, w_t, w_mg, prm, col = _pack_params(w_in, qk_gain, gm_norm, b_forget, d_model)
    bd = jnp.asarray(np.kron(np.eye(LANES // HEAD_DIM), np.full((HEAD_DIM, HEAD_DIM), 1.0 / HEAD_DIM)), BF16)
    wmix, btab = _gmlp_tables(tn, w_spatial, b_spatial)
    w_l, pe_l = _compress_weights(w_cmp, pe_cmp)
    w2 = _group_major(w_l.reshape(depth, L_CMP * 2 * KV_A, 2 * KV_A))
    pe2 = pe_l.reshape(depth, 1, L_CMP * 2 * KV_A)
    n_cmp_p = seq // L_CMP

    bf = lambda a: a.astype(BF16)
    wgu_a, wd_a, wgu_b, wd_b = bf(w_ffn_a_gu), bf(w_ffn_a_down), bf(w_ffn_b_gu), bf(w_ffn_b_down)
    wa, wb_, wc, wo = bf(w_branch_a), bf(w_branch_b), bf(w_branch_c), bf(w_out)

    cache_nsa_t = _feature_major(cache_nsa_kv, 4 * KV_A)
    cache_fox_t = _feature_major(cache_fox_kv, 2 * WIDTH_B)
    state_t = _feature_major(state_nsa_win, 2 * KV_A)
    logf_t = jnp.swapaxes(cache_fox_logf, 2, 3)
    nb = next(t for t in (64, 32, 16, 8, 4, 2, 1) if n_pool % t == 0)
    n_sel = past // L_SEL + 1
    pair = np.zeros((n_pages * (PAGE // L_CMP), LANES))
    for c in range(pair.shape[0]):
        pair[c, c // 2] = 1.0
    pair = jnp.asarray(pair, BF16)
    ustrict = jnp.asarray(np.triu(np.ones((PAGE, PAGE)), 1).T, BF16)
    assert n_sel <= LANES

    outs = {k: [] for k in ("nsas", "wint", "foxs", "logft", "logfs", "gmv")}
    stacked = None
    win_all = None
    for l in range(depth):
        x = _ffn(x, g_ffn_a[l][None], wgu_a[l], wd_a[l], tm)
        pr = _proj(x, g_mix[l][None], w_p[l], w_t[l], prm[l], col[l], rope, rope_t, bd, wmix[l], btab[l],
                   tm, batch, seq, n_sample, l, depth, stacked)
        stacked = (pr["nsat"], pr["foxt"])

        blocks = pr["kcvc"].reshape(batch, n_cmp_p // 2, 2, L_CMP * 2 * KV_A)
        x2 = jnp.swapaxes(blocks, 1, 2).reshape(batch, n_cmp_p, L_CMP * 2 * KV_A)
        kcv = _compress_prompt(x2, pe2[l], w2[l])
        oa = _nsa_prompt(pr["qa"], pr["misc"], kcv, pr["nsat"], pr["wint"], batch, seq, l)
        ob = _fox_prompt(pr["qb"], pr["foxt"], pr["faug"], batch, seq, l)

        cmp_pool = _compress_pool(cache_nsa_t, l, pe_l[l], w_l[l], nb)
        cmp_pool3 = cmp_pool.reshape(n_pool, PAGE // L_CMP, 2 * KV_A)
        sl = slice(n_prompt, None)
        oa, win_all = _nsa_sample(page_table, l, pr["qa"][sl].astype(F32), pr["misc"][sl], pr["nsas"],
                                  pr["wins"], state_t, pair, cache_nsa_t, cmp_pool3, oa, n_prompt, win_all)
        lfn_t = jnp.swapaxes(pr["misc"][sl, :FOX_HEADS].reshape(nseq, tn, FOX_HEADS), 1, 2)
        ob = _fox_sample(page_table, l, pr["qb"][sl].astype(F32), pr["foxs"], lfn_t, ustrict, cache_fox_t,
                         logf_t, ob, n_prompt)

        x = _mix(x, g_mix[l][None], w_mg[l], oa, ob, pr["oc"], wa[l], wb_[l], wc[l], wo[l], tm)
        x = _ffn(x, g_ffn_b[l][None], wgu_b[l], wd_b[l], tm)

        outs["nsas"].append(pr["nsas"])
        outs["wint"].append(pr["wint"][:, :, seq - min(WINDOW, seq):])
        outs["foxs"].append(pr["foxs"])
        outs["logft"].append(pr["logft"][:, :FOX_HEADS])
        outs["logfs"].append(pr["misc"][sl, :FOX_HEADS])
        outs["gmv"].append(pr["vs"])

    st = {k: jnp.stack(v) for k, v in outs.items()}
    st["nsat"], st["foxt"] = stacked
    st["win_s"] = win_all
    return (
        x[:n_prompt].reshape(batch, seq, d_model),
        x[n_prompt:].reshape(nseq, tn, d_model),
        _token_major_view(st["nsat"], (4, NSA_GROUPS, HEAD_DIM)),
        st["nsas"].reshape(depth, nseq, tn, 4, NSA_GROUPS, HEAD_DIM),
        _token_major_view(st["wint"], (2, NSA_GROUPS, HEAD_DIM)),
        _token_major_view(st["win_s"], (2, NSA_GROUPS, HEAD_DIM)),
        _token_major_view(st["foxt"], (2, FOX_HEADS, HEAD_DIM)),
        st["foxs"].reshape(depth, nseq, tn, 2, FOX_HEADS, HEAD_DIM),
        jnp.swapaxes(st["logft"], 2, 3),
        st["logfs"].reshape(depth, nseq, tn, FOX_HEADS),
        st["gmv"].reshape(depth, nseq, tn, GM_WIDTH),
    )
```

```python
import functools

import numpy as np
import jax
import jax.numpy as jnp
from jax import lax
from jax.experimental import pallas as pl
from jax.experimental.pallas import tpu as pltpu

F32 = jnp.float32
BF16 = jnp.bfloat16

HEAD_DIM = 64
ROT_DIM = HEAD_DIM // 4
ROT_HALF = ROT_DIM // 2
ROPE_THETA = 500000.0
NSA_HEADS = 8
NSA_GROUPS = 2
NSA_HPG = NSA_HEADS // NSA_GROUPS
L_CMP = 32
L_SEL = 64
TOP_N = 16
WINDOW = 512
FOX_HEADS = 4
GM_GROUPS = 4
GM_DIM = 64
GM_WIDTH = GM_GROUPS * GM_DIM
CHUNK = 128
PAGE = 128
EPS = 1e-6
NEG_BIG = -1e30
WIDTH_A = NSA_HEADS * HEAD_DIM
WIDTH_B = FOX_HEADS * HEAD_DIM
KV_A = NSA_GROUPS * HEAD_DIM
QK_SCALE = HEAD_DIM ** -0.5

LANES = 128
SUBLANES = 8
Q_BLOCK = 256
KEY_TILE = 256
FOX_Q_BLOCK = 1024
FAUG_ROWS = 16
SEQ_PER_STEP = 4
CMP_PITCH = L_CMP + 4
MASK_BIG = float(2 ** 30)
N_FORCED = 3
VMEM_LIMIT = 56 * 1024 * 1024

P_QA = 0
P_QB = 512
P_GM = 768
P_MISC = 1280
P_COLS = 1408
T_KC, T_VC, T_KS, T_VS, T_KW, T_VW, T_KB, T_VB, T_FL, T_ROWS = 0, 128, 256, 384, 512, 640, 768, 1024, 1280, 1288
R_GQ, R_GQB, R_GMN, R_BF, R_COLS = 0, 512, 768, 1024, 1152
GATE_LANE0 = FOX_HEADS


def _dot(a, b):
    return jnp.dot(a, b, preferred_element_type=F32)


def _dot_nt(a, b):
    return lax.dot_general(a, b, (((1,), (1,)), ((), ())), preferred_element_type=F32)


def _split3(x):
    a = x.astype(BF16)
    r = x - a.astype(F32)
    b = r.astype(BF16)
    c = (r - b.astype(F32)).astype(BF16)
    return a, b, c


def _rms_rows(x, g):
    return x * lax.rsqrt(jnp.mean(x * x, axis=-1, keepdims=True) + EPS) * g


def _log_sigmoid(z):
    return jnp.minimum(z, 0.0) - jnp.log(1.0 + jnp.exp(-jnp.abs(z)))


def _cparams(sem):
    return pltpu.CompilerParams(dimension_semantics=sem, vmem_limit_bytes=VMEM_LIMIT)


def _ffn_kernel(x_ref, g_ref, wgu_ref, wd_ref, o_ref, acc_ref, *, d_ff, chunk):
    x = x_ref[...]
    h = _rms_rows(x, g_ref[...]).astype(BF16)
    acc_ref[...] = jnp.zeros_like(acc_ref)
    for c in range(d_ff // chunk):
        g = _dot(h, wgu_ref[:, c * chunk:(c + 1) * chunk])
        u = _dot(h, wgu_ref[:, d_ff + c * chunk:d_ff + (c + 1) * chunk])
        a = (jax.nn.silu(g) * u).astype(BF16)
        acc_ref[...] += _dot(a, wd_ref[c * chunk:(c + 1) * chunk, :])
    o_ref[...] = x + 0.5 * acc_ref[...]


def _ffn(x, g, wgu, wd, tm):
    m, d = x.shape
    d_ff = wd.shape[0]
    return pl.pallas_call(
        functools.partial(_ffn_kernel, d_ff=d_ff, chunk=256),
        grid=(m // tm,),
        in_specs=[pl.BlockSpec((tm, d), lambda i: (i, 0)),
                  pl.BlockSpec((1, d), lambda i: (0, 0)),
                  pl.BlockSpec((d, 2 * d_ff), lambda i: (0, 0)),
                  pl.BlockSpec((d_ff, d), lambda i: (0, 0))],
        out_specs=pl.BlockSpec((tm, d), lambda i: (i, 0)),
        out_shape=jax.ShapeDtypeStruct((m, d), F32),
        scratch_shapes=[pltpu.VMEM((tm, d), F32)],
        compiler_params=_cparams(("parallel",)),
        name="ffn",
    )(x, g, wgu, wd)


def _proj_kernel(x_ref, gmix_ref, w_ref, wt_ref, prm_ref, col_ref, rope_ref, ropet_ref, bd_ref, wmix_ref, btab_ref,
                 qa_ref, qb_ref, misc_ref, oc_ref, kcvc_ref, nsat_ref, wint_ref, foxt_ref, faug_ref, logft_ref,
                 nsas_ref, wins_ref, foxs_ref, vs_ref, v_scr, carry_ref, *, tm, tiles_per_seq, n_prompt_tiles):
    i = pl.program_id(0)
    x = x_ref[...]
    h = _rms_rows(x, gmix_ref[...]).astype(BF16)
    lane = lax.broadcasted_iota(jnp.int32, (tm, LANES), 1)
    lo64 = lane < HEAD_DIM

    cos = rope_ref[:, 0:128]
    sin_lo = rope_ref[:, 128:256]
    sin_hi = rope_ref[:, 256:384]
    bd = bd_ref[...]

    def seg(a, b):
        return _dot(h, w_ref[:, a:b])

    def headnorm(t, gain):
        hi, lo, _ = _split3(t * t)
        ms = _dot(hi, bd) + _dot(lo, bd)
        return t * lax.rsqrt(ms + EPS) * gain

    def rope(t):
        return t * cos + pltpu.roll(t, LANES - ROT_HALF, 1) * sin_lo + pltpu.roll(t, ROT_HALF, 1) * sin_hi

    def head_split(t):
        return (jnp.where(lo64, t, 0.0).astype(BF16),
                jnp.where(lo64, pltpu.roll(t, HEAD_DIM, 1), 0.0).astype(BF16))

    for c in range(WIDTH_A // LANES):
        t = seg(P_QA + c * LANES, P_QA + (c + 1) * LANES)
        t = rope(headnorm(t, prm_ref[:, R_GQ + c * LANES:R_GQ + (c + 1) * LANES])) * QK_SCALE
        a, b = head_split(t)
        qa_ref[:, (2 * c) * LANES:(2 * c + 1) * LANES] = a
        qa_ref[:, (2 * c + 1) * LANES:(2 * c + 2) * LANES] = b
    for c in range(WIDTH_B // LANES):
        t = seg(P_QB + c * LANES, P_QB + (c + 1) * LANES)
        t = headnorm(t, prm_ref[:, R_GQB + c * LANES:R_GQB + (c + 1) * LANES]) * QK_SCALE
        a, b = head_split(t)
        qb_ref[:, (2 * c) * LANES:(2 * c + 1) * LANES] = a
        qb_ref[:, (2 * c + 1) * LANES:(2 * c + 2) * LANES] = b

    lane_grp = lax.broadcasted_iota(jnp.int32, (CHUNK, GM_WIDTH), 1) // GM_DIM
    for c in range(GM_WIDTH // LANES):
        gv = jax.nn.gelu(seg(P_GM + GM_WIDTH + c * LANES, P_GM + GM_WIDTH + (c + 1) * LANES))
        v_scr[:, c * LANES:(c + 1) * LANES] = headnorm(gv, prm_ref[:, R_GMN + c * LANES:R_GMN + (c + 1) * LANES])
    for r in range(tm // CHUNK):
        rows = slice(r * CHUNK, (r + 1) * CHUNK)
        vsub = v_scr[rows, :]
        s = btab_ref[0]
        for g in range(GM_GROUPS):
            s = s + _dot(wmix_ref[0, g], jnp.where(lane_grp == g, vsub, 0.0).astype(BF16))
        u = jnp.concatenate(
            [jax.nn.gelu(_dot(h[rows, :], w_ref[:, P_GM + c * LANES:P_GM + (c + 1) * LANES]))
             for c in range(GM_WIDTH // LANES)], axis=1)
        oc_ref[rows, :] = (u * s).astype(BF16)

    t = seg(P_MISC, P_MISC + LANES)
    logf = _log_sigmoid(t + prm_ref[:, R_BF:R_BF + LANES])
    misc_ref[...] = jnp.where(lane < FOX_HEADS, logf, jax.nn.sigmoid(t))

    cos_t = ropet_ref[0:ROT_HALF, :]
    sin_t = ropet_ref[ROT_HALF:ROT_DIM, :]

    def seg_t(a, b):
        return _dot_nt(wt_ref[a:b, :], h)

    def headnorm_t(t, row0, rot):
        outs = []
        for hh in range(t.shape[0] // HEAD_DIM):
            blk = t[hh * HEAD_DIM:(hh + 1) * HEAD_DIM]
            ms = jnp.mean(blk * blk, axis=0, keepdims=True)
            n = blk * lax.rsqrt(ms + EPS) * col_ref[row0 + hh * HEAD_DIM:row0 + (hh + 1) * HEAD_DIM, :]
            if rot:
                x1, x2 = n[0:ROT_HALF], n[ROT_HALF:ROT_DIM]
                n = jnp.concatenate([x1 * cos_t - x2 * sin_t, x2 * cos_t + x1 * sin_t, n[ROT_DIM:]], axis=0)
            outs.append(n)
        return jnp.concatenate(outs, axis=0)

    kc = headnorm_t(seg_t(T_KC, T_KC + KV_A), T_KC, True)
    vc = seg_t(T_VC, T_VC + KV_A)
    ks = headnorm_t(seg_t(T_KS, T_KS + KV_A), T_KS, True)
    vs = seg_t(T_VS, T_VS + KV_A)
    kw = headnorm_t(seg_t(T_KW, T_KW + KV_A), T_KW, True)
    vw = seg_t(T_VW, T_VW + KV_A)
    kb = headnorm_t(seg_t(T_KB, T_KB + WIDTH_B), T_KB, False)
    vb = seg_t(T_VB, T_VB + WIDTH_B)

    @pl.when(i % tiles_per_seq == 0)
    def _():
        carry_ref[...] = jnp.zeros_like(carry_ref)

    zf = seg_t(T_FL, T_FL + SUBLANES) + col_ref[T_FL:T_FL + SUBLANES, :]
    sub = lax.broadcasted_iota(jnp.int32, (SUBLANES, tm), 0)
    lane_t = lax.broadcasted_iota(jnp.int32, (SUBLANES, tm), 1)
    logf_t = jnp.where(sub < FOX_HEADS, _log_sigmoid(zf), 0.0)
    cum = logf_t
    shift = 1
    while shift < tm:
        cum = cum + jnp.where(lane_t >= shift, pltpu.roll(cum, shift, 1), 0.0)
        shift *= 2
    cum = cum + carry_ref[:, 0:1]
    carry_ref[...] = jnp.broadcast_to(cum[:, tm - 1:tm], carry_ref.shape)
    hi, mid, lo = [p.astype(F32) for p in _split3(-cum)]
    sub16 = lax.broadcasted_iota(jnp.int32, (FAUG_ROWS, tm), 0)

    @pl.when(i < n_prompt_tiles)
    def _():
        nsat_ref[0, 0:128, :] = kc
        nsat_ref[0, 128:256, :] = vc
        nsat_ref[0, 256:384, :] = ks
        nsat_ref[0, 384:512, :] = vs
        wint_ref[0, 0:128, :] = kw
        wint_ref[0, 128:256, :] = vw
        foxt_ref[0, 0:WIDTH_B, :] = kb
        foxt_ref[0, WIDTH_B:2 * WIDTH_B, :] = vb
        logft_ref[0] = logf_t
        for hh in range(FOX_HEADS):
            blk = jnp.where(sub16 == 0, hi[hh:hh + 1],
                            jnp.where(sub16 == 1, mid[hh:hh + 1], jnp.where(sub16 == 2, lo[hh:hh + 1], 0.0)))
            faug_ref[0, hh * FAUG_ROWS:(hh + 1) * FAUG_ROWS, :] = blk.astype(BF16)
        kcvc_ref[:, 0:128] = kc.T
        kcvc_ref[:, 128:256] = vc.T

    @pl.when(i >= n_prompt_tiles)
    def _():
        for j, t_ in enumerate((kc, vc, ks, vs)):
            nsas_ref[:, j * KV_A:(j + 1) * KV_A] = t_.T
        wins_ref[:, 0:128] = kw.T
        wins_ref[:, 128:256] = vw.T
        for c in range(WIDTH_B // LANES):
            foxs_ref[:, c * LANES:(c + 1) * LANES] = kb[c * LANES:(c + 1) * LANES].T
            foxs_ref[:, WIDTH_B + c * LANES:WIDTH_B + (c + 1) * LANES] = vb[c * LANES:(c + 1) * LANES].T
        vs_ref[...] = v_scr[...]


def _proj(x, gmix, w, wt, prm, col, rope, ropet, bd, wmix, btab, tm, batch, seq, n_sample, layer, depth, stacked):
    m, d = x.shape
    tps = seq // tm
    npt = batch * tps
    row = lambda width: pl.BlockSpec((tm, width), lambda i: (i, 0))
    full = lambda a: pl.BlockSpec(a.shape, lambda i: (0,) * a.ndim)
    kind = lambda i: (i >= npt).astype(jnp.int32)

    def featmajor(rows):
        def idx(i):
            ii = jnp.minimum(i, npt - 1)
            return (ii // tps, 0, ii % tps)
        return pl.BlockSpec((1, rows, tm), idx)

    def featmajor_stacked(rows):
        def idx(i):
            ii = jnp.minimum(i, npt - 1)
            return (layer, ii // tps, 0, ii % tps)
        return pl.BlockSpec((None, 1, rows, tm), idx)

    prow = lambda width: pl.BlockSpec((tm, width), lambda i: (jnp.minimum(i, npt - 1), 0))
    srow = lambda width: pl.BlockSpec((tm, width), lambda i: (jnp.maximum(i - npt, 0), 0))
    n_prompt = batch * seq
    outs = [
        ("qa", row(2 * WIDTH_A), (m, 2 * WIDTH_A), BF16),
        ("qb", row(2 * WIDTH_B), (m, 2 * WIDTH_B), BF16),
        ("misc", row(LANES), (m, LANES), F32),
        ("oc", row(GM_WIDTH), (m, GM_WIDTH), BF16),
        ("kcvc", prow(2 * KV_A), (n_prompt, 2 * KV_A), F32),
        ("nsat", featmajor_stacked(4 * KV_A), (depth, batch, 4 * KV_A, seq), F32),
        ("wint", featmajor(2 * KV_A), (batch, 2 * KV_A, seq), F32),
        ("foxt", featmajor_stacked(2 * WIDTH_B), (depth, batch, 2 * WIDTH_B, seq), F32),
        ("faug", featmajor(FOX_HEADS * FAUG_ROWS), (batch, FOX_HEADS * FAUG_ROWS, seq), BF16),
        ("logft", featmajor(SUBLANES), (batch, SUBLANES, seq), F32),
        ("nsas", srow(4 * KV_A), (n_sample, 4 * KV_A), F32),
        ("wins", srow(2 * KV_A), (n_sample, 2 * KV_A), F32),
        ("foxs", srow(2 * WIDTH_B), (n_sample, 2 * WIDTH_B), F32),
        ("vs", srow(GM_WIDTH), (n_sample, GM_WIDTH), F32),
    ]
    names = [o[0] for o in outs]
    n_in = 11
    carried = () if stacked is None else tuple(stacked)
    aliases = {n_in + k: names.index(n) for k, n in enumerate(("nsat", "foxt")[:len(carried)])}
    kern = functools.partial(_proj_kernel, tm=tm, tiles_per_seq=tps, n_prompt_tiles=npt)

    def body(*refs):
        kern(*refs[:n_in], *refs[n_in + len(carried):])

    res = pl.pallas_call(
        body,
        grid=(m // tm,),
        in_specs=[row(d), full(gmix), full(w), full(wt), full(prm), full(col), row(3 * LANES),
                  pl.BlockSpec((ROT_DIM, tm), lambda i: (0, i)), full(bd),
                  pl.BlockSpec((1,) + wmix.shape[1:], lambda i: (kind(i), 0, 0, 0)),
                  pl.BlockSpec((1,) + btab.shape[1:], lambda i: (kind(i), 0, 0))]
                 + [pl.BlockSpec(memory_space=pl.ANY)] * len(carried),
        out_specs=[o[1] for o in outs],
        out_shape=[jax.ShapeDtypeStruct(o[2], o[3]) for o in outs],
        scratch_shapes=[pltpu.VMEM((tm, GM_WIDTH), F32), pltpu.VMEM((SUBLANES, LANES), F32)],
        input_output_aliases=aliases,
        compiler_params=_cparams(("arbitrary",)),
        name="proj",
    )(x, gmix, w, wt, prm, col, rope, ropet, bd, wmix, btab, *carried)
    return dict(zip(names, res))


def _mix_kernel(x_ref, gmix_ref, wmg_ref, oa_ref, ob_ref, oc_ref, wa_ref, wb_ref, wc_ref, wout_ref, o_ref):
    x = x_ref[...]
    d = x.shape[1]
    h = _rms_rows(x, gmix_ref[...]).astype(BF16)
    m = jax.nn.sigmoid(_dot(h, wmg_ref[:, 0:d])) * _dot(oa_ref[...].astype(BF16), wa_ref[...])
    m = m + jax.nn.sigmoid(_dot(h, wmg_ref[:, d:2 * d])) * _dot(ob_ref[...].astype(BF16), wb_ref[...])
    m = m + jax.nn.sigmoid(_dot(h, wmg_ref[:, 2 * d:3 * d])) * _dot(oc_ref[...], wc_ref[...])
    o_ref[...] = x + _dot(m.astype(BF16), wout_ref[...])


def _mix(x, gmix, wmg, oa, ob, oc, wa, wb, wc, wout, tm):
    m, d = x.shape
    row = lambda a: pl.BlockSpec((tm, a.shape[1]), lambda i: (i, 0))
    full = lambda a: pl.BlockSpec(a.shape, lambda i: (0,) * a.ndim)
    return pl.pallas_call(
        _mix_kernel,
        grid=(m // tm,),
        in_specs=[row(x), full(gmix), full(wmg), row(oa), row(ob), row(oc), full(wa), full(wb), full(wc),
                  full(wout)],
        out_specs=row(x),
        out_shape=jax.ShapeDtypeStruct((m, d), F32),
        compiler_params=_cparams(("parallel",)),
        name="mix",
    )(x, gmix, wmg, oa, ob, oc, wa, wb, wc, wout)


def _compress_kernel(x_ref, pe_ref, w_ref, o_ref):
    o_ref[0] = _dot((x_ref[0] + pe_ref[...]).astype(BF16), w_ref[...])


def _compress_prompt(x2, pe2, w2):
    b, nc, k = x2.shape
    n = w2.shape[1]
    return pl.pallas_call(
        _compress_kernel,
        grid=(b,),
        in_specs=[pl.BlockSpec((1, nc, k), lambda i: (i, 0, 0)),
                  pl.BlockSpec((1, k), lambda i: (0, 0)),
                  pl.BlockSpec((k, n), lambda i: (0, 0))],
        out_specs=pl.BlockSpec((1, nc, n), lambda i: (i, 0, 0)),
        out_shape=jax.ShapeDtypeStruct((b, nc, n), F32),
        compiler_params=_cparams(("parallel",)),
        name="compress_prompt",
    )(x2, pe2, w2)


def _online_update(s, vt_aug, acc_ref, m_ref):
    n_chunk = s.shape[1] // LANES
    m_old = m_ref[...]
    mx = s[:, 0:LANES]
    for c in range(1, n_chunk):
        mx = jnp.maximum(mx, s[:, c * LANES:(c + 1) * LANES])
    m_new = jnp.maximum(m_old, jnp.max(mx, axis=-1, keepdims=True))
    alpha = jnp.exp(m_old - m_new)
    p = jnp.concatenate(
        [jnp.exp(s[:, c * LANES:(c + 1) * LANES] - m_new).astype(BF16) for c in range(n_chunk)], axis=1)
    acc_ref[...] = alpha * acc_ref[...] + _dot_nt(p, vt_aug)
    m_ref[...] = m_new


def _finish(acc):
    return (acc / pltpu.roll(acc, HEAD_DIM, 1))[:, :HEAD_DIM]


def _select_blocks(imp_t, tok_row):
    nb, nt = imp_t.shape
    blk = lax.broadcasted_iota(jnp.int32, (nb, nt), 0)
    cur = jnp.broadcast_to(tok_row, (nb, nt)) // L_SEL
    forced = (blk == 0) | (blk == cur) | (blk == cur - 1)
    visible = blk <= cur
    score = jnp.where(visible & jnp.logical_not(forced), imp_t, -jnp.inf)
    sel = jnp.where(visible & forced, 1.0, 0.0)
    for _ in range(min(TOP_N - N_FORCED, nb)):
        mx = jnp.max(score, axis=0, keepdims=True)
        first = jnp.min(jnp.where(score == mx, blk, nb), axis=0, keepdims=True)
        hit = (blk == first) & (mx > -jnp.inf)
        sel = jnp.where(hit, 1.0, sel)
        score = jnp.where(hit, -jnp.inf, score)
    return sel


def _nsa_prompt_kernel(q_ref, misc_ref, kcv_ref, kv_ref, win_ref, o_ref, acc_ref, m_ref, *, n_cmp):
    i = pl.program_id(1)
    rows = NSA_HPG * Q_BLOCK
    tok0 = i * Q_BLOCK
    tok = tok0 + (lax.broadcasted_iota(jnp.int32, (rows, 1), 0) % Q_BLOCK)
    half = n_cmp // 2
    ones_rows = jnp.ones((HEAD_DIM, 1), BF16)
    zero_rows = jnp.zeros((HEAD_DIM, 1), BF16)

    q_los, q_augs, o_cs = [], [], []
    for g in range(NSA_GROUPS):
        gl = slice(g * LANES, (g + 1) * LANES)
        q_lo = jnp.concatenate(
            [q_ref[:, (g * NSA_HPG + h) * LANES:(g * NSA_HPG + h + 1) * LANES] for h in range(NSA_HPG)], axis=0)
        kcv = kcv_ref[0, :, gl].astype(BF16)
        s_c = _dot_nt(q_lo, kcv)
        col = lax.broadcasted_iota(jnp.int32, (1, n_cmp), 1)
        blk_c = 2 * (col % half) + col // half
        s_c = jnp.where((blk_c + 1) * L_CMP - 1 <= tok, s_c, NEG_BIG)
        e_c = jnp.exp(s_c - jnp.max(s_c, axis=-1, keepdims=True))
        p_c = e_c / jnp.sum(e_c, axis=-1, keepdims=True)
        p_c = jnp.where(tok >= L_CMP - 1, p_c, 0.0)
        o_cs.append(_dot(p_c.astype(BF16), kcv)[:, HEAD_DIM:])
        imp = p_c[0:Q_BLOCK]
        for h in range(1, NSA_HPG):
            imp = imp + p_c[h * Q_BLOCK:(h + 1) * Q_BLOCK]
        imp_sel = imp[:, :half] + imp[:, half:]
        if half < LANES:
            imp_sel = jnp.concatenate([imp_sel, jnp.zeros((Q_BLOCK, LANES - half), F32)], axis=1)
        sel_t = _select_blocks(imp_sel.T, tok0 + lax.broadcasted_iota(jnp.int32, (1, Q_BLOCK), 1))
        nsel = (1.0 - sel_t).T.astype(BF16)
        q_los.append(q_lo)
        q_augs.append(jnp.concatenate([q_lo, jnp.concatenate([nsel] * NSA_HPG, axis=0)], axis=1))

    acc_ref[...] = jnp.zeros_like(acc_ref)
    m_ref[...] = jnp.full_like(m_ref, -jnp.inf)
    n_kt = (tok0 + Q_BLOCK + KEY_TILE - 1) // KEY_TILE

    def sel_tile(kt, causal, width=KEY_TILE):
        key0 = pl.multiple_of(kt * KEY_TILE, KEY_TILE)
        ks = pl.ds(key0, width)
        crow = lax.broadcasted_iota(jnp.int32, (LANES, width), 0)
        kblk = (key0 + lax.broadcasted_iota(jnp.int32, (LANES, width), 1)) // L_SEL
        aug = jnp.where(crow == kblk, -MASK_BIG, 0.0).astype(BF16)
        for g in range(NSA_GROUPS):
            k_t = kv_ref[0, g * HEAD_DIM:(g + 1) * HEAD_DIM, ks].astype(BF16)
            v_t = kv_ref[0, KV_A + g * HEAD_DIM:KV_A + (g + 1) * HEAD_DIM, ks].astype(BF16)
            zeros = jnp.broadcast_to(zero_rows, (HEAD_DIM, width))
            s = _dot(q_augs[g], jnp.concatenate([k_t, zeros, aug], axis=0))
            if causal:
                keypos = key0 + lax.broadcasted_iota(jnp.int32, (1, width), 1)
                s = jnp.where(keypos <= tok, s, -MASK_BIG)
            ones = jnp.broadcast_to(ones_rows, (HEAD_DIM, width))
            _online_update(s, jnp.concatenate([v_t, ones], axis=0), acc_ref.at[g], m_ref.at[g])

    def sel_body(j, carry):
        sel_tile(2 * j, False, 2 * KEY_TILE)
        return carry

    n_pairs = (n_kt - 1) // 2
    lax.fori_loop(0, n_pairs, sel_body, 0)

    @pl.when(2 * n_pairs < n_kt - 1)
    def _():
        sel_tile(n_kt - 2, False)

    sel_tile(n_kt - 1, True)

    band = WINDOW + Q_BLOCK
    start = pl.multiple_of(jnp.maximum(tok0 - WINDOW, 0), Q_BLOCK)
    wk = pl.ds(start, band)
    dpos = tok - (start + lax.broadcasted_iota(jnp.int32, (1, band), 1))
    ok = (dpos >= 0) & (dpos < WINDOW)
    gates = misc_ref[...]
    for g in range(NSA_GROUPS):
        k_t = win_ref[0, g * HEAD_DIM:(g + 1) * HEAD_DIM, wk].astype(BF16)
        v_t = win_ref[0, KV_A + g * HEAD_DIM:KV_A + (g + 1) * HEAD_DIM, wk].astype(BF16)
        zeros = jnp.broadcast_to(zero_rows, (HEAD_DIM, band))
        s = jnp.where(ok, _dot(q_los[g], jnp.concatenate([k_t, zeros], axis=0)), NEG_BIG)
        p = jnp.exp(s - jnp.max(s, axis=-1, keepdims=True)).astype(BF16)
        ones = jnp.broadcast_to(ones_rows, (HEAD_DIM, band))
        o_w = _finish(_dot_nt(p, jnp.concatenate([v_t, ones], axis=0)))
        o_s = _finish(acc_ref[g])
        o_c = o_cs[g]
        pieces = []
        for h in range(NSA_HPG):
            hr = slice(h * Q_BLOCK, (h + 1) * Q_BLOCK)
            c0 = GATE_LANE0 + (g * NSA_HPG + h) * 3
            pieces.append(gates[:, c0:c0 + 1] * o_c[hr] + gates[:, c0 + 1:c0 + 2] * o_s[hr]
                          + gates[:, c0 + 2:c0 + 3] * o_w[hr])
        o_ref[:, g * NSA_HPG * HEAD_DIM:(g + 1) * NSA_HPG * HEAD_DIM] = jnp.concatenate(pieces, axis=1)


def _nsa_prompt(qa, misc, kcv, nsat, wint, batch, seq, layer):
    m = qa.shape[0]
    nq = seq // Q_BLOCK
    n_cmp = kcv.shape[1]
    rows = NSA_HPG * Q_BLOCK
    return pl.pallas_call(
        functools.partial(_nsa_prompt_kernel, n_cmp=n_cmp),
        grid=(batch, nq),
        in_specs=[pl.BlockSpec((Q_BLOCK, 2 * WIDTH_A), lambda b, i: (b * nq + i, 0)),
                  pl.BlockSpec((Q_BLOCK, LANES), lambda b, i: (b * nq + i, 0)),
                  pl.BlockSpec((1, n_cmp, 2 * KV_A), lambda b, i: (b, 0, 0)),
                  pl.BlockSpec((None, 1, 2 * KV_A, seq), lambda b, i: (layer, b, 1, 0)),
                  pl.BlockSpec((1, 2 * KV_A, seq), lambda b, i: (b, 0, 0))],
        out_specs=pl.BlockSpec((Q_BLOCK, WIDTH_A), lambda b, i: (b * nq + i, 0)),
        out_shape=jax.ShapeDtypeStruct((m, WIDTH_A), F32),
        scratch_shapes=[pltpu.VMEM((NSA_GROUPS, rows, LANES), F32), pltpu.VMEM((NSA_GROUPS, rows, LANES), F32)],
        compiler_params=_cparams(("parallel", "arbitrary")),
        name="nsa_prompt",
    )(qa, misc, kcv, nsat, wint)


def _fox_prompt_kernel(q_ref, kv_ref, faug_ref, o_ref, acc_ref, m_ref, *, qb):
    i = pl.program_id(1)
    tok = i * qb + lax.broadcasted_iota(jnp.int32, (qb, 1), 0)
    lane = lax.broadcasted_iota(jnp.int32, (qb, LANES), 1)
    ones_lanes = jnp.where(lane < HEAD_DIM + 3, 1.0, 0.0).astype(BF16)
    q_augs = [jnp.where(lane < HEAD_DIM, q_ref[:, h * LANES:(h + 1) * LANES], ones_lanes) for h in range(FOX_HEADS)]
    pad_rows = LANES - HEAD_DIM - FAUG_ROWS
    acc_ref[...] = jnp.zeros_like(acc_ref)
    m_ref[...] = jnp.full_like(m_ref, -jnp.inf)

    def tile(kt, causal, width=KEY_TILE, row0=0):
        key0 = pl.multiple_of(kt * KEY_TILE, KEY_TILE)
        ks = pl.ds(key0, width)
        rs = pl.ds(row0, qb - row0)
        zeros = jnp.zeros((pad_rows, width), BF16)
        ones = jnp.ones((HEAD_DIM, width), BF16)
        for h in range(FOX_HEADS):
            k_t = kv_ref[0, h * HEAD_DIM:(h + 1) * HEAD_DIM, ks].astype(BF16)
            v_t = kv_ref[0, WIDTH_B + h * HEAD_DIM:WIDTH_B + (h + 1) * HEAD_DIM, ks].astype(BF16)
            fa = faug_ref[0, h * FAUG_ROWS:(h + 1) * FAUG_ROWS, ks]
            s = _dot(q_augs[h][row0:], jnp.concatenate([k_t, fa, zeros], axis=0))
            if causal:
                keypos = key0 + lax.broadcasted_iota(jnp.int32, (1, width), 1)
                s = jnp.where(keypos <= tok[row0:], s, -MASK_BIG)
            _online_update(s, jnp.concatenate([v_t, ones], axis=0), acc_ref.at[h, rs], m_ref.at[h, rs])

    def body(j, carry):
        tile(2 * j, False, 2 * KEY_TILE)
        return carry

    n_full = i * (qb // KEY_TILE)
    assert (qb // KEY_TILE) % 2 == 0
    lax.fori_loop(0, n_full // 2, body, 0)
    for d in range(qb // KEY_TILE):
        tile(n_full + d, True, KEY_TILE, d * KEY_TILE)
    for h in range(FOX_HEADS):
        o_ref[:, h * HEAD_DIM:(h + 1) * HEAD_DIM] = _finish(acc_ref[h])


def _fox_prompt(qb_arr, foxt, faug, batch, seq, layer):
    m = qb_arr.shape[0]
    qb = min(FOX_Q_BLOCK, seq)
    nq = seq // qb
    return pl.pallas_call(
        functools.partial(_fox_prompt_kernel, qb=qb),
        grid=(batch, nq),
        in_specs=[pl.BlockSpec((qb, 2 * WIDTH_B), lambda b, i: (b * nq + i, 0)),
                  pl.BlockSpec((None, 1, 2 * WIDTH_B, seq), lambda b, i: (layer, b, 0, 0)),
                  pl.BlockSpec((1, FOX_HEADS * FAUG_ROWS, seq), lambda b, i: (b, 0, 0))],
        out_specs=pl.BlockSpec((qb, WIDTH_B), lambda b, i: (b * nq + i, 0)),
        out_shape=jax.ShapeDtypeStruct((m, WIDTH_B), F32),
        scratch_shapes=[pltpu.VMEM((FOX_HEADS, qb, LANES), F32), pltpu.VMEM((FOX_HEADS, qb, LANES), F32)],
        compiler_params=_cparams(("parallel", "arbitrary")),
        name="fox_prompt",
    )(qb_arr, foxt, faug)


def _cmp_pool_kernel(x_ref, pe_ref, w_ref, o_ref, rows_ref, *, nb):
    per = PAGE // L_CMP

    def put(p, carry):
        base = pl.multiple_of(p * (per * CMP_PITCH), SUBLANES)
        for c in range(2):
            rows = x_ref[0, p, c * KV_A:(c + 1) * KV_A, :].T
            for n in range(per):
                rows_ref[c, pl.ds(base + n * CMP_PITCH, L_CMP), :] = rows[n * L_CMP:(n + 1) * L_CMP]
        return carry

    lax.fori_loop(0, nb, put, 0, unroll=2)
    n_blk = nb * per
    for c in range(2):
        cs = slice(c * KV_A, (c + 1) * KV_A)
        acc = jnp.zeros((n_blk, KV_A), F32)
        for l in range(L_CMP):
            xl = rows_ref[c, pl.ds(l, n_blk, stride=CMP_PITCH), :]
            acc = acc + _dot((xl + pe_ref[l:l + 1, cs]).astype(BF16), w_ref[l, cs, cs])
        o_ref[:, cs] = acc


def _compress_pool(cache_t, layer, pe_l, w_l, nb):
    n_pool = cache_t.shape[1]
    per = PAGE // L_CMP
    return pl.pallas_call(
        functools.partial(_cmp_pool_kernel, nb=nb),
        grid=(n_pool // nb,),
        in_specs=[pl.BlockSpec((1, nb, 2 * KV_A, PAGE), lambda i: (layer, i, 0, 0)),
                  pl.BlockSpec(pe_l.shape, lambda i: (0, 0)),
                  pl.BlockSpec(w_l.shape, lambda i: (0, 0, 0))],
        out_specs=pl.BlockSpec((nb * per, 2 * KV_A), lambda i: (i, 0)),
        out_shape=jax.ShapeDtypeStruct((n_pool * per, 2 * KV_A), F32),
        scratch_shapes=[pltpu.VMEM((2, nb * per * CMP_PITCH, KV_A), F32)],
        compiler_params=_cparams(("parallel",)),
        name="compress_pool",
    )(cache_t, pe_l, w_l)


def _softmax_pieces(pieces):
    mx = functools.reduce(jnp.maximum, [jnp.max(s, axis=-1, keepdims=True) for s in pieces])
    es = [jnp.exp(s - mx) for s in pieces]
    den = functools.reduce(lambda a, b: a + b, [jnp.sum(e, axis=-1, keepdims=True) for e in es])
    return es, den


def _nsa_sample_kernel(pt_ref, q_ref, misc_ref, new_ref, wnew_ref, state_ref, pair_ref, *rest, n_pages, past):
    del pt_ref
    pages = rest[:n_pages]
    cmps = rest[n_pages:2 * n_pages]
    o_ref, wout_ref = rest[2 * n_pages:]
    tn = q_ref.shape[0]
    rows = NSA_HEADS * tn
    rid = lax.broadcasted_iota(jnp.int32, (rows, 1), 0)
    qi = rid % tn
    tok = past + qi
    is_g1 = rid >= NSA_HPG * tn

    q = q_ref[...]
    blocks = []
    for g in range(NSA_GROUPS):
        for h in range(NSA_HPG):
            c = q[:, (g * NSA_HPG + h) * LANES:(g * NSA_HPG + h + 1) * LANES]
            blocks.append(pltpu.roll(c, HEAD_DIM, 1) if g == 1 else c)
    q64 = jnp.concatenate(blocks, axis=0).astype(BF16)

    def pick(x):
        return jnp.where(is_g1, x[:, HEAD_DIM:2 * HEAD_DIM], x[:, 0:HEAD_DIM])

    def pad_keys(x):
        return jnp.concatenate([x, jnp.zeros((2 * tn - x.shape[0], x.shape[1]), x.dtype)], axis=0)

    n_cmp = n_pages * (PAGE // L_CMP)
    kcv = jnp.concatenate([c[0] for c in cmps], axis=0).astype(BF16)
    s_c = _dot_nt(q64, kcv[:, 0:KV_A])
    blk_c = lax.broadcasted_iota(jnp.int32, (1, n_cmp), 1)
    s_c = jnp.where((blk_c + 1) * L_CMP - 1 <= tok, s_c, NEG_BIG)
    e_c = jnp.exp(s_c - jnp.max(s_c, axis=-1, keepdims=True))
    p_c = e_c / jnp.sum(e_c, axis=-1, keepdims=True)
    p_c = jnp.where(tok >= L_CMP - 1, p_c, 0.0)
    o_c = pick(_dot(p_c.astype(BF16), kcv[:, KV_A:2 * KV_A]))

    imps = []
    for g in range(NSA_GROUPS):
        imp = p_c[g * NSA_HPG * tn:g * NSA_HPG * tn + tn]
        for h in range(1, NSA_HPG):
            imp = imp + p_c[(g * NSA_HPG + h) * tn:(g * NSA_HPG + h + 1) * tn]
        imps.append(imp)
    imp2 = jnp.concatenate(imps + [jnp.zeros((LANES - NSA_GROUPS * tn, n_cmp), F32)], axis=0)
    imp_sel = jnp.zeros((LANES, LANES), F32)
    for part in _split3(imp2):
        imp_sel = imp_sel + _dot(part, pair_ref[...])
    tok_row = past + lax.broadcasted_iota(jnp.int32, (1, LANES), 1) % tn
    sel = _select_blocks(imp_sel.T, tok_row).T
    bias2 = jnp.where(sel > 0.0, 0.0, -MASK_BIG)
    bias_sel = jnp.concatenate([bias2[0:tn]] * NSA_HPG + [bias2[tn:2 * tn]] * NSA_HPG, axis=0)

    lane_lo = lax.broadcasted_iota(jnp.int32, (rows, PAGE), 1) < L_SEL
    new_kv = pad_keys(new_ref[:, 2 * KV_A:4 * KV_A]).astype(BF16)
    scores = []
    for j in range(n_pages):
        bias = jnp.where(lane_lo, bias_sel[:, 2 * j:2 * j + 1], bias_sel[:, 2 * j + 1:2 * j + 2])
        scores.append(_dot(q64, pages[j][0, 0, 0:KV_A, :].astype(BF16)) + bias)
    kcol = lax.broadcasted_iota(jnp.int32, (1, 2 * tn), 1)
    s_new = _dot_nt(q64, new_kv[:, 0:KV_A]) + bias_sel[:, 2 * n_pages:2 * n_pages + 1]
    scores.append(jnp.where((kcol <= qi) & (kcol < tn), s_new, -MASK_BIG))
    probs, den = _softmax_pieces(scores)
    acc = _dot(probs[n_pages].astype(BF16), new_kv[:, KV_A:2 * KV_A])
    for j in range(n_pages):
        acc = acc + _dot_nt(probs[j].astype(BF16), pages[j][0, 0, KV_A:2 * KV_A, :].astype(BF16))
    o_s = pick(acc) / den

    wb = state_ref.shape[3]
    st = state_ref[0, 0]
    wnew = wnew_ref[...]
    wn_b = pad_keys(wnew).astype(BF16)
    kpos = past - wb + lax.broadcasted_iota(jnp.int32, (1, wb), 1)
    dpos = tok - kpos
    s1 = jnp.where((dpos >= 0) & (dpos < WINDOW) & (kpos >= 0), _dot(q64, st[0:KV_A].astype(BF16)), NEG_BIG)
    kcol = lax.broadcasted_iota(jnp.int32, (1, 2 * tn), 1)
    s2 = jnp.where((kcol <= qi) & (kcol < tn), _dot_nt(q64, wn_b[:, 0:KV_A]), NEG_BIG)
    (p1, p2), den_w = _softmax_pieces([s1, s2])
    o_w = pick(_dot_nt(p1.astype(BF16), st[KV_A:2 * KV_A].astype(BF16))
               + _dot(p2.astype(BF16), wn_b[:, KV_A:2 * KV_A])) / den_w

    new_t = jnp.concatenate([jnp.zeros((LANES - tn, 2 * KV_A), F32), wnew], axis=0).T
    shifted = pltpu.roll(st, wb - tn, 1)
    wout_ref[0, 0, :, 0:wb - LANES] = shifted[:, 0:wb - LANES]
    lane_w = lax.broadcasted_iota(jnp.int32, (2 * KV_A, LANES), 1)
    wout_ref[0, 0, :, wb - LANES:wb] = jnp.where(lane_w >= LANES - tn, new_t, shifted[:, wb - LANES:wb])

    gates = misc_ref[...]
    pieces = []
    for hh in range(NSA_HEADS):
        hr = slice(hh * tn, (hh + 1) * tn)
        c0 = GATE_LANE0 + hh * 3
        pieces.append(gates[:, c0:c0 + 1] * o_c[hr] + gates[:, c0 + 1:c0 + 2] * o_s[hr]
                      + gates[:, c0 + 2:c0 + 3] * o_w[hr])
    o_ref[...] = jnp.concatenate(pieces, axis=1)


def _nsa_sample(page_table, layer, qa_s, misc_s, nsa_s, win_s, state_t, pair, cache_t, cmp_pool3, oa_full, n_prompt,
                win_all):
    nseq, n_pages = page_table.shape
    depth = state_t.shape[0]
    tn = qa_s.shape[0] // nseq
    past = n_pages * PAGE
    wb = state_t.shape[3]
    ns = SEQ_PER_STEP
    assert nseq % ns == 0 and n_prompt % (ns * tn) == 0
    carried = (oa_full,) if win_all is None else (oa_full, win_all)
    n_carried = len(carried)
    row = lambda width: pl.BlockSpec((ns * tn, width), lambda b, pt: (b, 0))
    page_specs = [pl.BlockSpec((1, 1, 2 * KV_A, PAGE),
                               functools.partial(lambda b, pt, s, j: (layer, pt[ns * b + s, j], 1, 0), s=s, j=j))
                  for s in range(ns) for j in range(n_pages)]
    cmp_specs = [pl.BlockSpec((1, PAGE // L_CMP, 2 * KV_A),
                              functools.partial(lambda b, pt, s, j: (pt[ns * b + s, j], 0, 0), s=s, j=j))
                 for s in range(ns) for j in range(n_pages)]
    grid_spec = pltpu.PrefetchScalarGridSpec(
        num_scalar_prefetch=1,
        grid=(nseq // ns,),
        in_specs=[row(2 * WIDTH_A), row(LANES), row(4 * KV_A), row(2 * KV_A),
                  pl.BlockSpec((1, ns, 2 * KV_A, wb), lambda b, pt: (layer, b, 0, 0)),
                  pl.BlockSpec(pair.shape, lambda b, pt: (0, 0))] + page_specs + cmp_specs
                 + [pl.BlockSpec(memory_space=pl.ANY)] * n_carried,
        out_specs=[pl.BlockSpec((ns * tn, WIDTH_A), lambda b, pt: (n_prompt // (ns * tn) + b, 0)),
                   pl.BlockSpec((1, ns, 2 * KV_A, wb), lambda b, pt: (layer, b, 0, 0))],
    )
    n_paged = ns * n_pages

    def body(pt_ref, q_ref, misc_ref, new_ref, wnew_ref, state_ref, pair_ref, *rest):
        pages, cmps = rest[:n_paged], rest[n_paged:2 * n_paged]
        o_ref, wout_ref = rest[2 * n_paged + n_carried:]
        for s in range(ns):
            rs, one = pl.ds(s * tn, tn), pl.ds(s, 1)
            _nsa_sample_kernel(pt_ref, q_ref.at[rs], misc_ref.at[rs], new_ref.at[rs], wnew_ref.at[rs],
                               state_ref.at[:, one], pair_ref, *pages[s * n_pages:(s + 1) * n_pages],
                               *cmps[s * n_pages:(s + 1) * n_pages], o_ref.at[rs], wout_ref.at[:, one],
                               n_pages=n_pages, past=past)

    n_in = 1 + 6 + 2 * n_paged
    aliases = {n_in: 0}
    if win_all is not None:
        aliases[n_in + 1] = 1
    return pl.pallas_call(
        body,
        grid_spec=grid_spec,
        out_shape=[jax.ShapeDtypeStruct(oa_full.shape, F32),
                   jax.ShapeDtypeStruct((depth, nseq, 2 * KV_A, wb), F32)],
        input_output_aliases=aliases,
        compiler_params=_cparams(("arbitrary",)),
        name="nsa_sample",
    )(page_table, qa_s, misc_s, nsa_s, win_s, state_t, pair, *([cache_t] * n_paged), *([cmp_pool3] * n_paged),
      *carried)


def _fox_sample_kernel(pt_ref, q_ref, new_ref, lfn_ref, ustrict_ref, *rest, n_pages):
    del pt_ref
    pages = rest[:n_pages]
    lfps = rest[n_pages:2 * n_pages]
    o_ref = rest[2 * n_pages]
    tn = q_ref.shape[0]
    rows = FOX_HEADS * tn
    qi = lax.broadcasted_iota(jnp.int32, (rows, 1), 0) % tn

    q = q_ref[...]
    zeros_half = jnp.zeros((tn, LANES), F32)
    blocks = []
    for h in range(FOX_HEADS):
        c = q[:, h * LANES:(h + 1) * LANES]
        if h % 2 == 1:
            c = pltpu.roll(c, HEAD_DIM, 1)
        blocks.append(jnp.concatenate([c, zeros_half] if h < 2 else [zeros_half, c], axis=1))
    q32 = jnp.concatenate(blocks, axis=0).astype(BF16)

    lf = jnp.concatenate([r[0, 0] for r in lfps], axis=0)
    within = jnp.zeros(lf.shape, F32)
    for part in _split3(lf):
        within = within + _dot(part, ustrict_ref[...])
    tot = jnp.sum(lf, axis=-1, keepdims=True)
    after = jnp.zeros((FOX_HEADS, 1), F32)
    page_bias = [None] * n_pages
    for j in reversed(range(n_pages)):
        b4 = within[j * FOX_HEADS:(j + 1) * FOX_HEADS] + after
        page_bias[j] = jnp.concatenate(
            [jnp.broadcast_to(b4[h:h + 1], (tn, PAGE)) for h in range(FOX_HEADS)], axis=0)
        after = after + tot[j * FOX_HEADS:(j + 1) * FOX_HEADS]

    lfn = lfn_ref[0]
    run = lfn[:, 0:1]
    cols = [run]
    for r in range(1, tn):
        run = run + lfn[:, r:r + 1]
        cols.append(run)
    cum = jnp.concatenate(cols + [jnp.zeros((FOX_HEADS, tn), F32)], axis=1)
    new_bias = jnp.concatenate(
        [jnp.broadcast_to(-cum[h:h + 1], (tn, 2 * tn)) for h in range(FOX_HEADS)], axis=0)

    new = jnp.concatenate([new_ref[...], jnp.zeros((tn, 2 * WIDTH_B), F32)], axis=0).astype(BF16)
    scores = [_dot(q32, pages[j][0, 0, 0:WIDTH_B, :].astype(BF16)) + page_bias[j] for j in range(n_pages)]
    kcol = lax.broadcasted_iota(jnp.int32, (1, 2 * tn), 1)
    scores.append(jnp.where((kcol <= qi) & (kcol < tn), _dot_nt(q32, new[:, :WIDTH_B]) + new_bias, -MASK_BIG))
    probs, den = _softmax_pieces(scores)
    acc = _dot(probs[n_pages].astype(BF16), new[:, WIDTH_B:])
    for j in range(n_pages):
        acc = acc + _dot_nt(probs[j].astype(BF16), pages[j][0, 0, WIDTH_B:2 * WIDTH_B, :].astype(BF16))
    o32 = acc / den
    lane_head = lax.broadcasted_iota(jnp.int32, (tn, WIDTH_B), 1) // HEAD_DIM
    out = jnp.zeros((tn, WIDTH_B), F32)
    for h in range(FOX_HEADS):
        out = out + jnp.where(lane_head == h, o32[h * tn:(h + 1) * tn], 0.0)
    o_ref[...] = out


def _fox_sample(page_table, layer, qb_s, fox_s, lfn_t, ustrict, cache_t, logf_t, ob_full, n_prompt):
    nseq, n_pages = page_table.shape
    tn = qb_s.shape[0] // nseq
    ns = SEQ_PER_STEP
    assert nseq % ns == 0 and n_prompt % (ns * tn) == 0
    row = lambda width: pl.BlockSpec((ns * tn, width), lambda b, pt: (b, 0))
    page_specs = [pl.BlockSpec((1, 1, 2 * WIDTH_B, PAGE),
                               functools.partial(lambda b, pt, s, j: (layer, pt[ns * b + s, j], 0, 0), s=s, j=j))
                  for s in range(ns) for j in range(n_pages)]
    lf_specs = [pl.BlockSpec((1, 1, FOX_HEADS, PAGE),
                             functools.partial(lambda b, pt, s, j: (layer, pt[ns * b + s, j], 0, 0), s=s, j=j))
                for s in range(ns) for j in range(n_pages)]
    grid_spec = pltpu.PrefetchScalarGridSpec(
        num_scalar_prefetch=1,
        grid=(nseq // ns,),
        in_specs=[row(2 * WIDTH_B), row(2 * WIDTH_B),
                  pl.BlockSpec((ns, FOX_HEADS, tn), lambda b, pt: (b, 0, 0)),
                  pl.BlockSpec(ustrict.shape, lambda b, pt: (0, 0))] + page_specs + lf_specs
                 + [pl.BlockSpec(memory_space=pl.ANY)],
        out_specs=pl.BlockSpec((ns * tn, WIDTH_B), lambda b, pt: (n_prompt // (ns * tn) + b, 0)),
    )
    n_paged = ns * n_pages

    def body(pt_ref, q_ref, new_ref, lfn_ref, ustrict_ref, *rest):
        pages, lfps = rest[:n_paged], rest[n_paged:2 * n_paged]
        o_ref = rest[2 * n_paged + 1]
        for s in range(ns):
            rs = pl.ds(s * tn, tn)
            _fox_sample_kernel(pt_ref, q_ref.at[rs], new_ref.at[rs], lfn_ref.at[pl.ds(s, 1)], ustrict_ref,
                               *pages[s * n_pages:(s + 1) * n_pages], *lfps[s * n_pages:(s + 1) * n_pages],
                               o_ref.at[rs], n_pages=n_pages)

    n_in = 1 + 4 + 2 * n_paged
    return pl.pallas_call(
        body,
        grid_spec=grid_spec,
        out_shape=jax.ShapeDtypeStruct(ob_full.shape, F32),
        input_output_aliases={n_in: 0},
        compiler_params=_cparams(("arbitrary",)),
        name="fox_sample",
    )(page_table, qb_s, fox_s, lfn_t, ustrict, *([cache_t] * n_paged), *([logf_t] * n_paged), ob_full)


def _in_offsets():
    splits = (WIDTH_A, KV_A, KV_A, KV_A, KV_A, KV_A, KV_A, 3 * NSA_HEADS, WIDTH_B, WIDTH_B, WIDTH_B, FOX_HEADS,
              GM_WIDTH, GM_WIDTH)
    offs = np.concatenate([[0], np.cumsum(splits)])
    names = ("qa", "kc", "vc", "ks", "vs", "kw", "vw", "ga", "qb", "kb", "vb", "fl", "gu", "gv", "mg")
    return {n: int(o) for n, o in zip(names, offs)}


def _pack_params(w_in, qk_gain, gm_norm, b_forget, d_model):
    depth = w_in.shape[0]
    o = _in_offsets()
    cols = np.concatenate([
        np.arange(o["qa"], o["qa"] + WIDTH_A),
        np.arange(o["qb"], o["qb"] + WIDTH_B),
        np.arange(o["gu"], o["gu"] + 2 * GM_WIDTH),
        np.arange(o["fl"], o["fl"] + FOX_HEADS),
        np.arange(o["ga"], o["ga"] + 3 * NSA_HEADS)])
    w_p = jnp.take(w_in, jnp.asarray(cols), axis=2)
    w_p = jnp.pad(w_p, ((0, 0), (0, 0), (0, P_COLS - w_p.shape[2]))).astype(BF16)
    rows_t = np.concatenate([
        np.arange(o["kc"], o["kc"] + 6 * KV_A),
        np.arange(o["kb"], o["kb"] + 2 * WIDTH_B),
        np.arange(o["fl"], o["fl"] + FOX_HEADS)])
    w_t = jnp.swapaxes(jnp.take(w_in, jnp.asarray(rows_t), axis=2), 1, 2)
    w_t = jnp.pad(w_t, ((0, 0), (0, T_ROWS - w_t.shape[1]), (0, 0))).astype(BF16)
    w_mg = w_in[:, :, o["mg"]:o["mg"] + 3 * d_model].astype(BF16)
    tile = lambda g, n: jnp.tile(g, (1, n))
    prm = jnp.concatenate([
        tile(qk_gain[:, 0], NSA_HEADS), tile(qk_gain[:, 4], FOX_HEADS), gm_norm,
        b_forget, jnp.zeros((depth, LANES - FOX_HEADS), F32)], axis=1)
    assert prm.shape[1] == R_COLS
    ones = jnp.ones((depth, KV_A), F32)
    col = jnp.concatenate([
        tile(qk_gain[:, 1], NSA_GROUPS), ones, tile(qk_gain[:, 2], NSA_GROUPS), ones,
        tile(qk_gain[:, 3], NSA_GROUPS), ones, tile(qk_gain[:, 5], FOX_HEADS), jnp.ones((depth, WIDTH_B), F32),
        b_forget, jnp.zeros((depth, SUBLANES - FOX_HEADS), F32)], axis=1)
    assert col.shape[1] == T_ROWS
    return w_p, w_t, w_mg, prm[:, None, :], col[:, :, None]


def _rope_tables(pos):
    half = ROT_HALF
    inv = ROPE_THETA ** (-jnp.arange(half, dtype=F32) / half)
    ang = pos.astype(F32)[:, None] * inv[None, :]
    cos, sin = jnp.cos(ang), jnp.sin(ang)
    n = pos.shape[0]
    zero = jnp.zeros((n, HEAD_DIM - ROT_DIM), F32)
    zero8 = jnp.zeros((n, half), F32)
    c64 = jnp.concatenate([cos, cos, zero + 1.0], axis=1)
    lo64 = jnp.concatenate([-sin, zero8, zero], axis=1)
    hi64 = jnp.concatenate([zero8, sin, zero], axis=1)
    rep = lambda t: jnp.tile(t, (1, LANES // HEAD_DIM))
    token_major = jnp.concatenate([rep(c64), rep(lo64), rep(hi64)], axis=1)
    feature_major = jnp.concatenate([cos.T, sin.T], axis=0)
    return token_major, feature_major


def _gmlp_tables(tn, w_spatial, b_spatial):
    w_p = jnp.tril(w_spatial)
    eye = jnp.asarray(np.eye(CHUNK // tn), F32)
    w_s = jnp.einsum("ab,lgts->lgatbs", eye, jnp.tril(w_spatial[:, :, :tn, :tn]))
    w_s = w_s.reshape(w_spatial.shape[0], GM_GROUPS, CHUNK, CHUNK)
    wmix = jnp.stack([w_p, w_s], axis=1).astype(BF16)
    b_p = jnp.repeat(jnp.swapaxes(b_spatial, 1, 2), GM_DIM, axis=2)
    b_s = jnp.tile(b_p[:, :tn], (1, CHUNK // tn, 1))
    btab = jnp.stack([b_p, b_s], axis=1)
    return wmix, btab


def _compress_weights(w_cmp, pe_cmp):
    depth = w_cmp.shape[0]
    eye_g = jnp.asarray(np.eye(NSA_GROUPS), F32)
    w_l = jnp.einsum("kq,gh,zklde->zlkgdqhe", jnp.asarray(np.eye(2), F32), eye_g, w_cmp)
    w_l = w_l.reshape(depth, L_CMP, 2 * KV_A, 2 * KV_A)
    pe_l = jnp.broadcast_to(pe_cmp[:, :, :, None, :], (depth, 2, L_CMP, NSA_GROUPS, HEAD_DIM))
    pe_l = jnp.transpose(pe_l, (0, 2, 1, 3, 4)).reshape(depth, L_CMP, 2 * KV_A)
    return w_l.astype(BF16), pe_l


def _group_major(x):
    s = x.shape[:-1]
    return jnp.swapaxes(x.reshape(s + (2, NSA_GROUPS, HEAD_DIM)), -3, -2).reshape(s + (2 * KV_A,))


def _feature_major(x, n_feat):
    nd = x.ndim
    perm = tuple(range(nd - 4)) + (nd - 3, nd - 2, nd - 1, nd - 4)
    y = jnp.transpose(x, perm)
    return y.reshape(y.shape[:nd - 4] + (n_feat, y.shape[-1]))


def _token_major_view(y, dims):
    lead = y.shape[:-2]
    z = y.reshape(lead + tuple(dims) + (y.shape[-1],))
    nd = z.ndim
    perm = tuple(range(nd - 4)) + (nd - 1, nd - 4, nd - 3, nd - 2)
    return jnp.transpose(z, perm)


def kernel(x_prompt, x_sample, cache_nsa_kv, cache_fox_kv, cache_fox_logf, state_nsa_win, page_table,
           g_ffn_a, w_ffn_a_gu, w_ffn_a_down, g_mix, w_in, b_forget, qk_gain, w_cmp, pe_cmp,
           gm_norm, w_spatial, b_spatial, w_branch_a, w_branch_b, w_branch_c, w_out,
           g_ffn_b, w_ffn_b_gu, w_ffn_b_down):
    batch, seq, d_model = x_prompt.shape
    nseq, tn, _ = x_sample.shape
    depth = w_in.shape[0]
    n_pool = cache_nsa_kv.shape[1]
    n_pages = page_table.shape[1]
    past = n_pages * PAGE
    wb = state_nsa_win.shape[2]
    n_prompt = batch * seq
    n_sample = nseq * tn
    tm = next(t for t in (512, 256, 128) if seq % t == 0 and n_sample % t == 0)
    assert seq % FOX_Q_BLOCK == 0 or seq < FOX_Q_BLOCK
    assert seq >= WINDOW + Q_BLOCK and wb % LANES == 0 and wb > LANES
    assert CHUNK % tn == 0 and past % L_SEL == 0 and wb >= tn

    x = jnp.concatenate([x_prompt.reshape(n_prompt, d_model), x_sample.reshape(n_sample, d_model)], axis=0)
    pos = jnp.concatenate([jnp.tile(jnp.arange(seq), batch), jnp.tile(past + jnp.arange(tn), nseq)])
    rope, rope_t = _rope_tables(pos)

    w_p, w_t, w_mg, prm, col = _pack_params(w_in, qk_gain, gm_norm, b_forget, d_model)
    bd = jnp.asarray(np.kron(np.eye(LANES // HEAD_DIM), np.full((HEAD_DIM, HEAD_DIM), 1.0 / HEAD_DIM)), BF16)
    wmix, btab = _gmlp_tables(tn, w_spatial, b_spatial)
    w_l, pe_l = _compress_weights(w_cmp, pe_cmp)
    w2 = _group_major(w_l.reshape(depth, L_CMP * 2 * KV_A, 2 * KV_A))
    pe2 = pe_l.reshape(depth, 1, L_CMP * 2 * KV_A)
    n_cmp_p = seq // L_CMP

    bf = lambda a: a.astype(BF16)
    wgu_a, wd_a, wgu_b, wd_b = bf(w_ffn_a_gu), bf(w_ffn_a_down), bf(w_ffn_b_gu), bf(w_ffn_b_down)
    wa, wb_, wc, wo = bf(w_branch_a), bf(w_branch_b), bf(w_branch_c), bf(w_out)

    cache_nsa_t = _feature_major(cache_nsa_kv, 4 * KV_A)
    cache_fox_t = _feature_major(cache_fox_kv, 2 * WIDTH_B)
    state_t = _feature_major(state_nsa_win, 2 * KV_A)
    logf_t = jnp.swapaxes(cache_fox_logf, 2, 3)
    nb = next(t for t in (64, 32, 16, 8, 4, 2, 1) if n_pool % t == 0)
    n_sel = past // L_SEL + 1
    pair = np.zeros((n_pages * (PAGE // L_CMP), LANES))
    for c in range(pair.shape[0]):
        pair[c, c // 2] = 1.0
    pair = jnp.asarray(pair, BF16)
    ustrict = jnp.asarray(np.triu(np.ones((PAGE, PAGE)), 1).T, BF16)
    assert n_sel <= LANES

    outs = {k: [] for k in ("nsas", "wint", "foxs", "logft", "logfs", "gmv")}
    stacked = None
    win_all = None
    for l in range(depth):
        x = _ffn(x, g_ffn_a[l][None], wgu_a[l], wd_a[l], tm)
        pr = _proj(x, g_mix[l][None], w_p[l], w_t[l], prm[l], col[l], rope, rope_t, bd, wmix[l], btab[l],
                   tm, batch, seq, n_sample, l, depth, stacked)
        stacked = (pr["nsat"], pr["foxt"])

        blocks = pr["kcvc"].reshape(batch, n_cmp_p // 2, 2, L_CMP * 2 * KV_A)
        x2 = jnp.swapaxes(blocks, 1, 2).reshape(batch, n_cmp_p, L_CMP * 2 * KV_A)
        kcv = _compress_prompt(x2, pe2[l], w2[l])
        oa = _nsa_prompt(pr["qa"], pr["misc"], kcv, pr["nsat"], pr["wint"], batch, seq, l)
        ob = _fox_prompt(pr["qb"], pr["foxt"], pr["faug"], batch, seq, l)

        cmp_pool = _compress_pool(cache_nsa_t, l, pe_l[l], w_l[l], nb)
        cmp_pool3 = cmp_pool.reshape(n_pool, PAGE // L_CMP, 2 * KV_A)
        sl = slice(n_prompt, None)
        oa, win_all = _nsa_sample(page_table, l, pr["qa"][sl].astype(F32), pr["misc"][sl], pr["nsas"],
                                  pr["wins"], state_t, pair, cache_nsa_t, cmp_pool3, oa, n_prompt, win_all)
        lfn_t = jnp.swapaxes(pr["misc"][sl, :FOX_HEADS].reshape(nseq, tn, FOX_HEADS), 1, 2)
        ob = _fox_sample(page_table, l, pr["qb"][sl].astype(F32), pr["foxs"], lfn_t, ustrict, cache_fox_t,
                         logf_t, ob, n_prompt)

        x = _mix(x, g_mix[l][None], w_mg[l], oa, ob, pr["oc"], wa[l], wb_[l], wc[l], wo[l], tm)
        x = _ffn(x, g_ffn_b[l][None], wgu_b[l], wd_b[l], tm)

        outs["nsas"].append(pr["nsas"])
        outs["wint"].append(pr["wint"][:, :, seq - min(WINDOW, seq):])
        outs["foxs"].append(pr["foxs"])
        outs["logft"].append(pr["logft"][:, :FOX_HEADS])
        outs["logfs"].append(pr["misc"][sl, :FOX_HEADS])
        outs["gmv"].append(pr["vs"])

    st = {k: jnp.stack(v) for k, v in outs.items()}
    st["nsat"], st["foxt"] = stacked
    st["win_s"] = win_all
    return (
        x[:n_prompt].reshape(batch, seq, d_model),
        x[n_prompt:].reshape(nseq, tn, d_model),
        _token_major_view(st["nsat"], (4, NSA_GROUPS, HEAD_DIM)),
        st["nsas"].reshape(depth, nseq, tn, 4, NSA_GROUPS, HEAD_DIM),
        _token_major_view(st["wint"], (2, NSA_GROUPS, HEAD_DIM)),
        _token_major_view(st["win_s"], (2, NSA_GROUPS, HEAD_DIM)),
        _token_major_view(st["foxt"], (2, FOX_HEADS, HEAD_DIM)),
        st["foxs"].reshape(depth, nseq, tn, 2, FOX_HEADS, HEAD_DIM),
        jnp.swapaxes(st["logft"], 2, 3),
        st["logfs"].reshape(depth, nseq, tn, FOX_HEADS),
        st["gmv"].reshape(depth, nseq, tn, GM_WIDTH),
    )
```
